```python
import jax
import jax.numpy as jnp
from jax import lax
import numpy as np

D_MODEL = 1024
BATCH = 32
SEQ = 2048
DEPTH = 2

HEAD_DIM = 64
NORM_EPS = 1e-6
A_HEADS = 8
A_WIDTH = A_HEADS * HEAD_DIM
A_PATTERNS = ((128, 1), (512, 4), (2048, 16))
A_ROT_DIMS = HEAD_DIM // 4
A_ROPE_THETA = 500000.0
B_Q_HEADS = 8
B_KV_HEADS = 2
B_Q_WIDTH = B_Q_HEADS * HEAD_DIM
B_KV_WIDTH = B_KV_HEADS * HEAD_DIM
B_ROPE_THETA = 10000.0
GRID_W = 64
Q_BLOCK = 128
IN_SPLITS = (A_WIDTH, A_WIDTH, A_WIDTH, B_Q_WIDTH, B_KV_WIDTH, B_KV_WIDTH, D_MODEL, D_MODEL)
IN_COLS = sum(IN_SPLITS)
IN_OFFSETS = tuple(int(v) for v in np.cumsum(IN_SPLITS)[:-1])
N_EXPERTS = 32
N_GROUPS = 4
EXPERTS_PER_GROUP = N_EXPERTS // N_GROUPS
TOP_K = 2
D_FF_EXPERT = D_MODEL // 2
MOE_BLOCK = 256

kernel_name = "hybrid_dilated_gqa_grouped_moe_block"


def rms_norm(x, g):
    xf = x.astype(jnp.float32)
    y = xf * lax.rsqrt(jnp.mean(xf * xf, axis=-1, keepdims=True) + NORM_EPS)
    return (y * g.astype(jnp.float32)).astype(x.dtype)


def modulate(h, shift, scale):
    return h * (1 + scale) + shift


def rope(x, pos, theta):
    half = x.shape[-1] // 2
    freqs = theta ** (-jnp.arange(half, dtype=jnp.float32) / half)
    ang = pos.astype(jnp.float32)[:, None] * freqs
    cos = jnp.cos(ang)[:, None, :]
    sin = jnp.sin(ang)[:, None, :]
    xf = x.astype(jnp.float32)
    x1, x2 = xf[..., :half], xf[..., half:]
    return jnp.concatenate([x1 * cos - x2 * sin, x2 * cos + x1 * sin], axis=-1).astype(x.dtype)


def partial_rope(x, pos):
    return jnp.concatenate([rope(x[..., :A_ROT_DIMS], pos, A_ROPE_THETA), x[..., A_ROT_DIMS:]], axis=-1)


def axial_rope(x, row, col):
    half = x.shape[-1] // 2
    return jnp.concatenate([rope(x[..., :half], row, B_ROPE_THETA), rope(x[..., half:], col, B_ROPE_THETA)], axis=-1)


def dilated_window_attention(q, k, v, dilation, radius):
    B, S, H, E = q.shape
    L = S // dilation
    nb = -(-L // radius)
    Lp = nb * radius

    def strided(a, lo, hi):
        a = a.reshape(B, L, dilation, H, E).transpose(0, 2, 1, 3, 4)
        return jnp.pad(a, ((0, 0), (0, 0), (lo, hi), (0, 0), (0, 0)))

    qs = strided(q, 0, Lp - L).reshape(B, dilation, nb, radius, H, E)

    def band(a):
        a = strided(a, radius, Lp - L + radius).reshape(B, dilation, nb + 2, radius, H, E)
        return jnp.concatenate([a[:, :, :-2], a[:, :, 1:-1], a[:, :, 2:]], axis=3)

    kw, vw = band(k), band(v)
    s = jnp.einsum('bdnqhe,bdnkhe->bdnhqk', qs, kw).astype(jnp.float32) * (E ** -0.5)
    n_idx = jnp.arange(nb)[:, None, None]
    qi = jnp.arange(radius)[None, :, None]
    ki = jnp.arange(3 * radius)[None, None, :]
    kpos = n_idx * radius - radius + ki
    valid = (jnp.abs(ki - radius - qi) <= radius) & (kpos >= 0) & (kpos < L)
    s = jnp.where(valid[:, None], s, -jnp.inf)
    lse = jax.nn.logsumexp(s, axis=-1, keepdims=True)
    p = jnp.exp(s - lse).astype(v.dtype)
    o = jnp.einsum('bdnhqk,bdnkhe->bdnqhe', p, vw).reshape(B, dilation, Lp, H, E)[:, :, :L]
    o = o.transpose(0, 2, 1, 3, 4).reshape(B, S, H, E)
    lse = lse[..., 0].transpose(0, 1, 2, 4, 3).reshape(B, dilation, Lp, H)[:, :, :L]
    lse = lse.transpose(0, 2, 1, 3).reshape(B, S, H)
    return o, lse


def gqa_attention(q, k, v):
    B, S, HQ, E = q.shape
    G = HQ // B_KV_HEADS
    nq = S // Q_BLOCK
    qb = q.reshape(B, nq, Q_BLOCK, B_KV_HEADS, G, E).transpose(1, 0, 2, 3, 4, 5)

    def block(qblk):
        s = jnp.einsum('bqkge,bske->bkgqs', qblk, k).astype(jnp.float32) * (E ** -0.5)
        p = jax.nn.softmax(s, axis=-1).astype(v.dtype)
        return jnp.einsum('bkgqs,bske->bqkge', p, v)

    o = lax.map(block, qb)
    return o.transpose(1, 0, 2, 3, 4, 5).reshape(B, S, HQ * E)


def hybrid_mixer(h, w_in, qn_a, kn_a, qn_b, kn_b, w_pa, w_pb, w_out, pos, row, col):
    B, S, _ = h.shape
    z = h @ w_in
    qa, ka, va, qb, kb, vb, ga, gb = jnp.split(z, IN_OFFSETS, axis=-1)

    def heads(a):
        return a.reshape(B, S, -1, HEAD_DIM)

    qa = partial_rope(rms_norm(heads(qa), qn_a), pos)
    ka = partial_rope(rms_norm(heads(ka), kn_a), pos)
    va = heads(va)
    outs, lses = [], []
    for window, dil in A_PATTERNS:
        o, lse = dilated_window_attention(qa, ka, va, dil, window // (2 * dil))
        outs.append(o)
        lses.append(lse)
    alpha = jax.nn.softmax(jnp.stack(lses), axis=0)
    ya = jnp.einsum('pbsh,pbshe->bshe', alpha, jnp.stack(outs).astype(jnp.float32))
    ya = ya.astype(h.dtype).reshape(B, S, A_WIDTH)

    qb = axial_rope(rms_norm(heads(qb), qn_b), row, col)
    kb = axial_rope(rms_norm(heads(kb), kn_b), row, col)
    yb = gqa_attention(qb, kb, heads(vb))

    merged = jax.nn.sigmoid(ga) * (ya @ w_pa) + jax.nn.sigmoid(gb) * (yb @ w_pb)
    return merged @ w_out


def grouped_moe(h, w_router, w_gate, w_up, w_down):
    B, S, D = h.shape
    T = B * S
    hf = h.reshape(T, D)
    probs = jax.nn.softmax((hf @ w_router).astype(jnp.float32), axis=-1)
    pg = probs.reshape(T, N_GROUPS, EXPERTS_PER_GROUP)
    group_score = lax.top_k(pg, TOP_K)[0].sum(-1)
    g_sel = jnp.argmax(group_score, axis=-1).astype(jnp.int32)
    in_group = pg[jnp.arange(T), g_sel]
    top_w, top_j = lax.top_k(in_group, TOP_K)
    top_e = g_sel[:, None] * EXPERTS_PER_GROUP + top_j.astype(jnp.int32)
    top_w = top_w / jnp.sum(top_w, axis=-1, keepdims=True)

    A = T * TOP_K
    e_flat = top_e.reshape(A)
    tok = jnp.repeat(jnp.arange(T, dtype=jnp.int32), TOP_K)
    wt = top_w.reshape(A)
    order = jnp.argsort(e_flat)
    e_sorted = e_flat[order]
    counts = jnp.zeros((N_EXPERTS,), jnp.int32).at[e_flat].add(1)
    padded = (counts + MOE_BLOCK - 1) // MOE_BLOCK * MOE_BLOCK
    pad_end = jnp.cumsum(padded)
    pad_start = pad_end - padded
    start = jnp.cumsum(counts) - counts
    dest = pad_start[e_sorted] + jnp.arange(A, dtype=jnp.int32) - start[e_sorted]
    n_pad = -(-(A + N_EXPERTS * MOE_BLOCK) // MOE_BLOCK) * MOE_BLOCK
    n_blocks = n_pad // MOE_BLOCK
    buf_tok = jnp.full((n_pad,), T, jnp.int32).at[dest].set(tok[order])
    buf_w = jnp.zeros((n_pad,), jnp.float32).at[dest].set(wt[order])
    blk_e = jnp.minimum(jnp.searchsorted(pad_end, jnp.arange(n_blocks) * MOE_BLOCK, side='right'),
                        N_EXPERTS - 1)
    xb = jnp.concatenate([hf, jnp.zeros((1, D), hf.dtype)], axis=0)[buf_tok]
    xb = xb.reshape(n_blocks, MOE_BLOCK, D)

    def expert_block(args):
        xblk, e = args
        return (jax.nn.silu(xblk @ w_gate[e]) * (xblk @ w_up[e])) @ w_down[e]

    yb = lax.map(expert_block, (xb, blk_e)).reshape(n_pad, D)
    out = jnp.zeros((T + 1, D), h.dtype).at[buf_tok].add(yb * buf_w[:, None].astype(h.dtype))
    return out[:T].reshape(B, S, D)


def setup_inputs(seed: int = 0) -> dict:
    key = jax.random.key(seed)
    ks = jax.random.split(key, 18)

    def nrm(k, shape, scale):
        return jax.random.normal(k, shape, jnp.float32) * scale

    return {
        "x": nrm(ks[0], (BATCH, SEQ, D_MODEL), 1.0),
        "c": nrm(ks[1], (BATCH, D_MODEL), 1.0),
        "w_ada": nrm(ks[2], (DEPTH, D_MODEL, 6 * D_MODEL), 0.5 * D_MODEL ** -0.5),
        "b_ada": nrm(ks[3], (DEPTH, 6 * D_MODEL), 0.02),
        "g_mix": 1.0 + nrm(ks[4], (DEPTH, D_MODEL), 0.02),
        "w_in": nrm(ks[5], (DEPTH, D_MODEL, IN_COLS), D_MODEL ** -0.5),
        "qn_a": 1.0 + nrm(ks[6], (DEPTH, HEAD_DIM), 0.02),
        "kn_a": 1.0 + nrm(ks[7], (DEPTH, HEAD_DIM), 0.02),
        "qn_b": 1.0 + nrm(ks[8], (DEPTH, HEAD_DIM), 0.02),
        "kn_b": 1.0 + nrm(ks[9], (DEPTH, HEAD_DIM), 0.02),
        "w_pa": nrm(ks[10], (DEPTH, A_WIDTH, D_MODEL), A_WIDTH ** -0.5),
        "w_pb": nrm(ks[11], (DEPTH, B_Q_WIDTH, D_MODEL), B_Q_WIDTH ** -0.5),
        "w_out": nrm(ks[12], (DEPTH, D_MODEL, D_MODEL), D_MODEL ** -0.5),
        "g_ffn": 1.0 + nrm(ks[13], (DEPTH, D_MODEL), 0.02),
        "w_router": nrm(ks[14], (D_MODEL, N_EXPERTS), D_MODEL ** -0.5),
        "w_gate": nrm(ks[15], (DEPTH, N_EXPERTS, D_MODEL, D_FF_EXPERT), D_MODEL ** -0.5),
        "w_up": nrm(ks[16], (DEPTH, N_EXPERTS, D_MODEL, D_FF_EXPERT), D_MODEL ** -0.5),
        "w_down": nrm(ks[17], (DEPTH, N_EXPERTS, D_FF_EXPERT, D_MODEL), D_FF_EXPERT ** -0.5),
    }


def reference(x, c, w_ada, b_ada, g_mix, w_in, qn_a, kn_a, qn_b, kn_b, w_pa, w_pb, w_out,
              g_ffn, w_router, w_gate, w_up, w_down):
    B, S, D = x.shape
    pos = jnp.arange(S, dtype=jnp.int32)
    rows = S // GRID_W
    row = jnp.repeat(jnp.arange(rows, dtype=jnp.int32), GRID_W)
    col = jnp.tile(jnp.arange(GRID_W, dtype=jnp.int32), rows)
    cond = jax.nn.silu(c)
    for l in range(DEPTH):
        mod = (cond @ w_ada[l] + b_ada[l])[:, None, :]
        sh1, sc1, gt1, sh2, sc2, gt2 = jnp.split(mod, 6, axis=-1)
        h = modulate(rms_norm(x, g_mix[l]), sh1, sc1)
        x = x + gt1 * hybrid_mixer(h, w_in[l], qn_a[l], kn_a[l], qn_b[l], kn_b[l],
                                   w_pa[l], w_pb[l], w_out[l], pos, row, col)
        h = modulate(rms_norm(x, g_ffn[l]), sh2, sc2)
        x = x + gt2 * grouped_moe(h, w_router, w_gate[l], w_up[l], w_down[l])
    return x
```

```python
import functools

import jax
import jax.numpy as jnp
import numpy as np
from jax import lax
from jax.experimental import pallas as pl
from jax.experimental.pallas import tpu as pltpu

D_MODEL = 1024
HEAD_DIM = 64
NORM_EPS = 1e-6
A_HEADS = 8
A_WIDTH = A_HEADS * HEAD_DIM
A_PATTERNS = ((128, 1), (512, 4), (2048, 16))
A_ROT_DIMS = HEAD_DIM // 4
A_ROPE_THETA = 500000.0
B_Q_HEADS = 8
B_KV_HEADS = 2
B_Q_WIDTH = B_Q_HEADS * HEAD_DIM
B_KV_WIDTH = B_KV_HEADS * HEAD_DIM
B_ROPE_THETA = 10000.0
GRID_W = 64
N_EXPERTS = 32
N_GROUPS = 4
EXPERTS_PER_GROUP = N_EXPERTS // N_GROUPS
TOP_K = 2
D_FF_EXPERT = D_MODEL // 2
MOE_BLOCK = 256

LANES = 128
SUBLANES = 8
ROW_CHUNKS = D_MODEL // LANES
VMEM_LIMIT = 56 * 1024 * 1024

F32 = jnp.float32
BF16 = jnp.bfloat16


def _cparams(sem):
    return pltpu.CompilerParams(dimension_semantics=sem, vmem_limit_bytes=VMEM_LIMIT)


def _adaln_kernel(c_ref, w_ref, b_ref, o_ref):
    c = c_ref[...]
    cond = c * jax.nn.sigmoid(c)
    o_ref[0] = jnp.dot(cond, w_ref[0], preferred_element_type=F32,
                       precision=lax.Precision.HIGHEST) + b_ref[0]


def _adaln(c, w_ada, b_ada):
    depth, d, n = w_ada.shape
    bsz = c.shape[0]
    tn = 1024
    return pl.pallas_call(
        _adaln_kernel,
        out_shape=jax.ShapeDtypeStruct((depth, bsz, n), F32),
        grid=(depth, n // tn),
        in_specs=[
            pl.BlockSpec((bsz, d), lambda l, j: (0, 0)),
            pl.BlockSpec((1, d, tn), lambda l, j: (l, 0, j)),
            pl.BlockSpec((1, 1, tn), lambda l, j: (l, 0, j)),
        ],
        out_specs=pl.BlockSpec((1, bsz, tn), lambda l, j: (l, 0, j)),
        compiler_params=_cparams(("arbitrary", "arbitrary")),
        name="adaln",
    )(c, w_ada, b_ada.reshape(depth, 1, n))


def _rope_tables(seq):
    pos = jnp.arange(seq, dtype=F32)
    row = jnp.floor(pos / GRID_W)
    col = pos - row * GRID_W
    d = np.arange(LANES) % HEAD_DIM

    def build(segments):
        c = jnp.ones((seq, LANES), F32)
        s1 = jnp.zeros((seq, LANES), F32)
        s2 = jnp.zeros((seq, LANES), F32)
        for lo, half, theta, p in segments:
            first = (d >= lo) & (d < lo + half)
            second = (d >= lo + half) & (d < lo + 2 * half)
            idx = np.where(first, d - lo, np.where(second, d - lo - half, 0))
            freqs = theta ** (-jnp.arange(half, dtype=F32) / half)
            ang = p[:, None] * freqs[idx][None, :]
            cs, sn = jnp.cos(ang), jnp.sin(ang)
            rot = jnp.asarray(first | second)[None, :]
            c = jnp.where(rot, cs, c)
            s1 = jnp.where(jnp.asarray(first)[None, :], -sn, s1)
            s2 = jnp.where(jnp.asarray(second)[None, :], sn, s2)
        return c, s1, s2

    ta = build([(0, A_ROT_DIMS // 2, A_ROPE_THETA, pos)])
    q = HEAD_DIM // 4
    tb = build([(0, q, B_ROPE_THETA, row), (2 * q, q, B_ROPE_THETA, col)])
    return ta + tb


def _in_proj_kernel(x_ref, sh_ref, sc_ref, g_ref, w_ref, bd_ref,
                    gqa_ref, gka_ref, gqb_ref, gkb_ref,
                    ca_ref, s1a_ref, s2a_ref, cb_ref, s1b_ref, s2b_ref,
                    za_ref, qb_ref, kvb_ref, ga_ref, gb_ref):
    x = x_ref[0]
    ms = jnp.mean(x * x, axis=-1, keepdims=True)
    h = x * lax.rsqrt(ms + NORM_EPS) * g_ref[...]
    h = h * (1.0 + sc_ref[0]) + sh_ref[0]
    hb = h.astype(BF16)

    def seg(lo, width):
        return jnp.dot(hb, w_ref[:, lo:lo + width], preferred_element_type=F32)

    def qk_norm(z, gain_ref):
        width = z.shape[-1]
        ss = jnp.dot((z * z).astype(BF16), bd_ref[:width, :width],
                     preferred_element_type=F32)
        return z * lax.rsqrt(ss * (1.0 / HEAD_DIM) + NORM_EPS) * gain_ref[...]

    def tile(t, width):
        reps = width // LANES
        return t if reps == 1 else jnp.concatenate([t] * reps, axis=-1)

    def rope(z, c_ref, s1_ref, s2_ref, half):
        width = z.shape[-1]
        up = pltpu.roll(z, width - half, 1)
        dn = pltpu.roll(z, half, 1)
        return (z * tile(c_ref[...], width) + up * tile(s1_ref[...], width)
                + dn * tile(s2_ref[...], width))

    ha = A_ROT_DIMS // 2
    hq = HEAD_DIM // 4
    o = 0
    qa = rope(qk_norm(seg(o, A_WIDTH), gqa_ref), ca_ref, s1a_ref, s2a_ref, ha)
    za_ref[0, :, 0:A_WIDTH] = qa
    o += A_WIDTH
    ka = rope(qk_norm(seg(o, A_WIDTH), gka_ref), ca_ref, s1a_ref, s2a_ref, ha)
    za_ref[0, :, A_WIDTH:2 * A_WIDTH] = ka
    o += A_WIDTH
    za_ref[0, :, 2 * A_WIDTH:3 * A_WIDTH] = seg(o, A_WIDTH)
    o += A_WIDTH
    qb = rope(qk_norm(seg(o, B_Q_WIDTH), gqb_ref), cb_ref, s1b_ref, s2b_ref, hq)
    qb_ref[0] = qb.astype(BF16)
    o += B_Q_WIDTH
    kb = rope(qk_norm(seg(o, B_KV_WIDTH), gkb_ref), cb_ref, s1b_ref, s2b_ref, hq)
    kvb_ref[0, :, 0:B_KV_WIDTH] = kb.astype(BF16)
    o += B_KV_WIDTH
    kvb_ref[0, :, B_KV_WIDTH:2 * B_KV_WIDTH] = seg(o, B_KV_WIDTH).astype(BF16)
    o += B_KV_WIDTH
    ga_ref[0] = jax.nn.sigmoid(seg(o, D_MODEL)).astype(BF16)
    o += D_MODEL
    gb_ref[0] = jax.nn.sigmoid(seg(o, D_MODEL)).astype(BF16)


def _in_proj(x, sh, sc, g, w_in, bd, gains, tables, tm):
    bsz, seq, d = x.shape
    n_in = w_in.shape[1]
    nst = seq // tm
    row = lambda st, b: (b, st, 0)
    per_b = lambda st, b: (b, 0, 0)
    const2 = lambda st, b: (0, 0)
    tab = pl.BlockSpec((tm, LANES), lambda st, b: (st, 0))
    return pl.pallas_call(
        _in_proj_kernel,
        out_shape=(
            jax.ShapeDtypeStruct((bsz, seq, 3 * A_WIDTH), F32),
            jax.ShapeDtypeStruct((bsz, seq, B_Q_WIDTH), BF16),
            jax.ShapeDtypeStruct((bsz, seq, 2 * B_KV_WIDTH), BF16),
            jax.ShapeDtypeStruct((bsz, seq, d), BF16),
            jax.ShapeDtypeStruct((bsz, seq, d), BF16),
        ),
        grid=(nst, bsz),
        in_specs=[
            pl.BlockSpec((1, tm, d), row),
            pl.BlockSpec((1, 1, d), per_b),
            pl.BlockSpec((1, 1, d), per_b),
            pl.BlockSpec((1, d), const2),
            pl.BlockSpec((d, n_in), const2),
            pl.BlockSpec((A_WIDTH, A_WIDTH), const2),
            pl.BlockSpec((1, A_WIDTH), const2),
            pl.BlockSpec((1, A_WIDTH), const2),
            pl.BlockSpec((1, B_Q_WIDTH), const2),
            pl.BlockSpec((1, B_KV_WIDTH), const2),
            tab, tab, tab, tab, tab, tab,
        ],
        out_specs=(
            pl.BlockSpec((1, tm, 3 * A_WIDTH), row),
            pl.BlockSpec((1, tm, B_Q_WIDTH), row),
            pl.BlockSpec((1, tm, 2 * B_KV_WIDTH), row),
            pl.BlockSpec((1, tm, d), row),
            pl.BlockSpec((1, tm, d), row),
        ),
        compiler_params=_cparams(("arbitrary", "arbitrary")),
        name="in_proj",
    )(x, sh, sc, g, w_in, bd, *gains, *tables)


_QB = 128


def _dilated_kernel(q_ref, k_ref, v_ref, o_ref, m_scr, l_scr, acc_scr):
    seq = q_ref.shape[1]
    lane = lax.broadcasted_iota(jnp.int32, (1, LANES), 1)
    head0 = lane < HEAD_DIM

    for pi, (window, dil) in enumerate(A_PATTERNS):
        radius = window // (2 * dil)
        sub_len = seq // dil
        kw = min(2 * _QB, sub_len)
        nqb = sub_len // _QB

        def block(blk, carry, pi=pi, dil=dil, radius=radius, sub_len=sub_len, kw=kw, nqb=nqb):
            r = blk // nqb
            i0q = (blk % nqb) * _QB
            i0k = jnp.clip(i0q - (kw - _QB) // 2, 0, sub_len - kw)
            if dil == 1:
                qrows = pl.ds(pl.multiple_of(i0q, SUBLANES), _QB)
                krows = pl.ds(pl.multiple_of(i0k, SUBLANES), kw)
            else:
                qrows = pl.ds(r + dil * i0q, _QB, stride=dil)
                krows = pl.ds(r + dil * i0k, kw, stride=dil)
            q = q_ref[0, qrows, :]
            kk = k_ref[0, krows, :].astype(BF16)
            vv = v_ref[0, krows, :].astype(BF16)
            q2 = jnp.concatenate([jnp.where(head0, q, 0.0), jnp.where(head0, 0.0, q)],
                                 axis=0).astype(BF16)
            s = lax.dot_general(q2, kk, (((1,), (1,)), ((), ())),
                                preferred_element_type=F32)
            qrow = lax.broadcasted_iota(jnp.int32, (2 * _QB, kw), 0) & (_QB - 1)
            kcol = lax.broadcasted_iota(jnp.int32, (2 * _QB, kw), 1)
            valid = jnp.abs(kcol - qrow + (i0k - i0q)) <= radius
            s = jnp.where(valid, s, -jnp.inf)
            mb = jnp.broadcast_to(jnp.max(s, axis=-1, keepdims=True), (2 * _QB, LANES))

            def wide(t):
                return t if kw == LANES else jnp.concatenate([t] * (kw // LANES), axis=-1)

            if pi == 0:
                m_new = mb
                p = jnp.exp(s - wide(m_new))
                l_new = jnp.broadcast_to(jnp.sum(p, axis=-1, keepdims=True), (2 * _QB, LANES))
                pv = jnp.dot(p.astype(BF16), vv, preferred_element_type=F32)
                acc_new = jnp.where(head0, pv[:_QB], pv[_QB:])
            else:
                m_old = jnp.concatenate([m_scr[0, qrows, :], m_scr[1, qrows, :]], axis=0)
                l_old = jnp.concatenate([l_scr[0, qrows, :], l_scr[1, qrows, :]], axis=0)
                m_new = jnp.maximum(m_old, mb)
                alpha = jnp.exp(m_old - m_new)
                p = jnp.exp(s - wide(m_new))
                l_new = alpha * l_old + jnp.broadcast_to(
                    jnp.sum(p, axis=-1, keepdims=True), (2 * _QB, LANES))
                pv = jnp.dot(p.astype(BF16), vv, preferred_element_type=F32)
                a2 = jnp.where(head0, alpha[:_QB], alpha[_QB:])
                acc_new = a2 * acc_scr[qrows, :] + jnp.where(head0, pv[:_QB], pv[_QB:])
            m_scr[0, qrows, :] = m_new[:_QB]
            m_scr[1, qrows, :] = m_new[_QB:]
            l_scr[0, qrows, :] = l_new[:_QB]
            l_scr[1, qrows, :] = l_new[_QB:]
            acc_scr[qrows, :] = acc_new
            return carry

        lax.fori_loop(0, dil * nqb, block, 0)

    def finish(i, carry):
        rows = pl.ds(pl.multiple_of(i * 256, 256), 256)
        l2 = jnp.where(head0, l_scr[0, rows, :], l_scr[1, rows, :])
        o_ref[0, rows, :] = (acc_scr[rows, :] / l2).astype(o_ref.dtype)
        return carry

    lax.fori_loop(0, seq // 256, finish, 0)


def _dilated(za):
    bsz, seq, _ = za.shape
    nhp = A_WIDTH // LANES
    blk = (1, seq, LANES)
    return pl.pallas_call(
        _dilated_kernel,
        out_shape=jax.ShapeDtypeStruct((bsz, seq, A_WIDTH), BF16),
        grid=(bsz, nhp),
        in_specs=[
            pl.BlockSpec(blk, lambda b, j: (b, 0, j)),
            pl.BlockSpec(blk, lambda b, j: (b, 0, nhp + j)),
            pl.BlockSpec(blk, lambda b, j: (b, 0, 2 * nhp + j)),
        ],
        out_specs=pl.BlockSpec(blk, lambda b, j: (b, 0, j)),
        scratch_shapes=[
            pltpu.VMEM((2, seq, LANES), F32),
            pltpu.VMEM((2, seq, LANES), F32),
            pltpu.VMEM((seq, LANES), F32),
        ],
        compiler_params=_cparams(("arbitrary", "arbitrary")),
        name="dilated_attn",
    )(za, za, za)


def _gqa_kernel(q_ref, kv_ref, o_ref):
    tq = q_ref.shape[1]
    lane = lax.broadcasted_iota(jnp.int32, (1, LANES), 1)
    head0 = lane < HEAD_DIM
    kk = kv_ref[0, :, 0:B_KV_WIDTH]
    vv = kv_ref[0, :, B_KV_WIDTH:2 * B_KV_WIDTH]
    zero = jnp.zeros((), BF16)
    for j in range(B_Q_WIDTH // LANES):
        qj = q_ref[0, :, j * LANES:(j + 1) * LANES]
        q2 = jnp.concatenate([jnp.where(head0, qj, zero), jnp.where(head0, zero, qj)], axis=0)
        s = lax.dot_general(q2, kk, (((1,), (1,)), ((), ())), preferred_element_type=F32)
        m = jnp.max(s, axis=-1, keepdims=True)
        p = jnp.exp(s - m)
        l = jnp.sum(p, axis=-1, keepdims=True)
        pv = jnp.dot(p.astype(BF16), vv, preferred_element_type=F32) / l
        o_ref[0, :, j * LANES:(j + 1) * LANES] = jnp.where(
            head0, pv[:tq], pv[tq:]).astype(o_ref.dtype)


def _gqa(qb, kvb, tq):
    bsz, seq, _ = qb.shape
    return pl.pallas_call(
        _gqa_kernel,
        out_shape=jax.ShapeDtypeStruct((bsz, seq, B_Q_WIDTH), BF16),
        grid=(bsz, seq // tq),
        in_specs=[
            pl.BlockSpec((1, tq, B_Q_WIDTH), lambda b, i: (b, i, 0)),
            pl.BlockSpec((1, seq, 2 * B_KV_WIDTH), lambda b, i: (b, 0, 0)),
        ],
        out_specs=pl.BlockSpec((1, tq, B_Q_WIDTH), lambda b, i: (b, i, 0)),
        compiler_params=_cparams(("arbitrary", "arbitrary")),
        name="gqa_attn",
    )(qb, kvb)


def _merge_kernel(x_ref, ya_ref, yb_ref, ga_ref, gb_ref, wpa_ref, wpb_ref, wo_ref,
                  gt_ref, g_ref, sh_ref, sc_ref, wr_ref,
                  xn_ref, hrow_ref, e_ref, w_ref):
    tm = x_ref.shape[0]
    pa = jnp.dot(ya_ref[...], wpa_ref[...], preferred_element_type=F32)
    pb = jnp.dot(yb_ref[...], wpb_ref[...], preferred_element_type=F32)
    merged = ga_ref[...].astype(F32) * pa + gb_ref[...].astype(F32) * pb
    out = jnp.dot(merged.astype(BF16), wo_ref[...], preferred_element_type=F32)
    xn = x_ref[...] + gt_ref[0] * out
    xn_ref[...] = xn

    ms = jnp.mean(xn * xn, axis=-1, keepdims=True)
    h = xn * lax.rsqrt(ms + NORM_EPS) * g_ref[...]
    h = h * (1.0 + sc_ref[0]) + sh_ref[0]
    for c in range(ROW_CHUNKS):
        hrow_ref[pl.ds(c, tm, stride=ROW_CHUNKS), :] = h[:, c * LANES:(c + 1) * LANES]

    logits = lax.dot_general(wr_ref[...], h, (((1,), (1,)), ((), ())),
                             preferred_element_type=F32,
                             precision=lax.Precision.HIGHEST)
    mx = jnp.max(logits, axis=0, keepdims=True)
    ex = jnp.exp(logits - mx)
    probs = ex / jnp.sum(ex, axis=0, keepdims=True)
    pg = probs.reshape(N_GROUPS, EXPERTS_PER_GROUP, tm)
    sub = lax.broadcasted_iota(jnp.int32, pg.shape, 1).astype(F32)
    m1 = jnp.max(pg, axis=1, keepdims=True)
    i1 = jnp.min(jnp.where(pg == m1, sub, float(EXPERTS_PER_GROUP)), axis=1, keepdims=True)
    pg2 = jnp.where(sub == i1, -1.0, pg)
    m2 = jnp.max(pg2, axis=1, keepdims=True)
    i2 = jnp.min(jnp.where(pg2 == m2, sub, float(EXPERTS_PER_GROUP)), axis=1, keepdims=True)
    score = m1 + m2
    gid = lax.broadcasted_iota(jnp.int32, score.shape, 0).astype(F32)
    best = jnp.max(score, axis=0, keepdims=True)
    gsel = jnp.min(jnp.where(score == best, gid, float(N_GROUPS)), axis=0, keepdims=True)
    pick = gid == gsel
    w0 = jnp.sum(jnp.where(pick, m1, 0.0), axis=0)
    w1 = jnp.sum(jnp.where(pick, m2, 0.0), axis=0)
    j0 = jnp.sum(jnp.where(pick, i1, 0.0), axis=0)
    j1 = jnp.sum(jnp.where(pick, i2, 0.0), axis=0)
    base = gsel[0] * float(EXPERTS_PER_GROUP)
    tot = w0 + w1
    e_ref[0, 0:1, :] = (base + j0).astype(jnp.int32)
    e_ref[0, 1:2, :] = (base + j1).astype(jnp.int32)
    w_ref[0, 0:1, :] = w0 / tot
    w_ref[0, 1:2, :] = w1 / tot


def _merge(x, ya, yb, ga, gb, wpa, wpb, wo, gt, g, sh, sc, wr_t, tm, tiles_per_seq):
    t, d = x.shape
    nt = t // tm
    row = lambda i: (i, 0)
    per_b = lambda i: (i // tiles_per_seq, 0, 0)
    const2 = lambda i: (0, 0)
    return pl.pallas_call(
        _merge_kernel,
        out_shape=(
            jax.ShapeDtypeStruct((t, d), F32),
            jax.ShapeDtypeStruct((t * ROW_CHUNKS, LANES), F32),
            jax.ShapeDtypeStruct((nt, TOP_K, tm), jnp.int32),
            jax.ShapeDtypeStruct((nt, TOP_K, tm), F32),
        ),
        grid=(nt,),
        in_specs=[
            pl.BlockSpec((tm, d), row),
            pl.BlockSpec((tm, A_WIDTH), row),
            pl.BlockSpec((tm, B_Q_WIDTH), row),
            pl.BlockSpec((tm, d), row),
            pl.BlockSpec((tm, d), row),
            pl.BlockSpec((A_WIDTH, d), const2),
            pl.BlockSpec((B_Q_WIDTH, d), const2),
            pl.BlockSpec((d, d), const2),
            pl.BlockSpec((1, 1, d), per_b),
            pl.BlockSpec((1, d), const2),
            pl.BlockSpec((1, 1, d), per_b),
            pl.BlockSpec((1, 1, d), per_b),
            pl.BlockSpec((N_EXPERTS, d), const2),
        ],
        out_specs=(
            pl.BlockSpec((tm, d), row),
            pl.BlockSpec((tm * ROW_CHUNKS, LANES), row),
            pl.BlockSpec((1, TOP_K, tm), lambda i: (i, 0, 0)),
            pl.BlockSpec((1, TOP_K, tm), lambda i: (i, 0, 0)),
        ),
        compiler_params=_cparams(("arbitrary",)),
        name="merge_router",
    )(x, ya, yb, ga, gb, wpa, wpb, wo, gt, g, sh, sc, wr_t)


def _expert_kernel(blk_e_ref, tokc_ref, tokn_ref, dst_ref, h_hbm, wg_ref, wu_ref, wd_ref,
                   y_hbm, gbuf, ybuf, gsem, ssem):
    i = pl.program_id(0)
    n = pl.num_programs(0)
    slot = i % 2

    def row_tile(idx):
        return pl.ds(pl.multiple_of(idx * ROW_CHUNKS, ROW_CHUNKS), ROW_CHUNKS)

    def gather_copy(tok_ref, s, j):
        return pltpu.make_async_copy(h_hbm.at[row_tile(tok_ref[0, 0, j]), :],
                                     gbuf.at[s, row_tile(j), :], gsem.at[s])

    def scatter_copy(s, j):
        return pltpu.make_async_copy(ybuf.at[s, row_tile(j), :],
                                     y_hbm.at[row_tile(dst_ref[0, 0, j]), :], ssem.at[s])

    def start_gather(tok_ref, s):
        def body(j, c):
            gather_copy(tok_ref, s, j).start()
            return c
        lax.fori_loop(0, MOE_BLOCK, body, 0)

    @pl.when(i == 0)
    def _():
        start_gather(tokc_ref, 0)

    @pl.when(i + 1 < n)
    def _():
        start_gather(tokn_ref, 1 - slot)

    def wait_gather(j, c):
        gather_copy(tokc_ref, slot, j).wait()
        return c
    lax.fori_loop(0, MOE_BLOCK, wait_gather, 0)

    xb = jnp.concatenate(
        [gbuf[slot, pl.ds(c, MOE_BLOCK, stride=ROW_CHUNKS), :] for c in range(ROW_CHUNKS)],
        axis=-1).astype(BF16)
    gate = jnp.dot(xb, wg_ref[0], preferred_element_type=F32)
    up = jnp.dot(xb, wu_ref[0], preferred_element_type=F32)
    hid = (gate * jax.nn.sigmoid(gate) * up).astype(BF16)
    y = jnp.dot(hid, wd_ref[0], preferred_element_type=F32)

    @pl.when(i >= 2)
    def _():
        def wait_old(j, c):
            scatter_copy(slot, j).wait()
            return c
        lax.fori_loop(0, MOE_BLOCK, wait_old, 0)

    for c in range(ROW_CHUNKS):
        ybuf[slot, pl.ds(c, MOE_BLOCK, stride=ROW_CHUNKS), :] = y[:, c * LANES:(c + 1) * LANES]

    def start_scatter(j, c):
        scatter_copy(slot, j).start()
        return c
    lax.fori_loop(0, MOE_BLOCK, start_scatter, 0)

    @pl.when(i == n - 1)
    def _():
        def wait_last(j, c):
            scatter_copy(slot, j).wait()
            return c
        lax.fori_loop(0, MOE_BLOCK, wait_last, 0)

        @pl.when(n >= 2)
        def _():
            def wait_prev(j, c):
                scatter_copy(1 - slot, j).wait()
                return c
            lax.fori_loop(0, MOE_BLOCK, wait_prev, 0)


def _experts(hrows, blk_e, buf_tok, buf_dst, wg, wu, wd, n_slots):
    n_blocks = blk_e.shape[0]
    d = D_MODEL
    tok3 = buf_tok.reshape(n_blocks, 1, MOE_BLOCK)
    dst3 = buf_dst.reshape(n_blocks, 1, MOE_BLOCK)
    idx_blk = (1, 1, MOE_BLOCK)
    grid_spec = pltpu.PrefetchScalarGridSpec(
        num_scalar_prefetch=1,
        grid=(n_blocks,),
        in_specs=[
            pl.BlockSpec(idx_blk, lambda i, be: (i, 0, 0), memory_space=pltpu.SMEM),
            pl.BlockSpec(idx_blk, lambda i, be: (jnp.minimum(i + 1, n_blocks - 1), 0, 0),
                         memory_space=pltpu.SMEM),
            pl.BlockSpec(idx_blk, lambda i, be: (i, 0, 0), memory_space=pltpu.SMEM),
            pl.BlockSpec(memory_space=pl.ANY),
            pl.BlockSpec((1, d, D_FF_EXPERT), lambda i, be: (be[i], 0, 0)),
            pl.BlockSpec((1, d, D_FF_EXPERT), lambda i, be: (be[i], 0, 0)),
            pl.BlockSpec((1, D_FF_EXPERT, d), lambda i, be: (be[i], 0, 0)),
        ],
        out_specs=pl.BlockSpec(memory_space=pl.ANY),
        scratch_shapes=[
            pltpu.VMEM((2, MOE_BLOCK * ROW_CHUNKS, LANES), F32),
            pltpu.VMEM((2, MOE_BLOCK * ROW_CHUNKS, LANES), F32),
            pltpu.SemaphoreType.DMA((2,)),
            pltpu.SemaphoreType.DMA((2,)),
        ],
    )
    return pl.pallas_call(
        _expert_kernel,
        out_shape=jax.ShapeDtypeStruct((n_slots * ROW_CHUNKS, LANES), F32),
        grid_spec=grid_spec,
        compiler_params=_cparams(("arbitrary",)),
        name="experts",
    )(blk_e, tok3, tok3, dst3, hrows, wg, wu, wd)


def _combine_kernel(x_ref, y0_ref, y1_ref, w_ref, gt_ref, o_ref):
    tm = x_ref.shape[0]
    w0 = w_ref[:, 0:1]
    w1 = w_ref[:, 1:2]
    gt = gt_ref[0]
    for c in range(ROW_CHUNKS):
        cols = slice(c * LANES, (c + 1) * LANES)
        rows = pl.ds(c, tm, stride=ROW_CHUNKS)
        moe = w0 * y0_ref[rows, :] + w1 * y1_ref[rows, :]
        o_ref[:, cols] = x_ref[:, cols] + gt[:, cols] * moe


def _combine(xn, yk, w_col, gt, tm, tiles_per_seq):
    t, d = xn.shape
    nt = t // tm
    return pl.pallas_call(
        _combine_kernel,
        out_shape=jax.ShapeDtypeStruct((t, d), F32),
        grid=(nt,),
        in_specs=[
            pl.BlockSpec((tm, d), lambda i: (i, 0)),
            pl.BlockSpec((tm * ROW_CHUNKS, LANES), lambda i: (i, 0)),
            pl.BlockSpec((tm * ROW_CHUNKS, LANES), lambda i: (nt + i, 0)),
            pl.BlockSpec((tm, TOP_K), lambda i: (i, 0)),
            pl.BlockSpec((1, 1, d), lambda i: (i // tiles_per_seq, 0, 0)),
        ],
        out_specs=pl.BlockSpec((tm, d), lambda i: (i, 0)),
        compiler_params=_cparams(("arbitrary",)),
        name="moe_combine",
    )(xn, yk, yk, w_col, gt)


def _dispatch_plan(e_sel, t):
    n_assign = t * TOP_K
    e_flat = e_sel.T.reshape(n_assign)
    order = jnp.argsort(e_flat).astype(jnp.int32)
    onehot = e_flat[:, None] == jnp.arange(N_EXPERTS, dtype=jnp.int32)[None, :]
    counts = jnp.sum(onehot.astype(jnp.int32), axis=0)
    padded = (counts + MOE_BLOCK - 1) // MOE_BLOCK * MOE_BLOCK
    pad_end = jnp.cumsum(padded)
    pad_start = pad_end - padded
    start = jnp.cumsum(counts) - counts
    n_pad = -(-(n_assign + N_EXPERTS * MOE_BLOCK) // MOE_BLOCK) * MOE_BLOCK
    n_blocks = n_pad // MOE_BLOCK
    slot = jnp.arange(n_pad, dtype=jnp.int32)
    slot_e = jnp.minimum(
        jnp.sum((slot[:, None] >= pad_end[None, :]).astype(jnp.int32), axis=1), N_EXPERTS - 1)
    off = slot - pad_start[slot_e]
    valid = off < counts[slot_e]
    src = jnp.clip(start[slot_e] + off, 0, n_assign - 1)
    a = order[src]
    pad_slot = n_assign + (slot % (2 * MOE_BLOCK))
    buf_dst = jnp.where(valid, a, pad_slot).astype(jnp.int32)
    buf_tok = jnp.where(valid, a % t, 0).astype(jnp.int32)
    blk_e = slot_e.reshape(n_blocks, MOE_BLOCK)[:, 0]
    return blk_e, buf_tok, buf_dst, n_assign + 2 * MOE_BLOCK


def kernel(x, c, w_ada, b_ada, g_mix, w_in, qn_a, kn_a, qn_b, kn_b, w_pa, w_pb, w_out,
           g_ffn, w_router, w_gate, w_up, w_down):
    bsz, seq, d = x.shape
    depth = w_ada.shape[0]
    t = bsz * seq
    tm = 512
    tiles_per_seq = seq // tm

    mod = _adaln(c, w_ada, b_ada)
    tables = _rope_tables(seq)
    bd = jnp.asarray(np.kron(np.eye(A_HEADS), np.ones((HEAD_DIM, HEAD_DIM))), BF16)

    g4 = B_Q_HEADS // B_KV_HEADS
    head_perm = np.concatenate(
        [np.r_[np.arange(j * HEAD_DIM, (j + 1) * HEAD_DIM),
               np.arange((j + g4) * HEAD_DIM, (j + g4 + 1) * HEAD_DIM)] for j in range(g4)])
    qb_lo = 3 * A_WIDTH
    col_perm = np.arange(w_in.shape[2])
    col_perm[qb_lo:qb_lo + B_Q_WIDTH] = qb_lo + head_perm
    scale = HEAD_DIM ** -0.5
    wr_t = w_router.T

    xf = x
    for l in range(depth):
        sh1, sc1, gt1, sh2, sc2, gt2 = [
            mod[l, :, i * d:(i + 1) * d].reshape(bsz, 1, d) for i in range(6)]
        w_in_l = w_in[l][:, col_perm].astype(BF16)
        gains = (
            jnp.tile(qn_a[l] * scale, A_HEADS).reshape(1, A_WIDTH),
            jnp.tile(kn_a[l], A_HEADS).reshape(1, A_WIDTH),
            jnp.tile(qn_b[l] * scale, B_Q_HEADS).reshape(1, B_Q_WIDTH),
            jnp.tile(kn_b[l], B_KV_HEADS).reshape(1, B_KV_WIDTH),
        )
        za, qb, kvb, ga, gb = _in_proj(xf.reshape(bsz, seq, d), sh1, sc1,
                                       g_mix[l].reshape(1, d), w_in_l, bd, gains, tables, tm)
        ya = _dilated(za)
        yb = _gqa(qb, kvb, 256)
        xn, hrows, e_sel, w_sel = _merge(
            xf.reshape(t, d), ya.reshape(t, A_WIDTH), yb.reshape(t, B_Q_WIDTH),
            ga.reshape(t, d), gb.reshape(t, d),
            w_pa[l].astype(BF16), w_pb[l][head_perm].astype(BF16), w_out[l].astype(BF16),
            gt1, g_ffn[l].reshape(1, d), sh2, sc2, wr_t, tm, tiles_per_seq)
        e_tok = e_sel.transpose(0, 2, 1).reshape(t, TOP_K)
        w_tok = w_sel.transpose(0, 2, 1).reshape(t, TOP_K)
        blk_e, buf_tok, buf_dst, n_slots = _dispatch_plan(e_tok, t)
        yk = _experts(hrows, blk_e, buf_tok, buf_dst, w_gate[l].astype(BF16),
                      w_up[l].astype(BF16), w_down[l].astype(BF16), n_slots)
        xf = _combine(xn, yk, w_tok, gt2, tm, tiles_per_seq)
    return xf.reshape(bsz, seq, d)
```

```python
import functools

import jax
import jax.numpy as jnp
import numpy as np
from jax import lax
from jax.experimental import pallas as pl
from jax.experimental.pallas import tpu as pltpu

D_MODEL = 1024
HEAD_DIM = 64
NORM_EPS = 1e-6
A_HEADS = 8
A_WIDTH = A_HEADS * HEAD_DIM
A_PATTERNS = ((128, 1), (512, 4), (2048, 16))
A_ROT_DIMS = HEAD_DIM // 4
A_ROPE_THETA = 500000.0
B_Q_HEADS = 8
B_KV_HEADS = 2
B_Q_WIDTH = B_Q_HEADS * HEAD_DIM
B_KV_WIDTH = B_KV_HEADS * HEAD_DIM
B_ROPE_THETA = 10000.0
GRID_W = 64
N_EXPERTS = 32
N_GROUPS = 4
EXPERTS_PER_GROUP = N_EXPERTS // N_GROUPS
TOP_K = 2
D_FF_EXPERT = D_MODEL // 2
MOE_BLOCK = 256

LANES = 128
SUBLANES = 8
ROW_CHUNKS = D_MODEL // LANES
VMEM_LIMIT = 56 * 1024 * 1024

F32 = jnp.float32
BF16 = jnp.bfloat16


def _cparams(sem):
    return pltpu.CompilerParams(dimension_semantics=sem, vmem_limit_bytes=VMEM_LIMIT)


def _adaln_kernel(c_ref, w_ref, b_ref, o_ref):
    c = c_ref[...]
    cond = c * jax.nn.sigmoid(c)
    o_ref[0] = jnp.dot(cond, w_ref[0], preferred_element_type=F32,
                       precision=lax.Precision.HIGHEST) + b_ref[0]


def _adaln(c, w_ada, b_ada):
    depth, d, n = w_ada.shape
    bsz = c.shape[0]
    tn = 1024
    return pl.pallas_call(
        _adaln_kernel,
        out_shape=jax.ShapeDtypeStruct((depth, bsz, n), F32),
        grid=(depth, n // tn),
        in_specs=[
            pl.BlockSpec((bsz, d), lambda l, j: (0, 0)),
            pl.BlockSpec((1, d, tn), lambda l, j: (l, 0, j)),
            pl.BlockSpec((1, 1, tn), lambda l, j: (l, 0, j)),
        ],
        out_specs=pl.BlockSpec((1, bsz, tn), lambda l, j: (l, 0, j)),
        compiler_params=_cparams(("arbitrary", "arbitrary")),
        name="adaln",
    )(c, w_ada, b_ada.reshape(depth, 1, n))


def _rope_tables(seq):
    pos = jnp.arange(seq, dtype=F32)
    row = jnp.floor(pos / GRID_W)
    col = pos - row * GRID_W
    d = np.arange(LANES) % HEAD_DIM

    def build(segments):
        c = jnp.ones((seq, LANES), F32)
        s1 = jnp.zeros((seq, LANES), F32)
        s2 = jnp.zeros((seq, LANES), F32)
        for lo, half, theta, p in segments:
            first = (d >= lo) & (d < lo + half)
            second = (d >= lo + half) & (d < lo + 2 * half)
            idx = np.where(first, d - lo, np.where(second, d - lo - half, 0))
            freqs = theta ** (-jnp.arange(half, dtype=F32) / half)
            ang = p[:, None] * freqs[idx][None, :]
            cs, sn = jnp.cos(ang), jnp.sin(ang)
            rot = jnp.asarray(first | second)[None, :]
            c = jnp.where(rot, cs, c)
            s1 = jnp.where(jnp.asarray(first)[None, :], -sn, s1)
            s2 = jnp.where(jnp.asarray(second)[None, :], sn, s2)
        return c, s1, s2

    ta = build([(0, A_ROT_DIMS // 2, A_ROPE_THETA, pos)])
    q = HEAD_DIM // 4
    tb = build([(0, q, B_ROPE_THETA, row), (2 * q, q, B_ROPE_THETA, col)])
    return ta + tb


def _in_proj_kernel(x_ref, sh_ref, sc_ref, g_ref, w_ref, bd_ref,
                    gqa_ref, gka_ref, gqb_ref, gkb_ref,
                    ca_ref, s1a_ref, s2a_ref, cb_ref, s1b_ref, s2b_ref,
                    za_ref, qb_ref, kvb_ref, ga_ref, gb_ref):
    x = x_ref[0]
    ms = jnp.mean(x * x, axis=-1, keepdims=True)
    h = x * lax.rsqrt(ms + NORM_EPS) * g_ref[...]
    h = h * (1.0 + sc_ref[0]) + sh_ref[0]
    hb = h.astype(BF16)

    def seg(lo, width):
        return jnp.dot(hb, w_ref[:, lo:lo + width], preferred_element_type=F32)

    def qk_norm(z, gain_ref):
        width = z.shape[-1]
        ss = jnp.dot((z * z).astype(BF16), bd_ref[:width, :width],
                     preferred_element_type=F32)
        return z * lax.rsqrt(ss * (1.0 / HEAD_DIM) + NORM_EPS) * gain_ref[...]

    def tile(t, width):
        reps = width // LANES
        return t if reps == 1 else jnp.concatenate([t] * reps, axis=-1)

    def rope(z, c_ref, s1_ref, s2_ref, half):
        width = z.shape[-1]
        up = pltpu.roll(z, width - half, 1)
        dn = pltpu.roll(z, half, 1)
        return (z * tile(c_ref[...], width) + up * tile(s1_ref[...], width)
                + dn * tile(s2_ref[...], width))

    ha = A_ROT_DIMS // 2
    hq = HEAD_DIM // 4
    o = 0
    qa = rope(qk_norm(seg(o, A_WIDTH), gqa_ref), ca_ref, s1a_ref, s2a_ref, ha)
    za_ref[0, :, 0:A_WIDTH] = qa
    o += A_WIDTH
    ka = rope(qk_norm(seg(o, A_WIDTH), gka_ref), ca_ref, s1a_ref, s2a_ref, ha)
    za_ref[0, :, A_WIDTH:2 * A_WIDTH] = ka
    o += A_WIDTH
    za_ref[0, :, 2 * A_WIDTH:3 * A_WIDTH] = seg(o, A_WIDTH)
    o += A_WIDTH
    qb = rope(qk_norm(seg(o, B_Q_WIDTH), gqb_ref), cb_ref, s1b_ref, s2b_ref, hq)
    qb_ref[0] = qb.astype(BF16)
    o += B_Q_WIDTH
    kb = rope(qk_norm(seg(o, B_KV_WIDTH), gkb_ref), cb_ref, s1b_ref, s2b_ref, hq)
    kvb_ref[0, :, 0:B_KV_WIDTH] = kb.astype(BF16)
    o += B_KV_WIDTH
    kvb_ref[0, :, B_KV_WIDTH:2 * B_KV_WIDTH] = seg(o, B_KV_WIDTH).astype(BF16)
    o += B_KV_WIDTH
    ga_ref[0] = jax.nn.sigmoid(seg(o, D_MODEL)).astype(BF16)
    o += D_MODEL
    gb_ref[0] = jax.nn.sigmoid(seg(o, D_MODEL)).astype(BF16)


def _in_proj(x, sh, sc, g, w_in, bd, gains, tables, tm):
    bsz, seq, d = x.shape
    n_in = w_in.shape[1]
    nst = seq // tm
    row = lambda st, b: (b, st, 0)
    per_b = lambda st, b: (b, 0, 0)
    const2 = lambda st, b: (0, 0)
    tab = pl.BlockSpec((tm, LANES), lambda st, b: (st, 0))
    return pl.pallas_call(
        _in_proj_kernel,
        out_shape=(
            jax.ShapeDtypeStruct((bsz, seq, 3 * A_WIDTH), F32),
            jax.ShapeDtypeStruct((bsz, seq, B_Q_WIDTH), BF16),
            jax.ShapeDtypeStruct((bsz, seq, 2 * B_KV_WIDTH), BF16),
            jax.ShapeDtypeStruct((bsz, seq, d), BF16),
            jax.ShapeDtypeStruct((bsz, seq, d), BF16),
        ),
        grid=(nst, bsz),
        in_specs=[
            pl.BlockSpec((1, tm, d), row),
            pl.BlockSpec((1, 1, d), per_b),
            pl.BlockSpec((1, 1, d), per_b),
            pl.BlockSpec((1, d), const2),
            pl.BlockSpec((d, n_in), const2),
            pl.BlockSpec((A_WIDTH, A_WIDTH), const2),
            pl.BlockSpec((1, A_WIDTH), const2),
            pl.BlockSpec((1, A_WIDTH), const2),
            pl.BlockSpec((1, B_Q_WIDTH), const2),
            pl.BlockSpec((1, B_KV_WIDTH), const2),
            tab, tab, tab, tab, tab, tab,
        ],
        out_specs=(
            pl.BlockSpec((1, tm, 3 * A_WIDTH), row),
            pl.BlockSpec((1, tm, B_Q_WIDTH), row),
            pl.BlockSpec((1, tm, 2 * B_KV_WIDTH), row),
            pl.BlockSpec((1, tm, d), row),
            pl.BlockSpec((1, tm, d), row),
        ),
        compiler_params=_cparams(("arbitrary", "arbitrary")),
        name="in_proj",
    )(x, sh, sc, g, w_in, bd, *gains, *tables)


_QB = 128
_UNROLL = 4


def _dilated_kernel(q_ref, k_ref, v_ref, o_ref, m_scr, l_scr, acc_scr):
    seq = q_ref.shape[1]
    lane = lax.broadcasted_iota(jnp.int32, (1, LANES), 1)
    head0 = lane < HEAD_DIM

    def merge(t):
        return jnp.where(head0, t[:_QB], t[_QB:])

    for pi, (window, dil) in enumerate(A_PATTERNS):
        radius = window // (2 * dil)
        sub_len = seq // dil
        kw = min(2 * _QB, sub_len)
        nqb = sub_len // _QB
        nblk = dil * nqb

        def wide(t, kw=kw):
            return t if kw == LANES else jnp.concatenate([t] * (kw // LANES), axis=-1)

        def load(blk, pi=pi, dil=dil, sub_len=sub_len, kw=kw, nqb=nqb):
            r = blk // nqb
            i0q = (blk % nqb) * _QB
            i0k = jnp.clip(i0q - (kw - _QB) // 2, 0, sub_len - kw)
            if dil == 1:
                qrows = pl.ds(pl.multiple_of(i0q, SUBLANES), _QB)
                krows = pl.ds(pl.multiple_of(i0k, SUBLANES), kw)
            else:
                qrows = pl.ds(r + dil * i0q, _QB, stride=dil)
                krows = pl.ds(r + dil * i0k, kw, stride=dil)
            q = q_ref[0, qrows, :]
            kk = k_ref[0, krows, :].astype(BF16)
            vv = v_ref[0, krows, :].astype(BF16)
            old = None
            if pi > 0:
                old = (jnp.concatenate([m_scr[0, qrows, :], m_scr[1, qrows, :]], axis=0),
                       l_scr[qrows, :], acc_scr[qrows, :])
            return qrows, i0k - i0q, q, kk, vv, old

        def compute(shift, q, kk, vv, old, radius=radius, kw=kw, wide=wide):
            q2 = jnp.concatenate([jnp.where(head0, q, 0.0), jnp.where(head0, 0.0, q)],
                                 axis=0).astype(BF16)
            s = lax.dot_general(q2, kk, (((1,), (1,)), ((), ())),
                                preferred_element_type=F32)
            qrow = lax.broadcasted_iota(jnp.int32, (2 * _QB, kw), 0) & (_QB - 1)
            kcol = lax.broadcasted_iota(jnp.int32, (2 * _QB, kw), 1)
            valid = jnp.abs(kcol - qrow + shift) <= radius
            s = jnp.where(valid, s, -jnp.inf)
            mb = jnp.broadcast_to(jnp.max(s, axis=-1, keepdims=True), (2 * _QB, LANES))
            m_new = mb if old is None else jnp.maximum(old[0], mb)
            p = jnp.exp(s - wide(m_new))
            psum = merge(jnp.broadcast_to(jnp.sum(p, axis=-1, keepdims=True),
                                          (2 * _QB, LANES)))
            pv = merge(jnp.dot(p.astype(BF16), vv, preferred_element_type=F32))
            if old is None:
                return m_new, psum, pv
            alpha = jnp.exp(merge(old[0]) - merge(m_new))
            return m_new, alpha * old[1] + psum, alpha * old[2] + pv

        def group(it, carry, load=load, compute=compute):
            loaded = [load(it * _UNROLL + u) for u in range(_UNROLL)]
            results = [compute(*ld[1:]) for ld in loaded]
            for ld, (m_new, l_new, acc_new) in zip(loaded, results):
                qrows = ld[0]
                m_scr[0, qrows, :] = m_new[:_QB]
                m_scr[1, qrows, :] = m_new[_QB:]
                l_scr[qrows, :] = l_new
                acc_scr[qrows, :] = acc_new
            return carry

        lax.fori_loop(0, nblk // _UNROLL, group, 0)

    def finish(i, carry):
        rows = pl.ds(pl.multiple_of(i * 256, 256), 256)
        o_ref[0, rows, :] = (acc_scr[rows, :] / l_scr[rows, :]).astype(o_ref.dtype)
        return carry

    lax.fori_loop(0, seq // 256, finish, 0)


def _dilated(za):
    bsz, seq, _ = za.shape
    nhp = A_WIDTH // LANES
    blk = (1, seq, LANES)
    return pl.pallas_call(
        _dilated_kernel,
        out_shape=jax.ShapeDtypeStruct((bsz, seq, A_WIDTH), BF16),
        grid=(bsz, nhp),
        in_specs=[
            pl.BlockSpec(blk, lambda b, j: (b, 0, j)),
            pl.BlockSpec(blk, lambda b, j: (b, 0, nhp + j)),
            pl.BlockSpec(blk, lambda b, j: (b, 0, 2 * nhp + j)),
        ],
        out_specs=pl.BlockSpec(blk, lambda b, j: (b, 0, j)),
        scratch_shapes=[
            pltpu.VMEM((2, seq, LANES), F32),
            pltpu.VMEM((seq, LANES), F32),
            pltpu.VMEM((seq, LANES), F32),
        ],
        compiler_params=_cparams(("arbitrary", "arbitrary")),
        name="dilated_attn",
    )(za, za, za)


def _gqa_kernel(q_ref, kv_ref, o_ref):
    tq = q_ref.shape[1]
    lane = lax.broadcasted_iota(jnp.int32, (1, LANES), 1)
    head0 = lane < HEAD_DIM
    kk = kv_ref[0, :, 0:B_KV_WIDTH]
    vv = kv_ref[0, :, B_KV_WIDTH:2 * B_KV_WIDTH]
    zero = jnp.zeros((), BF16)
    for j in range(B_Q_WIDTH // LANES):
        qj = q_ref[0, :, j * LANES:(j + 1) * LANES]
        q2 = jnp.concatenate([jnp.where(head0, qj, zero), jnp.where(head0, zero, qj)], axis=0)
        s = lax.dot_general(q2, kk, (((1,), (1,)), ((), ())), preferred_element_type=F32)
        m = jnp.max(s, axis=-1, keepdims=True)
        p = jnp.exp(s - m)
        l = jnp.sum(p, axis=-1, keepdims=True)
        pv = jnp.dot(p.astype(BF16), vv, preferred_element_type=F32) / l
        o_ref[0, :, j * LANES:(j + 1) * LANES] = jnp.where(
            head0, pv[:tq], pv[tq:]).astype(o_ref.dtype)


def _gqa(qb, kvb, tq):
    bsz, seq, _ = qb.shape
    return pl.pallas_call(
        _gqa_kernel,
        out_shape=jax.ShapeDtypeStruct((bsz, seq, B_Q_WIDTH), BF16),
        grid=(bsz, seq // tq),
        in_specs=[
            pl.BlockSpec((1, tq, B_Q_WIDTH), lambda b, i: (b, i, 0)),
            pl.BlockSpec((1, seq, 2 * B_KV_WIDTH), lambda b, i: (b, 0, 0)),
        ],
        out_specs=pl.BlockSpec((1, tq, B_Q_WIDTH), lambda b, i: (b, i, 0)),
        compiler_params=_cparams(("arbitrary", "arbitrary")),
        name="gqa_attn",
    )(qb, kvb)


def _merge_kernel(x_ref, ya_ref, yb_ref, ga_ref, gb_ref, wpa_ref, wpb_ref, wo_ref,
                  gt_ref, g_ref, sh_ref, sc_ref, wr_ref,
                  xn_ref, hrow_ref, e_ref, w_ref):
    tm = x_ref.shape[0]
    pa = jnp.dot(ya_ref[...], wpa_ref[...], preferred_element_type=F32)
    pb = jnp.dot(yb_ref[...], wpb_ref[...], preferred_element_type=F32)
    merged = ga_ref[...].astype(F32) * pa + gb_ref[...].astype(F32) * pb
    out = jnp.dot(merged.astype(BF16), wo_ref[...], preferred_element_type=F32)
    xn = x_ref[...] + gt_ref[0] * out
    xn_ref[...] = xn

    ms = jnp.mean(xn * xn, axis=-1, keepdims=True)
    h = xn * lax.rsqrt(ms + NORM_EPS) * g_ref[...]
    h = h * (1.0 + sc_ref[0]) + sh_ref[0]
    for c in range(ROW_CHUNKS):
        hrow_ref[pl.ds(c, tm, stride=ROW_CHUNKS), :] = h[:, c * LANES:(c + 1) * LANES]

    logits = lax.dot_general(wr_ref[...], h, (((1,), (1,)), ((), ())),
                             preferred_element_type=F32,
                             precision=lax.Precision.HIGHEST)
    mx = jnp.max(logits, axis=0, keepdims=True)
    ex = jnp.exp(logits - mx)
    probs = ex / jnp.sum(ex, axis=0, keepdims=True)
    pg = probs.reshape(N_GROUPS, EXPERTS_PER_GROUP, tm)
    sub = lax.broadcasted_iota(jnp.int32, pg.shape, 1).astype(F32)
    m1 = jnp.max(pg, axis=1, keepdims=True)
    i1 = jnp.min(jnp.where(pg == m1, sub, float(EXPERTS_PER_GROUP)), axis=1, keepdims=True)
    pg2 = jnp.where(sub == i1, -1.0, pg)
    m2 = jnp.max(pg2, axis=1, keepdims=True)
    i2 = jnp.min(jnp.where(pg2 == m2, sub, float(EXPERTS_PER_GROUP)), axis=1, keepdims=True)
    score = m1 + m2
    gid = lax.broadcasted_iota(jnp.int32, score.shape, 0).astype(F32)
    best = jnp.max(score, axis=0, keepdims=True)
    gsel = jnp.min(jnp.where(score == best, gid, float(N_GROUPS)), axis=0, keepdims=True)
    pick = gid == gsel
    w0 = jnp.sum(jnp.where(pick, m1, 0.0), axis=0)
    w1 = jnp.sum(jnp.where(pick, m2, 0.0), axis=0)
    j0 = jnp.sum(jnp.where(pick, i1, 0.0), axis=0)
    j1 = jnp.sum(jnp.where(pick, i2, 0.0), axis=0)
    base = gsel[0] * float(EXPERTS_PER_GROUP)
    tot = w0 + w1
    e_ref[0, 0:1, :] = (base + j0).astype(jnp.int32)
    e_ref[0, 1:2, :] = (base + j1).astype(jnp.int32)
    w_ref[0, 0:1, :] = w0 / tot
    w_ref[0, 1:2, :] = w1 / tot


def _merge(x, ya, yb, ga, gb, wpa, wpb, wo, gt, g, sh, sc, wr_t, tm, tiles_per_seq):
    t, d = x.shape
    nt = t // tm
    row = lambda i: (i, 0)
    per_b = lambda i: (i // tiles_per_seq, 0, 0)
    const2 = lambda i: (0, 0)
    return pl.pallas_call(
        _merge_kernel,
        out_shape=(
            jax.ShapeDtypeStruct((t, d), F32),
            jax.ShapeDtypeStruct((t * ROW_CHUNKS, LANES), F32),
            jax.ShapeDtypeStruct((nt, TOP_K, tm), jnp.int32),
            jax.ShapeDtypeStruct((nt, TOP_K, tm), F32),
        ),
        grid=(nt,),
        in_specs=[
            pl.BlockSpec((tm, d), row),
            pl.BlockSpec((tm, A_WIDTH), row),
            pl.BlockSpec((tm, B_Q_WIDTH), row),
            pl.BlockSpec((tm, d), row),
            pl.BlockSpec((tm, d), row),
            pl.BlockSpec((A_WIDTH, d), const2),
            pl.BlockSpec((B_Q_WIDTH, d), const2),
            pl.BlockSpec((d, d), const2),
            pl.BlockSpec((1, 1, d), per_b),
            pl.BlockSpec((1, d), const2),
            pl.BlockSpec((1, 1, d), per_b),
            pl.BlockSpec((1, 1, d), per_b),
            pl.BlockSpec((N_EXPERTS, d), const2),
        ],
        out_specs=(
            pl.BlockSpec((tm, d), row),
            pl.BlockSpec((tm * ROW_CHUNKS, LANES), row),
            pl.BlockSpec((1, TOP_K, tm), lambda i: (i, 0, 0)),
            pl.BlockSpec((1, TOP_K, tm), lambda i: (i, 0, 0)),
        ),
        compiler_params=_cparams(("arbitrary",)),
        name="merge_router",
    )(x, ya, yb, ga, gb, wpa, wpb, wo, gt, g, sh, sc, wr_t)


_ROW_UNROLL = 8


def _expert_kernel(blk_e_ref, blk_cnt_ref, tok_ref, wrow_ref, h_hbm, wg_ref, wu_ref, wd_ref,
                   acc_hbm, xs, acc, gbuf, ybuf, sem):
    g = pl.program_id(0)
    j = pl.program_id(1)
    nbs = pl.num_programs(1)
    ts = xs.shape[0] // ROW_CHUNKS
    cnt = blk_cnt_ref[g * nbs + j]

    def tile(idx):
        return pl.ds(pl.multiple_of(idx * ROW_CHUNKS, ROW_CHUNKS), ROW_CHUNKS)

    def group_rows():
        return pl.ds(pl.multiple_of(g * (ts * ROW_CHUNKS), ROW_CHUNKS), ts * ROW_CHUNKS)

    @pl.when(j == 0)
    def _():
        load = pltpu.make_async_copy(h_hbm.at[group_rows(), :], xs, sem.at[0])
        load.start()
        acc[...] = jnp.zeros_like(acc)

        @pl.when(g == 0)
        def _():
            gbuf[...] = jnp.zeros_like(gbuf)
        load.wait()

    @pl.when(cnt > 0)
    def _():
        nchunk = (cnt + _ROW_UNROLL - 1) // _ROW_UNROLL

        def gather(c, carry):
            base = c * _ROW_UNROLL
            for u in range(_ROW_UNROLL):
                src = jnp.minimum(tok_ref[0, 0, base + u], ts - 1)
                gbuf[tile(base + u), :] = xs[tile(src), :]
            return carry
        lax.fori_loop(0, nchunk, gather, 0)

        xb = jnp.concatenate(
            [gbuf[pl.ds(c, MOE_BLOCK, stride=ROW_CHUNKS), :] for c in range(ROW_CHUNKS)],
            axis=-1).astype(BF16)
        gate = jnp.dot(xb, wg_ref[0], preferred_element_type=F32)
        up = jnp.dot(xb, wu_ref[0], preferred_element_type=F32)
        hid = (gate * jax.nn.sigmoid(gate) * up).astype(BF16)
        y = jnp.dot(hid, wd_ref[0], preferred_element_type=F32)
        for c in range(ROW_CHUNKS):
            ybuf[pl.ds(c, MOE_BLOCK, stride=ROW_CHUNKS), :] = y[:, c * LANES:(c + 1) * LANES]

        def scatter_add(c, carry):
            base = c * _ROW_UNROLL
            new = []
            for u in range(_ROW_UNROLL):
                dst = tile(tok_ref[0, 0, base + u])
                new.append((dst, acc[dst, :] + wrow_ref[0, 0, base + u] * ybuf[tile(base + u), :]))
            for dst, val in new:
                acc[dst, :] = val
            return carry
        lax.fori_loop(0, nchunk, scatter_add, 0)

    @pl.when(j == nbs - 1)
    def _():
        store = pltpu.make_async_copy(acc.at[pl.ds(0, ts * ROW_CHUNKS), :],
                                      acc_hbm.at[group_rows(), :], sem.at[1])
        store.start()
        store.wait()


def _experts(hrows, plan, wg, wu, wd, ts):
    blk_e, blk_cnt, tok, wrow = plan
    n_groups, nbs = tok.shape[0], tok.shape[1]
    d = D_MODEL
    idx_blk = (1, 1, MOE_BLOCK)
    idx_map = lambda g, j, be, bc: (g * nbs + j, 0, 0)
    w_map = lambda g, j, be, bc: (be[g * nbs + j], 0, 0)
    grid_spec = pltpu.PrefetchScalarGridSpec(
        num_scalar_prefetch=2,
        grid=(n_groups, nbs),
        in_specs=[
            pl.BlockSpec(idx_blk, idx_map, memory_space=pltpu.SMEM),
            pl.BlockSpec(idx_blk, idx_map, memory_space=pltpu.SMEM),
            pl.BlockSpec(memory_space=pl.ANY),
            pl.BlockSpec((1, d, D_FF_EXPERT), w_map),
            pl.BlockSpec((1, d, D_FF_EXPERT), w_map),
            pl.BlockSpec((1, D_FF_EXPERT, d), w_map),
        ],
        out_specs=pl.BlockSpec(memory_space=pl.ANY),
        scratch_shapes=[
            pltpu.VMEM((ts * ROW_CHUNKS, LANES), F32),
            pltpu.VMEM(((ts + 1) * ROW_CHUNKS, LANES), F32),
            pltpu.VMEM((MOE_BLOCK * ROW_CHUNKS, LANES), F32),
            pltpu.VMEM((MOE_BLOCK * ROW_CHUNKS, LANES), F32),
            pltpu.SemaphoreType.DMA((2,)),
        ],
    )
    return pl.pallas_call(
        _expert_kernel,
        out_shape=jax.ShapeDtypeStruct(hrows.shape, F32),
        grid_spec=grid_spec,
        compiler_params=_cparams(("arbitrary", "arbitrary")),
        name="experts",
    )(blk_e, blk_cnt, tok.reshape(n_groups * nbs, 1, MOE_BLOCK),
      wrow.reshape(n_groups * nbs, 1, MOE_BLOCK), hrows, wg, wu, wd)


def _combine_kernel(x_ref, y_ref, gt_ref, o_ref):
    tm = x_ref.shape[0]
    gt = gt_ref[0]
    for c in range(ROW_CHUNKS):
        cols = slice(c * LANES, (c + 1) * LANES)
        o_ref[:, cols] = x_ref[:, cols] + gt[:, cols] * y_ref[pl.ds(c, tm, stride=ROW_CHUNKS), :]


def _combine(xn, yrows, gt, tm, tiles_per_seq):
    t, d = xn.shape
    nt = t // tm
    return pl.pallas_call(
        _combine_kernel,
        out_shape=jax.ShapeDtypeStruct((t, d), F32),
        grid=(nt,),
        in_specs=[
            pl.BlockSpec((tm, d), lambda i: (i, 0)),
            pl.BlockSpec((tm * ROW_CHUNKS, LANES), lambda i: (i, 0)),
            pl.BlockSpec((1, 1, d), lambda i: (i // tiles_per_seq, 0, 0)),
        ],
        out_specs=pl.BlockSpec((tm, d), lambda i: (i, 0)),
        compiler_params=_cparams(("arbitrary",)),
        name="moe_combine",
    )(xn, yrows, gt)


def _dispatch_plan(e_sel, w_sel, ts):
    t = e_sel.shape[0]
    n_groups = t // ts
    na = ts * TOP_K
    nbs = na // MOE_BLOCK + N_EXPERTS
    n_slots = nbs * MOE_BLOCK
    e_flat = e_sel.reshape(n_groups, ts, TOP_K).transpose(0, 2, 1).reshape(n_groups, na)
    w_flat = w_sel.reshape(n_groups, ts, TOP_K).transpose(0, 2, 1).reshape(n_groups, na)
    order = jnp.argsort(e_flat, axis=1).astype(jnp.int32)
    experts = jnp.arange(N_EXPERTS, dtype=jnp.int32)
    counts = jnp.sum((e_flat[:, :, None] == experts[None, None, :]).astype(jnp.int32), axis=1)
    padded = (counts + MOE_BLOCK - 1) // MOE_BLOCK * MOE_BLOCK
    pad_end = jnp.cumsum(padded, axis=1)
    pad_start = pad_end - padded
    start = jnp.cumsum(counts, axis=1) - counts
    slot = jnp.arange(n_slots, dtype=jnp.int32)
    slot_e = jnp.minimum(
        jnp.sum((slot[None, :, None] >= pad_end[:, None, :]).astype(jnp.int32), axis=2),
        N_EXPERTS - 1)
    take = lambda table, idx: jnp.take_along_axis(table, idx, axis=1)
    off = slot[None, :] - take(pad_start, slot_e)
    left = take(counts, slot_e) - off
    valid = left > 0
    src = jnp.clip(take(start, slot_e) + off, 0, na - 1)
    a = take(order, src)
    tok = jnp.where(valid, a % ts, ts).astype(jnp.int32)
    wrow = jnp.where(valid, take(w_flat, a), 0.0).astype(F32)
    first = slice(None, None, MOE_BLOCK)
    blk_e = slot_e[:, first].reshape(n_groups * nbs)
    blk_cnt = jnp.clip(left[:, first], 0, MOE_BLOCK).astype(jnp.int32).reshape(n_groups * nbs)
    return (blk_e, blk_cnt, tok.reshape(n_groups, nbs, MOE_BLOCK),
            wrow.reshape(n_groups, nbs, MOE_BLOCK))


def kernel(x, c, w_ada, b_ada, g_mix, w_in, qn_a, kn_a, qn_b, kn_b, w_pa, w_pb, w_out,
           g_ffn, w_router, w_gate, w_up, w_down):
    bsz, seq, d = x.shape
    depth = w_ada.shape[0]
    t = bsz * seq
    tm = 512
    tiles_per_seq = seq // tm
    ts = min(4096, t)

    mod = _adaln(c, w_ada, b_ada)
    tables = _rope_tables(seq)
    bd = jnp.asarray(np.kron(np.eye(A_HEADS), np.ones((HEAD_DIM, HEAD_DIM))), BF16)

    g4 = B_Q_HEADS // B_KV_HEADS
    head_perm = np.concatenate(
        [np.r_[np.arange(j * HEAD_DIM, (j + 1) * HEAD_DIM),
               np.arange((j + g4) * HEAD_DIM, (j + g4 + 1) * HEAD_DIM)] for j in range(g4)])
    qb_lo = 3 * A_WIDTH
    col_perm = np.arange(w_in.shape[2])
    col_perm[qb_lo:qb_lo + B_Q_WIDTH] = qb_lo + head_perm
    scale = HEAD_DIM ** -0.5
    wr_t = w_router.T

    xf = x
    for l in range(depth):
        sh1, sc1, gt1, sh2, sc2, gt2 = [
            mod[l, :, i * d:(i + 1) * d].reshape(bsz, 1, d) for i in range(6)]
        w_in_l = w_in[l][:, col_perm].astype(BF16)
        gains = (
            jnp.tile(qn_a[l] * scale, A_HEADS).reshape(1, A_WIDTH),
            jnp.tile(kn_a[l], A_HEADS).reshape(1, A_WIDTH),
            jnp.tile(qn_b[l] * scale, B_Q_HEADS).reshape(1, B_Q_WIDTH),
            jnp.tile(kn_b[l], B_KV_HEADS).reshape(1, B_KV_WIDTH),
        )
        za, qb, kvb, ga, gb = _in_proj(xf.reshape(bsz, seq, d), sh1, sc1,
                                       g_mix[l].reshape(1, d), w_in_l, bd, gains, tables, tm)
        ya = _dilated(za)
        yb = _gqa(qb, kvb, 256)
        xn, hrows, e_sel, w_sel = _merge(
            xf.reshape(t, d), ya.reshape(t, A_WIDTH), yb.reshape(t, B_Q_WIDTH),
            ga.reshape(t, d), gb.reshape(t, d),
            w_pa[l].astype(BF16), w_pb[l][head_perm].astype(BF16), w_out[l].astype(BF16),
            gt1, g_ffn[l].reshape(1, d), sh2, sc2, wr_t, tm, tiles_per_seq)
        e_tok = e_sel.transpose(0, 2, 1).reshape(t, TOP_K)
        w_tok = w_sel.transpose(0, 2, 1).reshape(t, TOP_K)
        plan = _dispatch_plan(e_tok, w_tok, ts)
        yrows = _experts(hrows, plan, w_gate[l].astype(BF16), w_up[l].astype(BF16),
                         w_down[l].astype(BF16), ts)
        xf = _combine(xn, yrows, gt2, tm, tiles_per_seq)
    return xf.reshape(bsz, seq, d)
```

```python
import functools

import jax
import jax.numpy as jnp
import numpy as np
from jax import lax
from jax.experimental import pallas as pl
from jax.experimental.pallas import tpu as pltpu

D_MODEL = 1024
HEAD_DIM = 64
NORM_EPS = 1e-6
A_HEADS = 8
A_WIDTH = A_HEADS * HEAD_DIM
A_PATTERNS = ((128, 1), (512, 4), (2048, 16))
A_ROT_DIMS = HEAD_DIM // 4
A_ROPE_THETA = 500000.0
B_Q_HEADS = 8
B_KV_HEADS = 2
B_Q_WIDTH = B_Q_HEADS * HEAD_DIM
B_KV_WIDTH = B_KV_HEADS * HEAD_DIM
B_ROPE_THETA = 10000.0
GRID_W = 64
N_EXPERTS = 32
N_GROUPS = 4
EXPERTS_PER_GROUP = N_EXPERTS // N_GROUPS
TOP_K = 2
D_FF_EXPERT = D_MODEL // 2
MOE_BLOCK = 256

LANES = 128
SUBLANES = 8
ROW_CHUNKS = D_MODEL // LANES
VMEM_LIMIT = 56 * 1024 * 1024

F32 = jnp.float32
BF16 = jnp.bfloat16


def _cparams(sem):
    return pltpu.CompilerParams(dimension_semantics=sem, vmem_limit_bytes=VMEM_LIMIT)


def _adaln_kernel(c_ref, w_ref, b_ref, o_ref):
    c = c_ref[...]
    cond = c * jax.nn.sigmoid(c)
    o_ref[0] = jnp.dot(cond, w_ref[0], preferred_element_type=F32,
                       precision=lax.Precision.HIGHEST) + b_ref[0]


def _adaln(c, w_ada, b_ada):
    depth, d, n = w_ada.shape
    bsz = c.shape[0]
    tn = 1024
    return pl.pallas_call(
        _adaln_kernel,
        out_shape=jax.ShapeDtypeStruct((depth, bsz, n), F32),
        grid=(depth, n // tn),
        in_specs=[
            pl.BlockSpec((bsz, d), lambda l, j: (0, 0)),
            pl.BlockSpec((1, d, tn), lambda l, j: (l, 0, j)),
            pl.BlockSpec((1, 1, tn), lambda l, j: (l, 0, j)),
        ],
        out_specs=pl.BlockSpec((1, bsz, tn), lambda l, j: (l, 0, j)),
        compiler_params=_cparams(("arbitrary", "arbitrary")),
        name="adaln",
    )(c, w_ada, b_ada.reshape(depth, 1, n))


def _rope_tables(seq):
    pos = jnp.arange(seq, dtype=F32)
    row = jnp.floor(pos / GRID_W)
    col = pos - row * GRID_W
    d = np.arange(LANES) % HEAD_DIM

    def build(segments):
        c = jnp.ones((seq, LANES), F32)
        s1 = jnp.zeros((seq, LANES), F32)
        s2 = jnp.zeros((seq, LANES), F32)
        for lo, half, theta, p in segments:
            first = (d >= lo) & (d < lo + half)
            second = (d >= lo + half) & (d < lo + 2 * half)
            idx = np.where(first, d - lo, np.where(second, d - lo - half, 0))
            freqs = theta ** (-jnp.arange(half, dtype=F32) / half)
            ang = p[:, None] * freqs[idx][None, :]
            cs, sn = jnp.cos(ang), jnp.sin(ang)
            rot = jnp.asarray(first | second)[None, :]
            c = jnp.where(rot, cs, c)
            s1 = jnp.where(jnp.asarray(first)[None, :], -sn, s1)
            s2 = jnp.where(jnp.asarray(second)[None, :], sn, s2)
        return c, s1, s2

    ta = build([(0, A_ROT_DIMS // 2, A_ROPE_THETA, pos)])
    q = HEAD_DIM // 4
    tb = build([(0, q, B_ROPE_THETA, row), (2 * q, q, B_ROPE_THETA, col)])
    return ta + tb


def _in_proj_kernel(x_ref, sh_ref, sc_ref, g_ref, w_ref, bd_ref,
                    gqa_ref, gka_ref, gqb_ref, gkb_ref,
                    ca_ref, s1a_ref, s2a_ref, cb_ref, s1b_ref, s2b_ref,
                    za_ref, qb_ref, kb_ref, vbt_ref, ga_ref, gb_ref):
    x = x_ref[0]
    ms = jnp.mean(x * x, axis=-1, keepdims=True)
    h = x * lax.rsqrt(ms + NORM_EPS) * g_ref[...]
    h = h * (1.0 + sc_ref[0]) + sh_ref[0]
    hb = h.astype(BF16)

    def seg(lo, width):
        return jnp.dot(hb, w_ref[:, lo:lo + width], preferred_element_type=F32)

    def qk_norm(z, gain_ref):
        width = z.shape[-1]
        bdw = min(width, bd_ref.shape[0])
        sq = (z * z).astype(BF16)
        parts = [jnp.dot(sq[:, lo:lo + bdw], bd_ref[:bdw, :bdw], preferred_element_type=F32)
                 for lo in range(0, width, bdw)]
        ss = parts[0] if len(parts) == 1 else jnp.concatenate(parts, axis=-1)
        return z * lax.rsqrt(ss * (1.0 / HEAD_DIM) + NORM_EPS) * gain_ref[...]

    def tile(t, width):
        reps = width // LANES
        return t if reps == 1 else jnp.concatenate([t] * reps, axis=-1)

    def rope(z, c_ref, s1_ref, s2_ref, half):
        width = z.shape[-1]
        up = pltpu.roll(z, width - half, 1)
        dn = pltpu.roll(z, half, 1)
        return (z * tile(c_ref[...], width) + up * tile(s1_ref[...], width)
                + dn * tile(s2_ref[...], width))

    ha = A_ROT_DIMS // 2
    hq = HEAD_DIM // 4
    o = 0
    qa = rope(qk_norm(seg(o, A_WIDTH), gqa_ref), ca_ref, s1a_ref, s2a_ref, ha)
    za_ref[0, :, 0:A_WIDTH] = qa
    o += A_WIDTH
    ka = rope(qk_norm(seg(o, A_WIDTH), gka_ref), ca_ref, s1a_ref, s2a_ref, ha)
    za_ref[0, :, A_WIDTH:2 * A_WIDTH] = ka
    o += A_WIDTH
    za_ref[0, :, 2 * A_WIDTH:3 * A_WIDTH] = seg(o, A_WIDTH)
    o += A_WIDTH
    qb = rope(qk_norm(seg(o, B_Q_WIDTH), gqb_ref), cb_ref, s1b_ref, s2b_ref, hq)
    qb_ref[0] = qb.astype(BF16)
    o += B_Q_WIDTH
    kb = rope(qk_norm(seg(o, B_KV_WIDTH), gkb_ref), cb_ref, s1b_ref, s2b_ref, hq)
    kb_ref[0] = kb.astype(BF16)
    o += B_KV_WIDTH
    vbt_ref[0] = seg(o, B_KV_WIDTH).T.astype(BF16)
    o += B_KV_WIDTH
    ga_ref[0] = jax.nn.sigmoid(seg(o, D_MODEL)).astype(BF16)
    o += D_MODEL
    gb_ref[0] = jax.nn.sigmoid(seg(o, D_MODEL)).astype(BF16)


def _in_proj(x, sh, sc, g, w_in, bd, gains, tables, tm):
    bsz, seq, d = x.shape
    n_in = w_in.shape[1]
    nst = seq // tm
    row = lambda st, b: (b, st, 0)
    per_b = lambda st, b: (b, 0, 0)
    const2 = lambda st, b: (0, 0)
    tab = pl.BlockSpec((tm, LANES), lambda st, b: (st, 0))
    return pl.pallas_call(
        _in_proj_kernel,
        out_shape=(
            jax.ShapeDtypeStruct((bsz, seq, 3 * A_WIDTH), F32),
            jax.ShapeDtypeStruct((bsz, seq, B_Q_WIDTH), BF16),
            jax.ShapeDtypeStruct((bsz, seq, B_KV_WIDTH), BF16),
            jax.ShapeDtypeStruct((bsz, B_KV_WIDTH, seq), BF16),
            jax.ShapeDtypeStruct((bsz, seq, d), BF16),
            jax.ShapeDtypeStruct((bsz, seq, d), BF16),
        ),
        grid=(nst, bsz),
        in_specs=[
            pl.BlockSpec((1, tm, d), row),
            pl.BlockSpec((1, 1, d), per_b),
            pl.BlockSpec((1, 1, d), per_b),
            pl.BlockSpec((1, d), const2),
            pl.BlockSpec((d, n_in), const2),
            pl.BlockSpec(bd.shape, const2),
            pl.BlockSpec((1, A_WIDTH), const2),
            pl.BlockSpec((1, A_WIDTH), const2),
            pl.BlockSpec((1, B_Q_WIDTH), const2),
            pl.BlockSpec((1, B_KV_WIDTH), const2),
            tab, tab, tab, tab, tab, tab,
        ],
        out_specs=(
            pl.BlockSpec((1, tm, 3 * A_WIDTH), row),
            pl.BlockSpec((1, tm, B_Q_WIDTH), row),
            pl.BlockSpec((1, tm, B_KV_WIDTH), row),
            pl.BlockSpec((1, B_KV_WIDTH, tm), lambda st, b: (b, 0, st)),
            pl.BlockSpec((1, tm, d), row),
            pl.BlockSpec((1, tm, d), row),
        ),
        compiler_params=_cparams(("arbitrary", "arbitrary")),
        name="in_proj",
    )(x, sh, sc, g, w_in, bd, *gains, *tables)


_QB = 128
_UNROLL = 4


def _dilated_kernel(q_ref, k_ref, v_ref, o_ref, m_scr, l_scr, acc_scr, bias_scr):
    seq = q_ref.shape[1]
    lane = lax.broadcasted_iota(jnp.int32, (1, LANES), 1)
    head0 = lane < HEAD_DIM
    radius = A_PATTERNS[0][0] // (2 * A_PATTERNS[0][1])
    assert all(w // (2 * d) == radius for w, d in A_PATTERNS)

    @pl.when((pl.program_id(0) == 0) & (pl.program_id(1) == 0))
    def _():
        qrow = lax.broadcasted_iota(jnp.int32, (2 * _QB, 2 * _QB), 0) & (_QB - 1)
        kcol = lax.broadcasted_iota(jnp.int32, (2 * _QB, 2 * _QB), 1)
        for i in range(3):
            bias_scr[i] = jnp.where(jnp.abs(kcol - qrow - radius * i) <= radius, 0.0, -jnp.inf)

    def merge(t):
        return jnp.where(head0, t[:_QB], t[_QB:])

    for pi, (window, dil) in enumerate(A_PATTERNS):
        sub_len = seq // dil
        kw = min(2 * _QB, sub_len)
        nqb = sub_len // _QB
        nblk = dil * nqb

        def wide(t, kw=kw):
            return t if kw == LANES else jnp.concatenate([t] * (kw // LANES), axis=-1)

        def load(blk, pi=pi, dil=dil, sub_len=sub_len, kw=kw, nqb=nqb):
            r = blk // nqb
            i0q = (blk % nqb) * _QB
            i0k = jnp.clip(i0q - (kw - _QB) // 2, 0, sub_len - kw)
            if dil == 1:
                qrows = pl.ds(pl.multiple_of(i0q, SUBLANES), _QB)
                krows = pl.ds(pl.multiple_of(i0k, SUBLANES), kw)
            else:
                qrows = pl.ds(r + dil * i0q, _QB, stride=dil)
                krows = pl.ds(r + dil * i0k, kw, stride=dil)
            q = q_ref[0, qrows, :]
            kk = k_ref[0, krows, :].astype(BF16)
            vv = v_ref[0, krows, :].astype(BF16)
            old = None
            if pi > 0:
                old = (jnp.concatenate([m_scr[0, qrows, :], m_scr[1, qrows, :]], axis=0),
                       l_scr[qrows, :], acc_scr[qrows, :])
            return qrows, (i0q - i0k) // radius, q, kk, vv, old

        def compute(mask_id, q, kk, vv, old, kw=kw, wide=wide):
            bias = bias_scr[0, :, 0:LANES] if kw == LANES else bias_scr[mask_id]
            q2 = jnp.concatenate([jnp.where(head0, q, 0.0), jnp.where(head0, 0.0, q)],
                                 axis=0).astype(BF16)
            s = lax.dot_general(q2, kk, (((1,), (1,)), ((), ())),
                                preferred_element_type=F32) + bias
            mb = jnp.broadcast_to(jnp.max(s, axis=-1, keepdims=True), (2 * _QB, LANES))
            m_new = mb if old is None else jnp.maximum(old[0], mb)
            p = jnp.exp(s - wide(m_new))
            psum = merge(jnp.broadcast_to(jnp.sum(p, axis=-1, keepdims=True),
                                          (2 * _QB, LANES)))
            pv = merge(jnp.dot(p.astype(BF16), vv, preferred_element_type=F32))
            if old is None:
                return m_new, psum, pv
            alpha = jnp.exp(merge(old[0]) - merge(m_new))
            return m_new, alpha * old[1] + psum, alpha * old[2] + pv

        def group(it, carry, load=load, compute=compute):
            loaded = [load(it * _UNROLL + u) for u in range(_UNROLL)]
            results = [compute(*ld[1:]) for ld in loaded]
            for ld, (m_new, l_new, acc_new) in zip(loaded, results):
                qrows = ld[0]
                m_scr[0, qrows, :] = m_new[:_QB]
                m_scr[1, qrows, :] = m_new[_QB:]
                l_scr[qrows, :] = l_new
                acc_scr[qrows, :] = acc_new
            return carry

        lax.fori_loop(0, nblk // _UNROLL, group, 0)

    def finish(i, carry):
        rows = pl.ds(pl.multiple_of(i * 256, 256), 256)
        o_ref[0, rows, :] = (acc_scr[rows, :] / l_scr[rows, :]).astype(o_ref.dtype)
        return carry

    lax.fori_loop(0, seq // 256, finish, 0)


def _dilated(za):
    bsz, seq, _ = za.shape
    nhp = A_WIDTH // LANES
    blk = (1, seq, LANES)
    return pl.pallas_call(
        _dilated_kernel,
        out_shape=jax.ShapeDtypeStruct((bsz, seq, A_WIDTH), BF16),
        grid=(bsz, nhp),
        in_specs=[
            pl.BlockSpec(blk, lambda b, j: (b, 0, j)),
            pl.BlockSpec(blk, lambda b, j: (b, 0, nhp + j)),
            pl.BlockSpec(blk, lambda b, j: (b, 0, 2 * nhp + j)),
        ],
        out_specs=pl.BlockSpec(blk, lambda b, j: (b, 0, j)),
        scratch_shapes=[
            pltpu.VMEM((2, seq, LANES), F32),
            pltpu.VMEM((seq, LANES), F32),
            pltpu.VMEM((seq, LANES), F32),
            pltpu.VMEM((3, 2 * _QB, 2 * _QB), F32),
        ],
        compiler_params=_cparams(("arbitrary", "arbitrary")),
        name="dilated_attn",
    )(za, za, za)


def _gqa_kernel(q_ref, k_ref, vt_ref, o_ref):
    tq = q_ref.shape[1]
    lane = lax.broadcasted_iota(jnp.int32, (1, LANES), 1)
    head0 = lane < HEAD_DIM
    kk = k_ref[0]
    zero = jnp.zeros((), BF16)
    for j in range(B_Q_WIDTH // LANES):
        qj = q_ref[0, :, j * LANES:(j + 1) * LANES]
        q2 = jnp.concatenate([jnp.where(head0, qj, zero), jnp.where(head0, zero, qj)], axis=0)
        st = lax.dot_general(kk, q2, (((1,), (1,)), ((), ())),
                             preferred_element_type=F32)
        m = jnp.max(st, axis=0, keepdims=True)
        p = jnp.exp(st - m)
        l = jnp.sum(p, axis=0, keepdims=True)
        pb = p.astype(BF16)
        o0 = jnp.dot(vt_ref[0, 0:HEAD_DIM, :], pb[:, :tq], preferred_element_type=F32)
        o1 = jnp.dot(vt_ref[0, HEAD_DIM:2 * HEAD_DIM, :], pb[:, tq:],
                     preferred_element_type=F32)
        ot = jnp.concatenate([o0 / l[:, :tq], o1 / l[:, tq:]], axis=0)
        o_ref[0, :, j * LANES:(j + 1) * LANES] = ot.T.astype(o_ref.dtype)


def _gqa(qb, kb, vbt, tq):
    bsz, seq, _ = qb.shape
    return pl.pallas_call(
        _gqa_kernel,
        out_shape=jax.ShapeDtypeStruct((bsz, seq, B_Q_WIDTH), BF16),
        grid=(bsz, seq // tq),
        in_specs=[
            pl.BlockSpec((1, tq, B_Q_WIDTH), lambda b, i: (b, i, 0)),
            pl.BlockSpec((1, seq, B_KV_WIDTH), lambda b, i: (b, 0, 0)),
            pl.BlockSpec((1, B_KV_WIDTH, seq), lambda b, i: (b, 0, 0)),
        ],
        out_specs=pl.BlockSpec((1, tq, B_Q_WIDTH), lambda b, i: (b, i, 0)),
        compiler_params=_cparams(("arbitrary", "arbitrary")),
        name="gqa_attn",
    )(qb, kb, vbt)


def _merge_kernel(x_ref, ya_ref, yb_ref, ga_ref, gb_ref, wpa_ref, wpb_ref, wo_ref,
                  gt_ref, g_ref, sh_ref, sc_ref, wr_ref,
                  xn_ref, hrow_ref, e_ref, w_ref):
    tm = x_ref.shape[0]
    pa = jnp.dot(ya_ref[...], wpa_ref[...], preferred_element_type=F32)
    pb = jnp.dot(yb_ref[...], wpb_ref[...], preferred_element_type=F32)
    merged = ga_ref[...].astype(F32) * pa + gb_ref[...].astype(F32) * pb
    out = jnp.dot(merged.astype(BF16), wo_ref[...], preferred_element_type=F32)
    xn = x_ref[...] + gt_ref[0] * out
    xn_ref[...] = xn

    ms = jnp.mean(xn * xn, axis=-1, keepdims=True)
    h = xn * lax.rsqrt(ms + NORM_EPS) * g_ref[...]
    h = h * (1.0 + sc_ref[0]) + sh_ref[0]
    for c in range(ROW_CHUNKS):
        hrow_ref[pl.ds(c, tm, stride=ROW_CHUNKS), :] = h[:, c * LANES:(c + 1) * LANES]

    logits = lax.dot_general(wr_ref[...], h, (((1,), (1,)), ((), ())),
                             preferred_element_type=F32,
                             precision=lax.Precision.HIGHEST)
    mx = jnp.max(logits, axis=0, keepdims=True)
    ex = jnp.exp(logits - mx)
    probs = ex / jnp.sum(ex, axis=0, keepdims=True)
    pg = probs.reshape(N_GROUPS, EXPERTS_PER_GROUP, tm)
    sub = lax.broadcasted_iota(jnp.int32, pg.shape, 1).astype(F32)
    m1 = jnp.max(pg, axis=1, keepdims=True)
    i1 = jnp.min(jnp.where(pg == m1, sub, float(EXPERTS_PER_GROUP)), axis=1, keepdims=True)
    pg2 = jnp.where(sub == i1, -1.0, pg)
    m2 = jnp.max(pg2, axis=1, keepdims=True)
    i2 = jnp.min(jnp.where(pg2 == m2, sub, float(EXPERTS_PER_GROUP)), axis=1, keepdims=True)
    score = m1 + m2
    gid = lax.broadcasted_iota(jnp.int32, score.shape, 0).astype(F32)
    best = jnp.max(score, axis=0, keepdims=True)
    gsel = jnp.min(jnp.where(score == best, gid, float(N_GROUPS)), axis=0, keepdims=True)
    pick = gid == gsel
    w0 = jnp.sum(jnp.where(pick, m1, 0.0), axis=0)
    w1 = jnp.sum(jnp.where(pick, m2, 0.0), axis=0)
    j0 = jnp.sum(jnp.where(pick, i1, 0.0), axis=0)
    j1 = jnp.sum(jnp.where(pick, i2, 0.0), axis=0)
    base = gsel[0] * float(EXPERTS_PER_GROUP)
    tot = w0 + w1
    e_ref[0, 0:1, :] = (base + j0).astype(jnp.int32)
    e_ref[0, 1:2, :] = (base + j1).astype(jnp.int32)
    w_ref[0, 0:1, :] = w0 / tot
    w_ref[0, 1:2, :] = w1 / tot


def _merge(x, ya, yb, ga, gb, wpa, wpb, wo, gt, g, sh, sc, wr_t, tm, tiles_per_seq):
    t, d = x.shape
    nt = t // tm
    row = lambda i: (i, 0)
    per_b = lambda i: (i // tiles_per_seq, 0, 0)
    const2 = lambda i: (0, 0)
    return pl.pallas_call(
        _merge_kernel,
        out_shape=(
            jax.ShapeDtypeStruct((t, d), F32),
            jax.ShapeDtypeStruct((t * ROW_CHUNKS, LANES), F32),
            jax.ShapeDtypeStruct((nt, TOP_K, tm), jnp.int32),
            jax.ShapeDtypeStruct((nt, TOP_K, tm), F32),
        ),
        grid=(nt,),
        in_specs=[
            pl.BlockSpec((tm, d), row),
            pl.BlockSpec((tm, A_WIDTH), row),
            pl.BlockSpec((tm, B_Q_WIDTH), row),
            pl.BlockSpec((tm, d), row),
            pl.BlockSpec((tm, d), row),
            pl.BlockSpec((A_WIDTH, d), const2),
            pl.BlockSpec((B_Q_WIDTH, d), const2),
            pl.BlockSpec((d, d), const2),
            pl.BlockSpec((1, 1, d), per_b),
            pl.BlockSpec((1, d), const2),
            pl.BlockSpec((1, 1, d), per_b),
            pl.BlockSpec((1, 1, d), per_b),
            pl.BlockSpec((N_EXPERTS, d), const2),
        ],
        out_specs=(
            pl.BlockSpec((tm, d), row),
            pl.BlockSpec((tm * ROW_CHUNKS, LANES), row),
            pl.BlockSpec((1, TOP_K, tm), lambda i: (i, 0, 0)),
            pl.BlockSpec((1, TOP_K, tm), lambda i: (i, 0, 0)),
        ),
        compiler_params=_cparams(("arbitrary",)),
        name="merge_router",
    )(x, ya, yb, ga, gb, wpa, wpb, wo, gt, g, sh, sc, wr_t)


_ROW_UNROLL = 8


def _expert_kernel(blk_e_ref, blk_cnt_ref, tok_ref, wrow_ref, h_hbm, wg_ref, wu_ref, wd_ref,
                   acc_hbm, xs, acc, gbuf, ybuf, sem):
    g = pl.program_id(0)
    j = pl.program_id(1)
    nbs = pl.num_programs(1)
    ts = xs.shape[0] // ROW_CHUNKS
    cnt = blk_cnt_ref[g * nbs + j]

    def tile(idx):
        return pl.ds(pl.multiple_of(idx * ROW_CHUNKS, ROW_CHUNKS), ROW_CHUNKS)

    def group_rows():
        return pl.ds(pl.multiple_of(g * (ts * ROW_CHUNKS), ROW_CHUNKS), ts * ROW_CHUNKS)

    @pl.when(j == 0)
    def _():
        load = pltpu.make_async_copy(h_hbm.at[group_rows(), :], xs, sem.at[0])
        load.start()
        acc[...] = jnp.zeros_like(acc)

        @pl.when(g == 0)
        def _():
            gbuf[...] = jnp.zeros_like(gbuf)
        load.wait()

    @pl.when(cnt > 0)
    def _():
        nchunk = (cnt + _ROW_UNROLL - 1) // _ROW_UNROLL

        def gather(c, carry):
            base = c * _ROW_UNROLL
            for u in range(_ROW_UNROLL):
                src = jnp.minimum(tok_ref[0, 0, base + u], ts - 1)
                gbuf[tile(base + u), :] = xs[tile(src), :]
            return carry
        lax.fori_loop(0, nchunk, gather, 0)

        xb = jnp.concatenate(
            [gbuf[pl.ds(c, MOE_BLOCK, stride=ROW_CHUNKS), :] for c in range(ROW_CHUNKS)],
            axis=-1).astype(BF16)
        gate = jnp.dot(xb, wg_ref[0], preferred_element_type=F32)
        up = jnp.dot(xb, wu_ref[0], preferred_element_type=F32)
        hid = (gate * jax.nn.sigmoid(gate) * up).astype(BF16)
        y = jnp.dot(hid, wd_ref[0], preferred_element_type=F32)
        for c in range(ROW_CHUNKS):
            ybuf[pl.ds(c, MOE_BLOCK, stride=ROW_CHUNKS), :] = y[:, c * LANES:(c + 1) * LANES]

        def scatter_add(c, carry):
            base = c * _ROW_UNROLL
            new = []
            for u in range(_ROW_UNROLL):
                dst = tile(tok_ref[0, 0, base + u])
                new.append((dst, acc[dst, :] + wrow_ref[0, 0, base + u] * ybuf[tile(base + u), :]))
            for dst, val in new:
                acc[dst, :] = val
            return carry
        lax.fori_loop(0, nchunk, scatter_add, 0)

    @pl.when(j == nbs - 1)
    def _():
        store = pltpu.make_async_copy(acc.at[pl.ds(0, ts * ROW_CHUNKS), :],
                                      acc_hbm.at[group_rows(), :], sem.at[1])
        store.start()
        store.wait()


def _experts(hrows, plan, wg, wu, wd, ts):
    blk_e, blk_cnt, tok, wrow = plan
    n_groups, nbs = tok.shape[0], tok.shape[1]
    d = D_MODEL
    idx_blk = (1, 1, MOE_BLOCK)
    idx_map = lambda g, j, be, bc: (g * nbs + j, 0, 0)
    w_map = lambda g, j, be, bc: (be[g * nbs + j], 0, 0)
    grid_spec = pltpu.PrefetchScalarGridSpec(
        num_scalar_prefetch=2,
        grid=(n_groups, nbs),
        in_specs=[
            pl.BlockSpec(idx_blk, idx_map, memory_space=pltpu.SMEM),
            pl.BlockSpec(idx_blk, idx_map, memory_space=pltpu.SMEM),
            pl.BlockSpec(memory_space=pl.ANY),
            pl.BlockSpec((1, d, D_FF_EXPERT), w_map),
            pl.BlockSpec((1, d, D_FF_EXPERT), w_map),
            pl.BlockSpec((1, D_FF_EXPERT, d), w_map),
        ],
        out_specs=pl.BlockSpec(memory_space=pl.ANY),
        scratch_shapes=[
            pltpu.VMEM((ts * ROW_CHUNKS, LANES), F32),
            pltpu.VMEM(((ts + 1) * ROW_CHUNKS, LANES), F32),
            pltpu.VMEM((MOE_BLOCK * ROW_CHUNKS, LANES), F32),
            pltpu.VMEM((MOE_BLOCK * ROW_CHUNKS, LANES), F32),
            pltpu.SemaphoreType.DMA((2,)),
        ],
    )
    return pl.pallas_call(
        _expert_kernel,
        out_shape=jax.ShapeDtypeStruct(hrows.shape, F32),
        grid_spec=grid_spec,
        compiler_params=_cparams(("arbitrary", "arbitrary")),
        name="experts",
    )(blk_e, blk_cnt, tok.reshape(n_groups * nbs, 1, MOE_BLOCK),
      wrow.reshape(n_groups * nbs, 1, MOE_BLOCK), hrows, wg, wu, wd)


def _combine_kernel(x_ref, y_ref, gt_ref, o_ref):
    tm = x_ref.shape[0]
    gt = gt_ref[0]
    for c in range(ROW_CHUNKS):
        cols = slice(c * LANES, (c + 1) * LANES)
        o_ref[:, cols] = x_ref[:, cols] + gt[:, cols] * y_ref[pl.ds(c, tm, stride=ROW_CHUNKS), :]


def _combine(xn, yrows, gt, tm, tiles_per_seq):
    t, d = xn.shape
    nt = t // tm
    return pl.pallas_call(
        _combine_kernel,
        out_shape=jax.ShapeDtypeStruct((t, d), F32),
        grid=(nt,),
        in_specs=[
            pl.BlockSpec((tm, d), lambda i: (i, 0)),
            pl.BlockSpec((tm * ROW_CHUNKS, LANES), lambda i: (i, 0)),
            pl.BlockSpec((1, 1, d), lambda i: (i // tiles_per_seq, 0, 0)),
        ],
        out_specs=pl.BlockSpec((tm, d), lambda i: (i, 0)),
        compiler_params=_cparams(("arbitrary",)),
        name="moe_combine",
    )(xn, yrows, gt)


def _dispatch_plan(e_sel, w_sel, ts):
    t = e_sel.shape[0]
    n_groups = t // ts
    na = ts * TOP_K
    nbs = na // MOE_BLOCK + N_EXPERTS
    n_slots = nbs * MOE_BLOCK
    e_flat = e_sel.reshape(n_groups, ts, TOP_K).transpose(0, 2, 1).reshape(n_groups, na)
    w_flat = w_sel.reshape(n_groups, ts, TOP_K).transpose(0, 2, 1).reshape(n_groups, na)
    a_ids = jnp.broadcast_to(jnp.arange(na, dtype=jnp.int32)[None, :], (n_groups, na))
    _, a_sorted, w_sorted = lax.sort((e_flat, a_ids, w_flat), dimension=1, num_keys=1)
    experts = jnp.arange(N_EXPERTS, dtype=jnp.int32)
    counts = jnp.sum((e_flat[:, :, None] == experts[None, None, :]).astype(jnp.int32), axis=1)
    padded = (counts + MOE_BLOCK - 1) // MOE_BLOCK * MOE_BLOCK
    pad_end = jnp.cumsum(padded, axis=1)
    pad_start = pad_end - padded
    start = jnp.cumsum(counts, axis=1) - counts
    slot = jnp.arange(n_slots, dtype=jnp.int32)
    slot_e = jnp.minimum(
        jnp.sum((slot[None, :, None] >= pad_end[:, None, :]).astype(jnp.int32), axis=2),
        N_EXPERTS - 1)
    is_e = slot_e[:, :, None] == experts[None, None, :]
    take = lambda table: jnp.sum(jnp.where(is_e, table[:, None, :], 0), axis=2)
    off = slot[None, :] - take(pad_start)
    left = take(counts) - off
    valid = left > 0
    src = jnp.clip(take(start) + off, 0, na - 1)
    src_flat = (src + (jnp.arange(n_groups, dtype=jnp.int32) * na)[:, None]).reshape(-1)
    a = a_sorted.reshape(-1)[src_flat].reshape(n_groups, n_slots)
    w = w_sorted.reshape(-1)[src_flat].reshape(n_groups, n_slots)
    tok = jnp.where(valid, a % ts, ts).astype(jnp.int32)
    wrow = jnp.where(valid, w, 0.0).astype(F32)
    first = slice(None, None, MOE_BLOCK)
    blk_e = slot_e[:, first].reshape(n_groups * nbs)
    blk_cnt = jnp.clip(left[:, first], 0, MOE_BLOCK).astype(jnp.int32).reshape(n_groups * nbs)
    return (blk_e, blk_cnt, tok.reshape(n_groups, nbs, MOE_BLOCK),
            wrow.reshape(n_groups, nbs, MOE_BLOCK))


def kernel(x, c, w_ada, b_ada, g_mix, w_in, qn_a, kn_a, qn_b, kn_b, w_pa, w_pb, w_out,
           g_ffn, w_router, w_gate, w_up, w_down):
    bsz, seq, d = x.shape
    depth = w_ada.shape[0]
    t = bsz * seq
    tm = 512
    tiles_per_seq = seq // tm
    ts = min(4096, t)

    mod = _adaln(c, w_ada, b_ada)
    tables = _rope_tables(seq)
    bd_heads = 2 * LANES // HEAD_DIM
    bd = jnp.asarray(np.kron(np.eye(bd_heads), np.ones((HEAD_DIM, HEAD_DIM))), BF16)

    g4 = B_Q_HEADS // B_KV_HEADS
    head_perm = np.concatenate(
        [np.r_[np.arange(j * HEAD_DIM, (j + 1) * HEAD_DIM),
               np.arange((j + g4) * HEAD_DIM, (j + g4 + 1) * HEAD_DIM)] for j in range(g4)])
    qb_lo = 3 * A_WIDTH
    col_perm = np.arange(w_in.shape[2])
    col_perm[qb_lo:qb_lo + B_Q_WIDTH] = qb_lo + head_perm
    scale = HEAD_DIM ** -0.5
    wr_t = w_router.T

    xf = x
    for l in range(depth):
        sh1, sc1, gt1, sh2, sc2, gt2 = [
            mod[l, :, i * d:(i + 1) * d].reshape(bsz, 1, d) for i in range(6)]
        w_in_l = w_in[l][:, col_perm].astype(BF16)
        gains = (
            jnp.tile(qn_a[l] * scale, A_HEADS).reshape(1, A_WIDTH),
            jnp.tile(kn_a[l], A_HEADS).reshape(1, A_WIDTH),
            jnp.tile(qn_b[l] * scale, B_Q_HEADS).reshape(1, B_Q_WIDTH),
            jnp.tile(kn_b[l], B_KV_HEADS).reshape(1, B_KV_WIDTH),
        )
        za, qb, kb, vbt, ga, gb = _in_proj(xf.reshape(bsz, seq, d), sh1, sc1,
                                       g_mix[l].reshape(1, d), w_in_l, bd, gains, tables, tm)
        ya = _dilated(za)
        yb = _gqa(qb, kb, vbt, 256)
        xn, hrows, e_sel, w_sel = _merge(
            xf.reshape(t, d), ya.reshape(t, A_WIDTH), yb.reshape(t, B_Q_WIDTH),
            ga.reshape(t, d), gb.reshape(t, d),
            w_pa[l].astype(BF16), w_pb[l][head_perm].astype(BF16), w_out[l].astype(BF16),
            gt1, g_ffn[l].reshape(1, d), sh2, sc2, wr_t, tm, tiles_per_seq)
        e_tok = e_sel.transpose(0, 2, 1).reshape(t, TOP_K)
        w_tok = w_sel.transpose(0, 2, 1).reshape(t, TOP_K)
        plan = _dispatch_plan(e_tok, w_tok, ts)
        yrows = _experts(hrows, plan, w_gate[l].astype(BF16), w_up[l].astype(BF16),
                         w_down[l].astype(BF16), ts)
        xf = _combine(xn, yrows, gt2, tm, tiles_per_seq)
    return xf.reshape(bsz, seq, d)
```

```python
import functools

import jax
import jax.numpy as jnp
import numpy as np
from jax import lax
from jax.experimental import pallas as pl
from jax.experimental.pallas import tpu as pltpu

D_MODEL = 1024
HEAD_DIM = 64
NORM_EPS = 1e-6
A_HEADS = 8
A_WIDTH = A_HEADS * HEAD_DIM
A_PATTERNS = ((128, 1), (512, 4), (2048, 16))
A_ROT_DIMS = HEAD_DIM // 4
A_ROPE_THETA = 500000.0
B_Q_HEADS = 8
B_KV_HEADS = 2
B_Q_WIDTH = B_Q_HEADS * HEAD_DIM
B_KV_WIDTH = B_KV_HEADS * HEAD_DIM
B_ROPE_THETA = 10000.0
GRID_W = 64
N_EXPERTS = 32
N_GROUPS = 4
EXPERTS_PER_GROUP = N_EXPERTS // N_GROUPS
TOP_K = 2
D_FF_EXPERT = D_MODEL // 2
MOE_BLOCK = 256

LANES = 128
SUBLANES = 8
ROW_CHUNKS = D_MODEL // LANES
VMEM_LIMIT = 56 * 1024 * 1024

F32 = jnp.float32
BF16 = jnp.bfloat16


def _cparams(sem):
    return pltpu.CompilerParams(dimension_semantics=sem, vmem_limit_bytes=VMEM_LIMIT)


def _adaln_kernel(c_ref, w_ref, b_ref, o_ref):
    c = c_ref[...]
    cond = c * jax.nn.sigmoid(c)
    o_ref[0] = jnp.dot(cond, w_ref[0], preferred_element_type=F32,
                       precision=lax.Precision.HIGHEST) + b_ref[0]


def _adaln(c, w_ada, b_ada):
    depth, d, n = w_ada.shape
    bsz = c.shape[0]
    tn = 1024
    return pl.pallas_call(
        _adaln_kernel,
        out_shape=jax.ShapeDtypeStruct((depth, bsz, n), F32),
        grid=(depth, n // tn),
        in_specs=[
            pl.BlockSpec((bsz, d), lambda l, j: (0, 0)),
            pl.BlockSpec((1, d, tn), lambda l, j: (l, 0, j)),
            pl.BlockSpec((1, 1, tn), lambda l, j: (l, 0, j)),
        ],
        out_specs=pl.BlockSpec((1, bsz, tn), lambda l, j: (l, 0, j)),
        compiler_params=_cparams(("arbitrary", "arbitrary")),
        name="adaln",
    )(c, w_ada, b_ada.reshape(depth, 1, n))


def _rope_tables(seq):
    pos = jnp.arange(seq, dtype=F32)
    row = jnp.floor(pos / GRID_W)
    col = pos - row * GRID_W
    d = np.arange(LANES) % HEAD_DIM

    def build(segments):
        c = jnp.ones((seq, LANES), F32)
        s1 = jnp.zeros((seq, LANES), F32)
        s2 = jnp.zeros((seq, LANES), F32)
        for lo, half, theta, p in segments:
            first = (d >= lo) & (d < lo + half)
            second = (d >= lo + half) & (d < lo + 2 * half)
            idx = np.where(first, d - lo, np.where(second, d - lo - half, 0))
            freqs = theta ** (-jnp.arange(half, dtype=F32) / half)
            ang = p[:, None] * freqs[idx][None, :]
            cs, sn = jnp.cos(ang), jnp.sin(ang)
            rot = jnp.asarray(first | second)[None, :]
            c = jnp.where(rot, cs, c)
            s1 = jnp.where(jnp.asarray(first)[None, :], -sn, s1)
            s2 = jnp.where(jnp.asarray(second)[None, :], sn, s2)
        return c, s1, s2

    ta = build([(0, A_ROT_DIMS // 2, A_ROPE_THETA, pos)])
    q = HEAD_DIM // 4
    tb = build([(0, q, B_ROPE_THETA, row), (2 * q, q, B_ROPE_THETA, col)])
    return ta + tb


def _in_proj_kernel(x_ref, sh_ref, sc_ref, g_ref, w_ref, bd_ref,
                    gqa_ref, gka_ref, gqb_ref, gkb_ref,
                    ca_ref, s1a_ref, s2a_ref, cb_ref, s1b_ref, s2b_ref,
                    za_ref, qb_ref, kb_ref, vb_ref, ga_ref, gb_ref):
    x = x_ref[0]
    ms = jnp.mean(x * x, axis=-1, keepdims=True)
    h = x * lax.rsqrt(ms + NORM_EPS) * g_ref[...]
    h = h * (1.0 + sc_ref[0]) + sh_ref[0]
    hb = h.astype(BF16)

    def seg(lo, width):
        return jnp.dot(hb, w_ref[:, lo:lo + width], preferred_element_type=F32)

    def qk_norm(z, gain_ref):
        width = z.shape[-1]
        bdw = min(width, bd_ref.shape[0])
        sq = (z * z).astype(BF16)
        parts = [jnp.dot(sq[:, lo:lo + bdw], bd_ref[:bdw, :bdw], preferred_element_type=F32)
                 for lo in range(0, width, bdw)]
        ss = parts[0] if len(parts) == 1 else jnp.concatenate(parts, axis=-1)
        return z * lax.rsqrt(ss * (1.0 / HEAD_DIM) + NORM_EPS) * gain_ref[...]

    def tile(t, width):
        reps = width // LANES
        return t if reps == 1 else jnp.concatenate([t] * reps, axis=-1)

    def rope(z, c_ref, s1_ref, s2_ref, half):
        width = z.shape[-1]
        up = pltpu.roll(z, width - half, 1)
        dn = pltpu.roll(z, half, 1)
        return (z * tile(c_ref[...], width) + up * tile(s1_ref[...], width)
                + dn * tile(s2_ref[...], width))

    ha = A_ROT_DIMS // 2
    hq = HEAD_DIM // 4
    o = 0
    qa = rope(qk_norm(seg(o, A_WIDTH), gqa_ref), ca_ref, s1a_ref, s2a_ref, ha)
    za_ref[0, :, 0:A_WIDTH] = qa
    o += A_WIDTH
    ka = rope(qk_norm(seg(o, A_WIDTH), gka_ref), ca_ref, s1a_ref, s2a_ref, ha)
    za_ref[0, :, A_WIDTH:2 * A_WIDTH] = ka
    o += A_WIDTH
    za_ref[0, :, 2 * A_WIDTH:3 * A_WIDTH] = seg(o, A_WIDTH)
    o += A_WIDTH
    qb = rope(qk_norm(seg(o, B_Q_WIDTH), gqb_ref), cb_ref, s1b_ref, s2b_ref, hq)
    qb_ref[0] = qb.astype(BF16)
    o += B_Q_WIDTH
    kb = rope(qk_norm(seg(o, B_KV_WIDTH), gkb_ref), cb_ref, s1b_ref, s2b_ref, hq)
    kb_ref[0] = kb.astype(BF16)
    o += B_KV_WIDTH
    vb_ref[0] = seg(o, B_KV_WIDTH).astype(BF16)
    o += B_KV_WIDTH
    ga_ref[0] = jax.nn.sigmoid(seg(o, D_MODEL)).astype(BF16)
    o += D_MODEL
    gb_ref[0] = jax.nn.sigmoid(seg(o, D_MODEL)).astype(BF16)


def _in_proj(x, sh, sc, g, w_in, bd, gains, tables, tm):
    bsz, seq, d = x.shape
    n_in = w_in.shape[1]
    nst = seq // tm
    row = lambda st, b: (b, st, 0)
    per_b = lambda st, b: (b, 0, 0)
    const2 = lambda st, b: (0, 0)
    tab = pl.BlockSpec((tm, LANES), lambda st, b: (st, 0))
    return pl.pallas_call(
        _in_proj_kernel,
        out_shape=(
            jax.ShapeDtypeStruct((bsz, seq, 3 * A_WIDTH), F32),
            jax.ShapeDtypeStruct((bsz, seq, B_Q_WIDTH), BF16),
            jax.ShapeDtypeStruct((bsz, seq, B_KV_WIDTH), BF16),
            jax.ShapeDtypeStruct((bsz, seq, B_KV_WIDTH), BF16),
            jax.ShapeDtypeStruct((bsz, seq, d), BF16),
            jax.ShapeDtypeStruct((bsz, seq, d), BF16),
        ),
        grid=(nst, bsz),
        in_specs=[
            pl.BlockSpec((1, tm, d), row),
            pl.BlockSpec((1, 1, d), per_b),
            pl.BlockSpec((1, 1, d), per_b),
            pl.BlockSpec((1, d), const2),
            pl.BlockSpec((d, n_in), const2),
            pl.BlockSpec(bd.shape, const2),
            pl.BlockSpec((1, A_WIDTH), const2),
            pl.BlockSpec((1, A_WIDTH), const2),
            pl.BlockSpec((1, B_Q_WIDTH), const2),
            pl.BlockSpec((1, B_KV_WIDTH), const2),
            tab, tab, tab, tab, tab, tab,
        ],
        out_specs=(
            pl.BlockSpec((1, tm, 3 * A_WIDTH), row),
            pl.BlockSpec((1, tm, B_Q_WIDTH), row),
            pl.BlockSpec((1, tm, B_KV_WIDTH), row),
            pl.BlockSpec((1, tm, B_KV_WIDTH), row),
            pl.BlockSpec((1, tm, d), row),
            pl.BlockSpec((1, tm, d), row),
        ),
        compiler_params=_cparams(("arbitrary", "arbitrary")),
        name="in_proj",
    )(x, sh, sc, g, w_in, bd, *gains, *tables)


_QB = 128
_UNROLL = 4


def _dilated_kernel(q_ref, k_ref, v_ref, o_ref, m_scr, l_scr, acc_scr, bias_scr):
    seq = q_ref.shape[1]
    lane = lax.broadcasted_iota(jnp.int32, (1, LANES), 1)
    head0 = lane < HEAD_DIM
    radius = A_PATTERNS[0][0] // (2 * A_PATTERNS[0][1])
    assert all(w // (2 * d) == radius for w, d in A_PATTERNS)

    @pl.when((pl.program_id(0) == 0) & (pl.program_id(1) == 0))
    def _():
        qrow = lax.broadcasted_iota(jnp.int32, (2 * _QB, 2 * _QB), 0) & (_QB - 1)
        kcol = lax.broadcasted_iota(jnp.int32, (2 * _QB, 2 * _QB), 1)
        for i in range(3):
            bias_scr[i] = jnp.where(jnp.abs(kcol - qrow - radius * i) <= radius, 0.0, -jnp.inf)

    def merge(t):
        return jnp.where(head0, t[:_QB], t[_QB:])

    for pi, (window, dil) in enumerate(A_PATTERNS):
        sub_len = seq // dil
        kw = min(2 * _QB, sub_len)
        nqb = sub_len // _QB
        nblk = dil * nqb

        def wide(t, kw=kw):
            return t if kw == LANES else jnp.concatenate([t] * (kw // LANES), axis=-1)

        def load(blk, pi=pi, dil=dil, sub_len=sub_len, kw=kw, nqb=nqb):
            r = blk // nqb
            i0q = (blk % nqb) * _QB
            i0k = jnp.clip(i0q - (kw - _QB) // 2, 0, sub_len - kw)
            if dil == 1:
                qrows = pl.ds(pl.multiple_of(i0q, SUBLANES), _QB)
                krows = pl.ds(pl.multiple_of(i0k, SUBLANES), kw)
            else:
                qrows = pl.ds(r + dil * i0q, _QB, stride=dil)
                krows = pl.ds(r + dil * i0k, kw, stride=dil)
            q = q_ref[0, qrows, :]
            kk = k_ref[0, krows, :].astype(BF16)
            vv = v_ref[0, krows, :].astype(BF16)
            old = None
            if pi > 0:
                old = (jnp.concatenate([m_scr[0, qrows, :], m_scr[1, qrows, :]], axis=0),
                       l_scr[qrows, :], acc_scr[qrows, :])
            return qrows, (i0q - i0k) // radius, q, kk, vv, old

        def compute(mask_id, q, kk, vv, old, kw=kw, wide=wide):
            bias = bias_scr[0, :, 0:LANES] if kw == LANES else bias_scr[mask_id]
            q2 = jnp.concatenate([jnp.where(head0, q, 0.0), jnp.where(head0, 0.0, q)],
                                 axis=0).astype(BF16)
            s = lax.dot_general(q2, kk, (((1,), (1,)), ((), ())),
                                preferred_element_type=F32) + bias
            mb = jnp.broadcast_to(jnp.max(s, axis=-1, keepdims=True), (2 * _QB, LANES))
            m_new = mb if old is None else jnp.maximum(old[0], mb)
            p = jnp.exp2(s - wide(m_new))
            psum = merge(jnp.broadcast_to(jnp.sum(p, axis=-1, keepdims=True),
                                          (2 * _QB, LANES)))
            pv = merge(jnp.dot(p.astype(BF16), vv, preferred_element_type=F32))
            if old is None:
                return m_new, psum, pv
            alpha = jnp.exp2(merge(old[0]) - merge(m_new))
            return m_new, alpha * old[1] + psum, alpha * old[2] + pv

        def group(it, carry, load=load, compute=compute):
            loaded = [load(it * _UNROLL + u) for u in range(_UNROLL)]
            results = [compute(*ld[1:]) for ld in loaded]
            for ld, (m_new, l_new, acc_new) in zip(loaded, results):
                qrows = ld[0]
                m_scr[0, qrows, :] = m_new[:_QB]
                m_scr[1, qrows, :] = m_new[_QB:]
                l_scr[qrows, :] = l_new
                acc_scr[qrows, :] = acc_new
            return carry

        lax.fori_loop(0, nblk // _UNROLL, group, 0)

    def finish(i, carry):
        rows = pl.ds(pl.multiple_of(i * 256, 256), 256)
        o_ref[0, rows, :] = (acc_scr[rows, :] / l_scr[rows, :]).astype(o_ref.dtype)
        return carry

    lax.fori_loop(0, seq // 256, finish, 0)


def _dilated(za):
    bsz, seq, _ = za.shape
    nhp = A_WIDTH // LANES
    blk = (1, seq, LANES)
    return pl.pallas_call(
        _dilated_kernel,
        out_shape=jax.ShapeDtypeStruct((bsz, seq, A_WIDTH), BF16),
        grid=(bsz, nhp),
        in_specs=[
            pl.BlockSpec(blk, lambda b, j: (b, 0, j)),
            pl.BlockSpec(blk, lambda b, j: (b, 0, nhp + j)),
            pl.BlockSpec(blk, lambda b, j: (b, 0, 2 * nhp + j)),
        ],
        out_specs=pl.BlockSpec(blk, lambda b, j: (b, 0, j)),
        scratch_shapes=[
            pltpu.VMEM((2, seq, LANES), F32),
            pltpu.VMEM((seq, LANES), F32),
            pltpu.VMEM((seq, LANES), F32),
            pltpu.VMEM((3, 2 * _QB, 2 * _QB), F32),
        ],
        compiler_params=_cparams(("arbitrary", "arbitrary")),
        name="dilated_attn",
    )(za, za, za)


def _gqa_kernel(q_ref, k_ref, v_ref, o_ref):
    tq = q_ref.shape[1]
    lane = lax.broadcasted_iota(jnp.int32, (1, LANES), 1)
    head0 = lane < HEAD_DIM
    kk = k_ref[0]
    zero = jnp.zeros((), BF16)
    v_aug = jnp.concatenate([v_ref[0], jnp.ones_like(v_ref[0])], axis=-1)
    for j in range(B_Q_WIDTH // LANES):
        qj = q_ref[0, :, j * LANES:(j + 1) * LANES]
        q2 = jnp.concatenate([jnp.where(head0, qj, zero), jnp.where(head0, zero, qj)], axis=0)
        s = lax.dot_general(q2, kk, (((1,), (1,)), ((), ())), preferred_element_type=F32)
        m = jnp.max(s, axis=-1, keepdims=True)
        pb = jnp.exp2((s - m).astype(BF16))
        oa = jnp.dot(pb, v_aug, preferred_element_type=F32)
        pv = oa[:, :LANES] / oa[:, LANES:]
        o_ref[0, :, j * LANES:(j + 1) * LANES] = jnp.where(
            head0, pv[:tq], pv[tq:]).astype(o_ref.dtype)


def _gqa(qb, kb, vb, tq):
    bsz, seq, _ = qb.shape
    kv_spec = pl.BlockSpec((1, seq, B_KV_WIDTH), lambda b, i: (b, 0, 0))
    return pl.pallas_call(
        _gqa_kernel,
        out_shape=jax.ShapeDtypeStruct((bsz, seq, B_Q_WIDTH), BF16),
        grid=(bsz, seq // tq),
        in_specs=[pl.BlockSpec((1, tq, B_Q_WIDTH), lambda b, i: (b, i, 0)), kv_spec, kv_spec],
        out_specs=pl.BlockSpec((1, tq, B_Q_WIDTH), lambda b, i: (b, i, 0)),
        compiler_params=_cparams(("arbitrary", "arbitrary")),
        name="gqa_attn",
    )(qb, kb, vb)


def _merge_kernel(x_ref, ya_ref, yb_ref, ga_ref, gb_ref, wpa_ref, wpb_ref, wo_ref,
                  gt_ref, g_ref, sh_ref, sc_ref, wr_ref,
                  xn_ref, hrow_ref, e_ref, w_ref):
    tm = x_ref.shape[0]
    pa = jnp.dot(ya_ref[...], wpa_ref[...], preferred_element_type=F32)
    pb = jnp.dot(yb_ref[...], wpb_ref[...], preferred_element_type=F32)
    merged = ga_ref[...].astype(F32) * pa + gb_ref[...].astype(F32) * pb
    out = jnp.dot(merged.astype(BF16), wo_ref[...], preferred_element_type=F32)
    xn = x_ref[...] + gt_ref[0] * out
    xn_ref[...] = xn

    ms = jnp.mean(xn * xn, axis=-1, keepdims=True)
    h = xn * lax.rsqrt(ms + NORM_EPS) * g_ref[...]
    h = h * (1.0 + sc_ref[0]) + sh_ref[0]
    for c in range(ROW_CHUNKS):
        hrow_ref[pl.ds(c, tm, stride=ROW_CHUNKS), :] = h[:, c * LANES:(c + 1) * LANES]

    logits = lax.dot_general(wr_ref[...], h, (((1,), (1,)), ((), ())),
                             preferred_element_type=F32,
                             precision=lax.Precision.HIGHEST)
    mx = jnp.max(logits, axis=0, keepdims=True)
    ex = jnp.exp(logits - mx)
    probs = ex / jnp.sum(ex, axis=0, keepdims=True)
    pg = probs.reshape(N_GROUPS, EXPERTS_PER_GROUP, tm)
    sub = lax.broadcasted_iota(jnp.int32, pg.shape, 1).astype(F32)
    m1 = jnp.max(pg, axis=1, keepdims=True)
    i1 = jnp.min(jnp.where(pg == m1, sub, float(EXPERTS_PER_GROUP)), axis=1, keepdims=True)
    pg2 = jnp.where(sub == i1, -1.0, pg)
    m2 = jnp.max(pg2, axis=1, keepdims=True)
    i2 = jnp.min(jnp.where(pg2 == m2, sub, float(EXPERTS_PER_GROUP)), axis=1, keepdims=True)
    score = m1 + m2
    gid = lax.broadcasted_iota(jnp.int32, score.shape, 0).astype(F32)
    best = jnp.max(score, axis=0, keepdims=True)
    gsel = jnp.min(jnp.where(score == best, gid, float(N_GROUPS)), axis=0, keepdims=True)
    pick = gid == gsel
    w0 = jnp.sum(jnp.where(pick, m1, 0.0), axis=0)
    w1 = jnp.sum(jnp.where(pick, m2, 0.0), axis=0)
    j0 = jnp.sum(jnp.where(pick, i1, 0.0), axis=0)
    j1 = jnp.sum(jnp.where(pick, i2, 0.0), axis=0)
    base = gsel[0] * float(EXPERTS_PER_GROUP)
    tot = w0 + w1
    e_ref[0, 0:1, :] = (base + j0).astype(jnp.int32)
    e_ref[0, 1:2, :] = (base + j1).astype(jnp.int32)
    w_ref[0, 0:1, :] = w0 / tot
    w_ref[0, 1:2, :] = w1 / tot


def _merge(x, ya, yb, ga, gb, wpa, wpb, wo, gt, g, sh, sc, wr_t, tm, tiles_per_seq):
    t, d = x.shape
    nt = t // tm
    row = lambda i: (i, 0)
    per_b = lambda i: (i // tiles_per_seq, 0, 0)
    const2 = lambda i: (0, 0)
    return pl.pallas_call(
        _merge_kernel,
        out_shape=(
            jax.ShapeDtypeStruct((t, d), F32),
            jax.ShapeDtypeStruct((t * ROW_CHUNKS, LANES), F32),
            jax.ShapeDtypeStruct((nt, TOP_K, tm), jnp.int32),
            jax.ShapeDtypeStruct((nt, TOP_K, tm), F32),
        ),
        grid=(nt,),
        in_specs=[
            pl.BlockSpec((tm, d), row),
            pl.BlockSpec((tm, A_WIDTH), row),
            pl.BlockSpec((tm, B_Q_WIDTH), row),
            pl.BlockSpec((tm, d), row),
            pl.BlockSpec((tm, d), row),
            pl.BlockSpec((A_WIDTH, d), const2),
            pl.BlockSpec((B_Q_WIDTH, d), const2),
            pl.BlockSpec((d, d), const2),
            pl.BlockSpec((1, 1, d), per_b),
            pl.BlockSpec((1, d), const2),
            pl.BlockSpec((1, 1, d), per_b),
            pl.BlockSpec((1, 1, d), per_b),
            pl.BlockSpec((N_EXPERTS, d), const2),
        ],
        out_specs=(
            pl.BlockSpec((tm, d), row),
            pl.BlockSpec((tm * ROW_CHUNKS, LANES), row),
            pl.BlockSpec((1, TOP_K, tm), lambda i: (i, 0, 0)),
            pl.BlockSpec((1, TOP_K, tm), lambda i: (i, 0, 0)),
        ),
        compiler_params=_cparams(("arbitrary",)),
        name="merge_router",
    )(x, ya, yb, ga, gb, wpa, wpb, wo, gt, g, sh, sc, wr_t)


_ROW_UNROLL = 8


def _expert_kernel(blk_e_ref, blk_cnt_ref, blk_base_ref, tok_ref, wrow_ref, h_hbm,
                   wg_ref, wu_ref, wd_ref, acc_hbm, xs, acc, gbuf, ybuf, sem):
    g = pl.program_id(0)
    j = pl.program_id(1)
    nbs = pl.num_programs(1)
    ts = xs.shape[0] // ROW_CHUNKS
    n_assign = tok_ref.shape[2]
    cnt = blk_cnt_ref[g * nbs + j]
    first = blk_base_ref[g * nbs + j]

    def tile(idx):
        return pl.ds(pl.multiple_of(idx * ROW_CHUNKS, ROW_CHUNKS), ROW_CHUNKS)

    def group_rows():
        return pl.ds(pl.multiple_of(g * (ts * ROW_CHUNKS), ROW_CHUNKS), ts * ROW_CHUNKS)

    @pl.when(j == 0)
    def _():
        load = pltpu.make_async_copy(h_hbm.at[group_rows(), :], xs, sem.at[0])
        load.start()
        acc[...] = jnp.zeros_like(acc)

        @pl.when(g == 0)
        def _():
            gbuf[...] = jnp.zeros_like(gbuf)
        load.wait()

    @pl.when(cnt > 0)
    def _():
        nchunk = (cnt + _ROW_UNROLL - 1) // _ROW_UNROLL

        def gather(c, carry):
            base = c * _ROW_UNROLL
            for u in range(_ROW_UNROLL):
                src = tok_ref[0, 0, jnp.minimum(first + base + u, n_assign - 1)]
                gbuf[tile(base + u), :] = xs[tile(src), :]
            return carry
        lax.fori_loop(0, nchunk, gather, 0)

        xb = jnp.concatenate(
            [gbuf[pl.ds(c, MOE_BLOCK, stride=ROW_CHUNKS), :] for c in range(ROW_CHUNKS)],
            axis=-1).astype(BF16)
        gate = jnp.dot(xb, wg_ref[0], preferred_element_type=F32)
        up = jnp.dot(xb, wu_ref[0], preferred_element_type=F32)
        hid = (gate * jax.nn.sigmoid(gate) * up).astype(BF16)
        y = jnp.dot(hid, wd_ref[0], preferred_element_type=F32)
        for c in range(ROW_CHUNKS):
            ybuf[pl.ds(c, MOE_BLOCK, stride=ROW_CHUNKS), :] = y[:, c * LANES:(c + 1) * LANES]

        def scatter_add(c, carry):
            base = c * _ROW_UNROLL
            new = []
            for u in range(_ROW_UNROLL):
                real = base + u < cnt
                entry = jnp.minimum(first + base + u, n_assign - 1)
                dst = tile(jnp.where(real, tok_ref[0, 0, entry], ts))
                wgt = jnp.where(real, wrow_ref[0, 0, entry], 0.0)
                new.append((dst, acc[dst, :] + wgt * ybuf[tile(base + u), :]))
            for dst, val in new:
                acc[dst, :] = val
            return carry
        lax.fori_loop(0, nchunk, scatter_add, 0)

    @pl.when(j == nbs - 1)
    def _():
        store = pltpu.make_async_copy(acc.at[pl.ds(0, ts * ROW_CHUNKS), :],
                                      acc_hbm.at[group_rows(), :], sem.at[1])
        store.start()
        store.wait()


def _experts(hrows, plan, wg, wu, wd, ts):
    blk_e, blk_cnt, blk_first, tok, wrow = plan
    n_groups, na = tok.shape[0], tok.shape[2]
    nbs = blk_e.shape[0] // n_groups
    d = D_MODEL
    idx_blk = (1, 1, na)
    idx_map = lambda g, j, be, bc, bf: (g, 0, 0)
    w_map = lambda g, j, be, bc, bf: (be[g * nbs + j], 0, 0)
    grid_spec = pltpu.PrefetchScalarGridSpec(
        num_scalar_prefetch=3,
        grid=(n_groups, nbs),
        in_specs=[
            pl.BlockSpec(idx_blk, idx_map, memory_space=pltpu.SMEM),
            pl.BlockSpec(idx_blk, idx_map, memory_space=pltpu.SMEM),
            pl.BlockSpec(memory_space=pl.ANY),
            pl.BlockSpec((1, d, D_FF_EXPERT), w_map),
            pl.BlockSpec((1, d, D_FF_EXPERT), w_map),
            pl.BlockSpec((1, D_FF_EXPERT, d), w_map),
        ],
        out_specs=pl.BlockSpec(memory_space=pl.ANY),
        scratch_shapes=[
            pltpu.VMEM((ts * ROW_CHUNKS, LANES), F32),
            pltpu.VMEM(((ts + 1) * ROW_CHUNKS, LANES), F32),
            pltpu.VMEM((MOE_BLOCK * ROW_CHUNKS, LANES), F32),
            pltpu.VMEM((MOE_BLOCK * ROW_CHUNKS, LANES), F32),
            pltpu.SemaphoreType.DMA((2,)),
        ],
    )
    return pl.pallas_call(
        _expert_kernel,
        out_shape=jax.ShapeDtypeStruct(hrows.shape, F32),
        grid_spec=grid_spec,
        compiler_params=_cparams(("arbitrary", "arbitrary")),
        name="experts",
    )(blk_e, blk_cnt, blk_first, tok, wrow, hrows, wg, wu, wd)


def _combine_kernel(x_ref, y_ref, gt_ref, o_ref):
    tm = x_ref.shape[0]
    gt = gt_ref[0]
    for c in range(ROW_CHUNKS):
        cols = slice(c * LANES, (c + 1) * LANES)
        o_ref[:, cols] = x_ref[:, cols] + gt[:, cols] * y_ref[pl.ds(c, tm, stride=ROW_CHUNKS), :]


def _combine(xn, yrows, gt, tm, tiles_per_seq):
    t, d = xn.shape
    nt = t // tm
    return pl.pallas_call(
        _combine_kernel,
        out_shape=jax.ShapeDtypeStruct((t, d), F32),
        grid=(nt,),
        in_specs=[
            pl.BlockSpec((tm, d), lambda i: (i, 0)),
            pl.BlockSpec((tm * ROW_CHUNKS, LANES), lambda i: (i, 0)),
            pl.BlockSpec((1, 1, d), lambda i: (i // tiles_per_seq, 0, 0)),
        ],
        out_specs=pl.BlockSpec((tm, d), lambda i: (i, 0)),
        compiler_params=_cparams(("arbitrary",)),
        name="moe_combine",
    )(xn, yrows, gt)


def _dispatch_plan(e_sel, w_sel, ts):
    t = e_sel.shape[0]
    n_groups = t // ts
    na = ts * TOP_K
    nbs = na // MOE_BLOCK + N_EXPERTS
    n_slots = nbs * MOE_BLOCK
    e_flat = e_sel.reshape(n_groups, ts, TOP_K).transpose(0, 2, 1).reshape(n_groups, na)
    w_flat = w_sel.reshape(n_groups, ts, TOP_K).transpose(0, 2, 1).reshape(n_groups, na)
    a_ids = jnp.broadcast_to(jnp.arange(na, dtype=jnp.int32)[None, :], (n_groups, na))
    _, a_sorted, w_sorted = lax.sort((e_flat, a_ids, w_flat), dimension=1, num_keys=1)
    experts = jnp.arange(N_EXPERTS, dtype=jnp.int32)
    counts = jnp.sum((e_flat[:, :, None] == experts[None, None, :]).astype(jnp.int32), axis=1)
    padded = (counts + MOE_BLOCK - 1) // MOE_BLOCK * MOE_BLOCK
    pad_end = jnp.cumsum(padded, axis=1)
    pad_start = pad_end - padded
    start = jnp.cumsum(counts, axis=1) - counts
    row0 = jnp.arange(nbs, dtype=jnp.int32) * MOE_BLOCK
    blk_e = jnp.minimum(
        jnp.sum((row0[None, :, None] >= pad_end[:, None, :]).astype(jnp.int32), axis=2),
        N_EXPERTS - 1)
    is_e = blk_e[:, :, None] == experts[None, None, :]
    take = lambda table: jnp.sum(jnp.where(is_e, table[:, None, :], 0), axis=2)
    off = row0[None, :] - take(pad_start)
    blk_cnt = jnp.clip(take(counts) - off, 0, MOE_BLOCK)
    blk_first = jnp.clip(take(start) + off, 0, na - 1)
    flat = lambda v: v.astype(jnp.int32).reshape(n_groups * nbs)
    return (flat(blk_e), flat(blk_cnt), flat(blk_first),
            (a_sorted % ts).astype(jnp.int32).reshape(n_groups, 1, na),
            w_sorted.astype(F32).reshape(n_groups, 1, na))


def kernel(x, c, w_ada, b_ada, g_mix, w_in, qn_a, kn_a, qn_b, kn_b, w_pa, w_pb, w_out,
           g_ffn, w_router, w_gate, w_up, w_down):
    bsz, seq, d = x.shape
    depth = w_ada.shape[0]
    t = bsz * seq
    tm = 512
    tiles_per_seq = seq // tm
    ts = min(4096, t)

    mod = _adaln(c, w_ada, b_ada)
    tables = _rope_tables(seq)
    bd_heads = 2 * LANES // HEAD_DIM
    bd = jnp.asarray(np.kron(np.eye(bd_heads), np.ones((HEAD_DIM, HEAD_DIM))), BF16)

    g4 = B_Q_HEADS // B_KV_HEADS
    head_perm = np.concatenate(
        [np.r_[np.arange(j * HEAD_DIM, (j + 1) * HEAD_DIM),
               np.arange((j + g4) * HEAD_DIM, (j + g4 + 1) * HEAD_DIM)] for j in range(g4)])
    qb_lo = 3 * A_WIDTH
    col_perm = np.arange(w_in.shape[2])
    col_perm[qb_lo:qb_lo + B_Q_WIDTH] = qb_lo + head_perm
    scale = HEAD_DIM ** -0.5 * float(np.log2(np.e))
    wr_t = w_router.T

    xf = x
    for l in range(depth):
        sh1, sc1, gt1, sh2, sc2, gt2 = [
            mod[l, :, i * d:(i + 1) * d].reshape(bsz, 1, d) for i in range(6)]
        w_in_l = w_in[l][:, col_perm].astype(BF16)
        gains = (
            jnp.tile(qn_a[l] * scale, A_HEADS).reshape(1, A_WIDTH),
            jnp.tile(kn_a[l], A_HEADS).reshape(1, A_WIDTH),
            jnp.tile(qn_b[l] * scale, B_Q_HEADS).reshape(1, B_Q_WIDTH),
            jnp.tile(kn_b[l], B_KV_HEADS).reshape(1, B_KV_WIDTH),
        )
        za, qb, kb, vb, ga, gb = _in_proj(xf.reshape(bsz, seq, d), sh1, sc1,
                                       g_mix[l].reshape(1, d), w_in_l, bd, gains, tables, tm)
        ya = _dilated(za)
        yb = _gqa(qb, kb, vb, 256)
        xn, hrows, e_sel, w_sel = _merge(
            xf.reshape(t, d), ya.reshape(t, A_WIDTH), yb.reshape(t, B_Q_WIDTH),
            ga.reshape(t, d), gb.reshape(t, d),
            w_pa[l].astype(BF16), w_pb[l][head_perm].astype(BF16), w_out[l].astype(BF16),
            gt1, g_ffn[l].reshape(1, d), sh2, sc2, wr_t, tm, tiles_per_seq)
        e_tok = e_sel.transpose(0, 2, 1).reshape(t, TOP_K)
        w_tok = w_sel.transpose(0, 2, 1).reshape(t, TOP_K)
        plan = _dispatch_plan(e_tok, w_tok, ts)
        yrows = _experts(hrows, plan, w_gate[l].astype(BF16), w_up[l].astype(BF16),
                         w_down[l].astype(BF16), ts)
        xf = _combine(xn, yrows, gt2, tm, tiles_per_seq)
    return xf.reshape(bsz, seq, d)
```

```python
import functools

import jax
import jax.numpy as jnp
import numpy as np
from jax import lax
from jax.experimental import pallas as pl
from jax.experimental.pallas import tpu as pltpu

D_MODEL = 1024
HEAD_DIM = 64
NORM_EPS = 1e-6
A_HEADS = 8
A_WIDTH = A_HEADS * HEAD_DIM
A_PATTERNS = ((128, 1), (512, 4), (2048, 16))
A_ROT_DIMS = HEAD_DIM // 4
A_ROPE_THETA = 500000.0
B_Q_HEADS = 8
B_KV_HEADS = 2
B_Q_WIDTH = B_Q_HEADS * HEAD_DIM
B_KV_WIDTH = B_KV_HEADS * HEAD_DIM
B_ROPE_THETA = 10000.0
GRID_W = 64
N_EXPERTS = 32
N_GROUPS = 4
EXPERTS_PER_GROUP = N_EXPERTS // N_GROUPS
TOP_K = 2
D_FF_EXPERT = D_MODEL // 2
MOE_BLOCK = 256

LANES = 128
SUBLANES = 8
ROW_CHUNKS = D_MODEL // LANES
VMEM_LIMIT = 56 * 1024 * 1024

F32 = jnp.float32
BF16 = jnp.bfloat16


def _cparams(sem):
    return pltpu.CompilerParams(dimension_semantics=sem, vmem_limit_bytes=VMEM_LIMIT)


def _adaln_kernel(c_ref, w_ref, b_ref, o_ref):
    c = c_ref[...]
    cond = c * jax.nn.sigmoid(c)
    o_ref[0] = jnp.dot(cond, w_ref[0], preferred_element_type=F32,
                       precision=lax.Precision.HIGHEST) + b_ref[0]


def _adaln(c, w_ada, b_ada):
    depth, d, n = w_ada.shape
    bsz = c.shape[0]
    tn = 1024
    return pl.pallas_call(
        _adaln_kernel,
        out_shape=jax.ShapeDtypeStruct((depth, bsz, n), F32),
        grid=(depth, n // tn),
        in_specs=[
            pl.BlockSpec((bsz, d), lambda l, j: (0, 0)),
            pl.BlockSpec((1, d, tn), lambda l, j: (l, 0, j)),
            pl.BlockSpec((1, 1, tn), lambda l, j: (l, 0, j)),
        ],
        out_specs=pl.BlockSpec((1, bsz, tn), lambda l, j: (l, 0, j)),
        compiler_params=_cparams(("arbitrary", "arbitrary")),
        name="adaln",
    )(c, w_ada, b_ada.reshape(depth, 1, n))


def _rope_tables(seq):
    pos = jnp.arange(seq, dtype=F32)
    row = jnp.floor(pos / GRID_W)
    col = pos - row * GRID_W
    d = np.arange(LANES) % HEAD_DIM

    def build(segments):
        c = jnp.ones((seq, LANES), F32)
        s1 = jnp.zeros((seq, LANES), F32)
        s2 = jnp.zeros((seq, LANES), F32)
        for lo, half, theta, p in segments:
            first = (d >= lo) & (d < lo + half)
            second = (d >= lo + half) & (d < lo + 2 * half)
            idx = np.where(first, d - lo, np.where(second, d - lo - half, 0))
            freqs = theta ** (-jnp.arange(half, dtype=F32) / half)
            ang = p[:, None] * freqs[idx][None, :]
            cs, sn = jnp.cos(ang), jnp.sin(ang)
            rot = jnp.asarray(first | second)[None, :]
            c = jnp.where(rot, cs, c)
            s1 = jnp.where(jnp.asarray(first)[None, :], -sn, s1)
            s2 = jnp.where(jnp.asarray(second)[None, :], sn, s2)
        return c, s1, s2

    ta = build([(0, A_ROT_DIMS // 2, A_ROPE_THETA, pos)])
    q = HEAD_DIM // 4
    tb = build([(0, q, B_ROPE_THETA, row), (2 * q, q, B_ROPE_THETA, col)])
    return ta + tb


def _in_proj_kernel(x_ref, sh_ref, sc_ref, g_ref, w_ref, bd_ref,
                    gqa_ref, gka_ref, gqb_ref, gkb_ref,
                    ca_ref, s1a_ref, s2a_ref, cb_ref, s1b_ref, s2b_ref,
                    za_ref, qb_ref, kb_ref, vb_ref, ga_ref, gb_ref):
    x = x_ref[0]
    ms = jnp.mean(x * x, axis=-1, keepdims=True)
    h = x * lax.rsqrt(ms + NORM_EPS) * g_ref[...]
    h = h * (1.0 + sc_ref[0]) + sh_ref[0]
    hb = h.astype(BF16)

    def seg(lo, width):
        return jnp.dot(hb, w_ref[:, lo:lo + width], preferred_element_type=F32)

    def qk_norm(z, gain_ref):
        width = z.shape[-1]
        bdw = min(width, bd_ref.shape[0])
        sq = (z * z).astype(BF16)
        parts = [jnp.dot(sq[:, lo:lo + bdw], bd_ref[:bdw, :bdw], preferred_element_type=F32)
                 for lo in range(0, width, bdw)]
        ss = parts[0] if len(parts) == 1 else jnp.concatenate(parts, axis=-1)
        return z * lax.rsqrt(ss * (1.0 / HEAD_DIM) + NORM_EPS) * gain_ref[...]

    def tile(t, width):
        reps = width // LANES
        return t if reps == 1 else jnp.concatenate([t] * reps, axis=-1)

    def rope(z, c_ref, s1_ref, s2_ref, half):
        width = z.shape[-1]
        up = pltpu.roll(z, width - half, 1)
        dn = pltpu.roll(z, half, 1)
        return (z * tile(c_ref[...], width) + up * tile(s1_ref[...], width)
                + dn * tile(s2_ref[...], width))

    ha = A_ROT_DIMS // 2
    hq = HEAD_DIM // 4
    o = 0
    qa = rope(qk_norm(seg(o, A_WIDTH), gqa_ref), ca_ref, s1a_ref, s2a_ref, ha)
    za_ref[0, :, 0:A_WIDTH] = qa
    o += A_WIDTH
    ka = rope(qk_norm(seg(o, A_WIDTH), gka_ref), ca_ref, s1a_ref, s2a_ref, ha)
    za_ref[0, :, A_WIDTH:2 * A_WIDTH] = ka
    o += A_WIDTH
    za_ref[0, :, 2 * A_WIDTH:3 * A_WIDTH] = seg(o, A_WIDTH)
    o += A_WIDTH
    qb = rope(qk_norm(seg(o, B_Q_WIDTH), gqb_ref), cb_ref, s1b_ref, s2b_ref, hq)
    qb_ref[0] = qb.astype(BF16)
    o += B_Q_WIDTH
    kb = rope(qk_norm(seg(o, B_KV_WIDTH), gkb_ref), cb_ref, s1b_ref, s2b_ref, hq)
    kb_ref[0] = kb.astype(BF16)
    o += B_KV_WIDTH
    vb_ref[0] = seg(o, B_KV_WIDTH).astype(BF16)
    o += B_KV_WIDTH
    ga_ref[0] = jax.nn.sigmoid(seg(o, D_MODEL)).astype(BF16)
    o += D_MODEL
    gb_ref[0] = jax.nn.sigmoid(seg(o, D_MODEL)).astype(BF16)


def _in_proj(x, sh, sc, g, w_in, bd, gains, tables, tm):
    bsz, seq, d = x.shape
    n_in = w_in.shape[1]
    nst = seq // tm
    row = lambda st, b: (b, st, 0)
    per_b = lambda st, b: (b, 0, 0)
    const2 = lambda st, b: (0, 0)
    tab = pl.BlockSpec((tm, LANES), lambda st, b: (st, 0))
    return pl.pallas_call(
        _in_proj_kernel,
        out_shape=(
            jax.ShapeDtypeStruct((bsz, seq, 3 * A_WIDTH), F32),
            jax.ShapeDtypeStruct((bsz, seq, B_Q_WIDTH), BF16),
            jax.ShapeDtypeStruct((bsz, seq, B_KV_WIDTH), BF16),
            jax.ShapeDtypeStruct((bsz, seq, B_KV_WIDTH), BF16),
            jax.ShapeDtypeStruct((bsz, seq, d), BF16),
            jax.ShapeDtypeStruct((bsz, seq, d), BF16),
        ),
        grid=(nst, bsz),
        in_specs=[
            pl.BlockSpec((1, tm, d), row),
            pl.BlockSpec((1, 1, d), per_b),
            pl.BlockSpec((1, 1, d), per_b),
            pl.BlockSpec((1, d), const2),
            pl.BlockSpec((d, n_in), const2),
            pl.BlockSpec(bd.shape, const2),
            pl.BlockSpec((1, A_WIDTH), const2),
            pl.BlockSpec((1, A_WIDTH), const2),
            pl.BlockSpec((1, B_Q_WIDTH), const2),
            pl.BlockSpec((1, B_KV_WIDTH), const2),
            tab, tab, tab, tab, tab, tab,
        ],
        out_specs=(
            pl.BlockSpec((1, tm, 3 * A_WIDTH), row),
            pl.BlockSpec((1, tm, B_Q_WIDTH), row),
            pl.BlockSpec((1, tm, B_KV_WIDTH), row),
            pl.BlockSpec((1, tm, B_KV_WIDTH), row),
            pl.BlockSpec((1, tm, d), row),
            pl.BlockSpec((1, tm, d), row),
        ),
        compiler_params=_cparams(("arbitrary", "arbitrary")),
        name="in_proj",
    )(x, sh, sc, g, w_in, bd, *gains, *tables)


_QB = 128
_UNROLL = 8


def _dilated_kernel(q_ref, k_ref, v_ref, o_ref, m_scr, l_scr, acc_scr, bias_scr):
    seq = q_ref.shape[1]
    lane = lax.broadcasted_iota(jnp.int32, (1, LANES), 1)
    head0 = lane < HEAD_DIM
    radius = A_PATTERNS[0][0] // (2 * A_PATTERNS[0][1])
    assert all(w // (2 * d) == radius for w, d in A_PATTERNS)

    @pl.when((pl.program_id(0) == 0) & (pl.program_id(1) == 0))
    def _():
        qrow = lax.broadcasted_iota(jnp.int32, (2 * _QB, 2 * _QB), 0) & (_QB - 1)
        kcol = lax.broadcasted_iota(jnp.int32, (2 * _QB, 2 * _QB), 1)
        for i in range(3):
            bias_scr[i] = jnp.where(jnp.abs(kcol - qrow - radius * i) <= radius, 0.0, -jnp.inf)

    def merge(t):
        return jnp.where(head0, t[:_QB], t[_QB:])

    for pi, (window, dil) in enumerate(A_PATTERNS):
        sub_len = seq // dil
        kw = min(2 * _QB, sub_len)
        nqb = sub_len // _QB
        nblk = dil * nqb

        def wide(t, kw=kw):
            return t if kw == LANES else jnp.concatenate([t] * (kw // LANES), axis=-1)

        def load(blk, pi=pi, dil=dil, sub_len=sub_len, kw=kw, nqb=nqb):
            r = blk // nqb
            i0q = (blk % nqb) * _QB
            i0k = jnp.clip(i0q - (kw - _QB) // 2, 0, sub_len - kw)
            if dil == 1:
                qrows = pl.ds(pl.multiple_of(i0q, SUBLANES), _QB)
                krows = pl.ds(pl.multiple_of(i0k, SUBLANES), kw)
            else:
                qrows = pl.ds(r + dil * i0q, _QB, stride=dil)
                krows = pl.ds(r + dil * i0k, kw, stride=dil)
            q = q_ref[0, qrows, :]
            kk = k_ref[0, krows, :].astype(BF16)
            vv = v_ref[0, krows, :].astype(BF16)
            old = None
            if pi > 0:
                old = (jnp.concatenate([m_scr[0, qrows, :], m_scr[1, qrows, :]], axis=0),
                       l_scr[qrows, :], acc_scr[qrows, :])
            return qrows, (i0q - i0k) // radius, q, kk, vv, old

        def compute(mask_id, q, kk, vv, old, kw=kw, wide=wide):
            bias = bias_scr[0, :, 0:LANES] if kw == LANES else bias_scr[mask_id]
            q2 = jnp.concatenate([jnp.where(head0, q, 0.0), jnp.where(head0, 0.0, q)],
                                 axis=0).astype(BF16)
            s = lax.dot_general(q2, kk, (((1,), (1,)), ((), ())),
                                preferred_element_type=F32) + bias
            mb = jnp.broadcast_to(jnp.max(s, axis=-1, keepdims=True), (2 * _QB, LANES))
            m_new = mb if old is None else jnp.maximum(old[0], mb)
            p = jnp.exp2(s - wide(m_new))
            psum = merge(jnp.broadcast_to(jnp.sum(p, axis=-1, keepdims=True),
                                          (2 * _QB, LANES)))
            pv = merge(jnp.dot(p.astype(BF16), vv, preferred_element_type=F32))
            if old is None:
                return m_new, psum, pv
            alpha = jnp.exp2(merge(old[0]) - merge(m_new))
            return m_new, alpha * old[1] + psum, alpha * old[2] + pv

        def group(it, carry, load=load, compute=compute):
            loaded = [load(it * _UNROLL + u) for u in range(_UNROLL)]
            results = [compute(*ld[1:]) for ld in loaded]
            for ld, (m_new, l_new, acc_new) in zip(loaded, results):
                qrows = ld[0]
                m_scr[0, qrows, :] = m_new[:_QB]
                m_scr[1, qrows, :] = m_new[_QB:]
                l_scr[qrows, :] = l_new
                acc_scr[qrows, :] = acc_new
            return carry

        lax.fori_loop(0, nblk // _UNROLL, group, 0)

    def finish(i, carry):
        rows = pl.ds(pl.multiple_of(i * 256, 256), 256)
        o_ref[0, rows, :] = (acc_scr[rows, :] / l_scr[rows, :]).astype(o_ref.dtype)
        return carry

    lax.fori_loop(0, seq // 256, finish, 0)


def _dilated(za):
    bsz, seq, _ = za.shape
    nhp = A_WIDTH // LANES
    blk = (1, seq, LANES)
    return pl.pallas_call(
        _dilated_kernel,
        out_shape=jax.ShapeDtypeStruct((bsz, seq, A_WIDTH), BF16),
        grid=(bsz, nhp),
        in_specs=[
            pl.BlockSpec(blk, lambda b, j: (b, 0, j)),
            pl.BlockSpec(blk, lambda b, j: (b, 0, nhp + j)),
            pl.BlockSpec(blk, lambda b, j: (b, 0, 2 * nhp + j)),
        ],
        out_specs=pl.BlockSpec(blk, lambda b, j: (b, 0, j)),
        scratch_shapes=[
            pltpu.VMEM((2, seq, LANES), F32),
            pltpu.VMEM((seq, LANES), F32),
            pltpu.VMEM((seq, LANES), F32),
            pltpu.VMEM((3, 2 * _QB, 2 * _QB), F32),
        ],
        compiler_params=_cparams(("arbitrary", "arbitrary")),
        name="dilated_attn",
    )(za, za, za)


def _gqa_kernel(q_ref, k_ref, v_ref, o_ref):
    tq = q_ref.shape[1]
    lane = lax.broadcasted_iota(jnp.int32, (1, LANES), 1)
    head0 = lane < HEAD_DIM
    kk = k_ref[0]
    zero = jnp.zeros((), BF16)
    v_aug = jnp.concatenate([v_ref[0], jnp.ones_like(v_ref[0])], axis=-1)
    for j in range(B_Q_WIDTH // LANES):
        qj = q_ref[0, :, j * LANES:(j + 1) * LANES]
        q2 = jnp.concatenate([jnp.where(head0, qj, zero), jnp.where(head0, zero, qj)], axis=0)
        s = lax.dot_general(q2, kk, (((1,), (1,)), ((), ())), preferred_element_type=F32)
        m = jnp.max(s, axis=-1, keepdims=True)
        pb = jnp.exp2((s - m).astype(BF16))
        oa = jnp.dot(pb, v_aug, preferred_element_type=F32)
        pv = oa[:, :LANES] / oa[:, LANES:]
        o_ref[0, :, j * LANES:(j + 1) * LANES] = jnp.where(
            head0, pv[:tq], pv[tq:]).astype(o_ref.dtype)


def _gqa(qb, kb, vb, tq):
    bsz, seq, _ = qb.shape
    kv_spec = pl.BlockSpec((1, seq, B_KV_WIDTH), lambda b, i: (b, 0, 0))
    return pl.pallas_call(
        _gqa_kernel,
        out_shape=jax.ShapeDtypeStruct((bsz, seq, B_Q_WIDTH), BF16),
        grid=(bsz, seq // tq),
        in_specs=[pl.BlockSpec((1, tq, B_Q_WIDTH), lambda b, i: (b, i, 0)), kv_spec, kv_spec],
        out_specs=pl.BlockSpec((1, tq, B_Q_WIDTH), lambda b, i: (b, i, 0)),
        compiler_params=_cparams(("arbitrary", "arbitrary")),
        name="gqa_attn",
    )(qb, kb, vb)


def _merge_kernel(x_ref, ya_ref, yb_ref, ga_ref, gb_ref, wpa_ref, wpb_ref, wo_ref,
                  gt_ref, g_ref, sh_ref, sc_ref, wr_ref,
                  xn_ref, hrow_ref, e_ref, w_ref):
    tm = x_ref.shape[0]
    pa = jnp.dot(ya_ref[...], wpa_ref[...], preferred_element_type=F32)
    pb = jnp.dot(yb_ref[...], wpb_ref[...], preferred_element_type=F32)
    merged = ga_ref[...].astype(F32) * pa + gb_ref[...].astype(F32) * pb
    out = jnp.dot(merged.astype(BF16), wo_ref[...], preferred_element_type=F32)
    xn = x_ref[...] + gt_ref[0] * out
    xn_ref[...] = xn

    ms = jnp.mean(xn * xn, axis=-1, keepdims=True)
    h = xn * lax.rsqrt(ms + NORM_EPS) * g_ref[...]
    h = h * (1.0 + sc_ref[0]) + sh_ref[0]
    for c in range(ROW_CHUNKS):
        hrow_ref[pl.ds(c, tm, stride=ROW_CHUNKS), :] = h[:, c * LANES:(c + 1) * LANES]

    logits = lax.dot_general(wr_ref[...], h, (((1,), (1,)), ((), ())),
                             preferred_element_type=F32,
                             precision=lax.Precision.HIGHEST)
    mx = jnp.max(logits, axis=0, keepdims=True)
    ex = jnp.exp(logits - mx)
    probs = ex / jnp.sum(ex, axis=0, keepdims=True)
    pg = probs.reshape(N_GROUPS, EXPERTS_PER_GROUP, tm)
    sub = lax.broadcasted_iota(jnp.int32, pg.shape, 1).astype(F32)
    m1 = jnp.max(pg, axis=1, keepdims=True)
    i1 = jnp.min(jnp.where(pg == m1, sub, float(EXPERTS_PER_GROUP)), axis=1, keepdims=True)
    pg2 = jnp.where(sub == i1, -1.0, pg)
    m2 = jnp.max(pg2, axis=1, keepdims=True)
    i2 = jnp.min(jnp.where(pg2 == m2, sub, float(EXPERTS_PER_GROUP)), axis=1, keepdims=True)
    score = m1 + m2
    gid = lax.broadcasted_iota(jnp.int32, score.shape, 0).astype(F32)
    best = jnp.max(score, axis=0, keepdims=True)
    gsel = jnp.min(jnp.where(score == best, gid, float(N_GROUPS)), axis=0, keepdims=True)
    pick = gid == gsel
    w0 = jnp.sum(jnp.where(pick, m1, 0.0), axis=0)
    w1 = jnp.sum(jnp.where(pick, m2, 0.0), axis=0)
    j0 = jnp.sum(jnp.where(pick, i1, 0.0), axis=0)
    j1 = jnp.sum(jnp.where(pick, i2, 0.0), axis=0)
    base = gsel[0] * float(EXPERTS_PER_GROUP)
    tot = w0 + w1
    e_ref[0, 0:1, :] = (base + j0).astype(jnp.int32)
    e_ref[0, 1:2, :] = (base + j1).astype(jnp.int32)
    w_ref[0, 0:1, :] = w0 / tot
    w_ref[0, 1:2, :] = w1 / tot


def _merge(x, ya, yb, ga, gb, wpa, wpb, wo, gt, g, sh, sc, wr_t, tm, tiles_per_seq):
    t, d = x.shape
    nt = t // tm
    row = lambda i: (i, 0)
    per_b = lambda i: (i // tiles_per_seq, 0, 0)
    const2 = lambda i: (0, 0)
    return pl.pallas_call(
        _merge_kernel,
        out_shape=(
            jax.ShapeDtypeStruct((t, d), F32),
            jax.ShapeDtypeStruct((t * ROW_CHUNKS, LANES), F32),
            jax.ShapeDtypeStruct((nt, TOP_K, tm), jnp.int32),
            jax.ShapeDtypeStruct((nt, TOP_K, tm), F32),
        ),
        grid=(nt,),
        in_specs=[
            pl.BlockSpec((tm, d), row),
            pl.BlockSpec((tm, A_WIDTH), row),
            pl.BlockSpec((tm, B_Q_WIDTH), row),
            pl.BlockSpec((tm, d), row),
            pl.BlockSpec((tm, d), row),
            pl.BlockSpec((A_WIDTH, d), const2),
            pl.BlockSpec((B_Q_WIDTH, d), const2),
            pl.BlockSpec((d, d), const2),
            pl.BlockSpec((1, 1, d), per_b),
            pl.BlockSpec((1, d), const2),
            pl.BlockSpec((1, 1, d), per_b),
            pl.BlockSpec((1, 1, d), per_b),
            pl.BlockSpec((N_EXPERTS, d), const2),
        ],
        out_specs=(
            pl.BlockSpec((tm, d), row),
            pl.BlockSpec((tm * ROW_CHUNKS, LANES), row),
            pl.BlockSpec((1, TOP_K, tm), lambda i: (i, 0, 0)),
            pl.BlockSpec((1, TOP_K, tm), lambda i: (i, 0, 0)),
        ),
        compiler_params=_cparams(("arbitrary",)),
        name="merge_router",
    )(x, ya, yb, ga, gb, wpa, wpb, wo, gt, g, sh, sc, wr_t)


_ROW_UNROLL = 8


def _expert_kernel(blk_e_ref, blk_cnt_ref, blk_base_ref, tok_ref, wrow_ref, h_hbm,
                   wg_ref, wu_ref, wd_ref, acc_hbm, xs, acc, gbuf, ybuf, sem):
    g = pl.program_id(0)
    j = pl.program_id(1)
    n_groups = pl.num_programs(0)
    nbs = pl.num_programs(1)
    group_len = xs.shape[0]
    cnt = blk_cnt_ref[g * nbs + j]
    first = blk_base_ref[g * nbs + j]

    def tile_at(row):
        return pl.ds(pl.multiple_of(row, ROW_CHUNKS), ROW_CHUNKS)

    def group_rows(gi):
        return pl.ds(pl.multiple_of(gi * group_len, ROW_CHUNKS), group_len)

    def store_copy(gi):
        return pltpu.make_async_copy(acc.at[pl.ds(0, group_len), :],
                                     acc_hbm.at[group_rows(gi), :], sem.at[1])

    @pl.when(j == 0)
    def _():
        load = pltpu.make_async_copy(h_hbm.at[group_rows(g), :], xs, sem.at[0])
        load.start()

        @pl.when(g > 0)
        def _():
            store_copy(g - 1).wait()
        acc[...] = jnp.zeros_like(acc)

        @pl.when(g == 0)
        def _():
            gbuf[...] = jnp.zeros_like(gbuf)
        load.wait()

    @pl.when(cnt > 0)
    def _():
        nfull = cnt // _ROW_UNROLL
        tail = cnt - nfull * _ROW_UNROLL

        def gather(c, carry):
            base = c * _ROW_UNROLL
            for u in range(_ROW_UNROLL):
                gbuf[tile_at((base + u) * ROW_CHUNKS), :] = xs[
                    tile_at(tok_ref[0, 0, first + base + u]), :]
            return carry
        lax.fori_loop(0, (cnt + _ROW_UNROLL - 1) // _ROW_UNROLL, gather, 0)

        xb = jnp.concatenate(
            [gbuf[pl.ds(c, MOE_BLOCK, stride=ROW_CHUNKS), :] for c in range(ROW_CHUNKS)],
            axis=-1).astype(BF16)
        gate = jnp.dot(xb, wg_ref[0], preferred_element_type=F32)
        up = jnp.dot(xb, wu_ref[0], preferred_element_type=F32)
        hid = (gate * jax.nn.sigmoid(gate) * up).astype(BF16)
        y = jnp.dot(hid, wd_ref[0], preferred_element_type=F32)
        for c in range(ROW_CHUNKS):
            ybuf[pl.ds(c, MOE_BLOCK, stride=ROW_CHUNKS), :] = y[:, c * LANES:(c + 1) * LANES]

        def scatter_chunk(base, n_real):
            new = []
            for u in range(_ROW_UNROLL):
                dst_row = tok_ref[0, 0, first + base + u]
                wgt = wrow_ref[0, 0, first + base + u]
                if n_real is not None:
                    dst_row = jnp.where(u < n_real, dst_row, group_len)
                    wgt = jnp.where(u < n_real, wgt, 0.0)
                dst = tile_at(dst_row)
                new.append((dst, acc[dst, :] + wgt * ybuf[tile_at((base + u) * ROW_CHUNKS), :]))
            for dst, val in new:
                acc[dst, :] = val

        def scatter_add(c, carry):
            scatter_chunk(c * _ROW_UNROLL, None)
            return carry
        lax.fori_loop(0, nfull, scatter_add, 0)

        @pl.when(tail > 0)
        def _():
            scatter_chunk(nfull * _ROW_UNROLL, tail)

    @pl.when(j == nbs - 1)
    def _():
        store_copy(g).start()

        @pl.when(g == n_groups - 1)
        def _():
            store_copy(g).wait()


def _experts(hrows, plan, wg, wu, wd, ts):
    blk_e, blk_cnt, blk_first, tok, wrow = plan
    n_groups, list_len = tok.shape[0], tok.shape[2]
    nbs = blk_e.shape[0] // n_groups
    d = D_MODEL
    idx_blk = (1, 1, list_len)
    idx_map = lambda g, j, be, bc, bf: (g, 0, 0)
    w_map = lambda g, j, be, bc, bf: (be[g * nbs + j], 0, 0)
    grid_spec = pltpu.PrefetchScalarGridSpec(
        num_scalar_prefetch=3,
        grid=(n_groups, nbs),
        in_specs=[
            pl.BlockSpec(idx_blk, idx_map, memory_space=pltpu.SMEM),
            pl.BlockSpec(idx_blk, idx_map, memory_space=pltpu.SMEM),
            pl.BlockSpec(memory_space=pl.ANY),
            pl.BlockSpec((1, d, D_FF_EXPERT), w_map),
            pl.BlockSpec((1, d, D_FF_EXPERT), w_map),
            pl.BlockSpec((1, D_FF_EXPERT, d), w_map),
        ],
        out_specs=pl.BlockSpec(memory_space=pl.ANY),
        scratch_shapes=[
            pltpu.VMEM((ts * ROW_CHUNKS, LANES), F32),
            pltpu.VMEM(((ts + 1) * ROW_CHUNKS, LANES), F32),
            pltpu.VMEM((MOE_BLOCK * ROW_CHUNKS, LANES), F32),
            pltpu.VMEM((MOE_BLOCK * ROW_CHUNKS, LANES), F32),
            pltpu.SemaphoreType.DMA((2,)),
        ],
    )
    return pl.pallas_call(
        _expert_kernel,
        out_shape=jax.ShapeDtypeStruct(hrows.shape, F32),
        grid_spec=grid_spec,
        compiler_params=_cparams(("arbitrary", "arbitrary")),
        name="experts",
    )(blk_e, blk_cnt, blk_first, tok, wrow, hrows, wg, wu, wd)


def _combine_kernel(x_ref, y_ref, gt_ref, o_ref):
    tm = x_ref.shape[0]
    gt = gt_ref[0]
    for c in range(ROW_CHUNKS):
        cols = slice(c * LANES, (c + 1) * LANES)
        o_ref[:, cols] = x_ref[:, cols] + gt[:, cols] * y_ref[pl.ds(c, tm, stride=ROW_CHUNKS), :]


def _combine(xn, yrows, gt, tm, tiles_per_seq):
    t, d = xn.shape
    nt = t // tm
    return pl.pallas_call(
        _combine_kernel,
        out_shape=jax.ShapeDtypeStruct((t, d), F32),
        grid=(nt,),
        in_specs=[
            pl.BlockSpec((tm, d), lambda i: (i, 0)),
            pl.BlockSpec((tm * ROW_CHUNKS, LANES), lambda i: (i, 0)),
            pl.BlockSpec((1, 1, d), lambda i: (i // tiles_per_seq, 0, 0)),
        ],
        out_specs=pl.BlockSpec((tm, d), lambda i: (i, 0)),
        compiler_params=_cparams(("arbitrary",)),
        name="moe_combine",
    )(xn, yrows, gt)


def _dispatch_plan(e_sel, w_sel, ts):
    t = e_sel.shape[0]
    n_groups = t // ts
    na = ts * TOP_K
    nbs = na // MOE_BLOCK + N_EXPERTS
    n_slots = nbs * MOE_BLOCK
    e_flat = e_sel.reshape(n_groups, ts, TOP_K).transpose(0, 2, 1).reshape(n_groups, na)
    w_flat = w_sel.reshape(n_groups, ts, TOP_K).transpose(0, 2, 1).reshape(n_groups, na)
    a_ids = jnp.broadcast_to(jnp.arange(na, dtype=jnp.int32)[None, :], (n_groups, na))
    _, a_sorted, w_sorted = lax.sort((e_flat, a_ids, w_flat), dimension=1, num_keys=1)
    experts = jnp.arange(N_EXPERTS, dtype=jnp.int32)
    counts = jnp.sum((e_flat[:, :, None] == experts[None, None, :]).astype(jnp.int32), axis=1)
    padded = (counts + MOE_BLOCK - 1) // MOE_BLOCK * MOE_BLOCK
    pad_end = jnp.cumsum(padded, axis=1)
    pad_start = pad_end - padded
    start = jnp.cumsum(counts, axis=1) - counts
    row0 = jnp.arange(nbs, dtype=jnp.int32) * MOE_BLOCK
    blk_e = jnp.minimum(
        jnp.sum((row0[None, :, None] >= pad_end[:, None, :]).astype(jnp.int32), axis=2),
        N_EXPERTS - 1)
    is_e = blk_e[:, :, None] == experts[None, None, :]
    take = lambda table: jnp.sum(jnp.where(is_e, table[:, None, :], 0), axis=2)
    off = row0[None, :] - take(pad_start)
    blk_cnt = jnp.clip(take(counts) - off, 0, MOE_BLOCK)
    blk_first = jnp.clip(take(start) + off, 0, na - 1)
    flat = lambda v: v.astype(jnp.int32).reshape(n_groups * nbs)
    filler = jnp.zeros((n_groups, LANES), jnp.int32)
    tok_rows = jnp.concatenate([(a_sorted % ts) * ROW_CHUNKS, filler], axis=1).astype(jnp.int32)
    w_list = jnp.concatenate([w_sorted.astype(F32), filler.astype(F32)], axis=1)
    return (flat(blk_e), flat(blk_cnt), flat(blk_first),
            tok_rows.reshape(n_groups, 1, na + LANES), w_list.reshape(n_groups, 1, na + LANES))


def kernel(x, c, w_ada, b_ada, g_mix, w_in, qn_a, kn_a, qn_b, kn_b, w_pa, w_pb, w_out,
           g_ffn, w_router, w_gate, w_up, w_down):
    bsz, seq, d = x.shape
    depth = w_ada.shape[0]
    t = bsz * seq
    tm = 512
    tiles_per_seq = seq // tm
    ts = min(4096, t)

    mod = _adaln(c, w_ada, b_ada)
    tables = _rope_tables(seq)
    bd_heads = 2 * LANES // HEAD_DIM
    bd = jnp.asarray(np.kron(np.eye(bd_heads), np.ones((HEAD_DIM, HEAD_DIM))), BF16)

    g4 = B_Q_HEADS // B_KV_HEADS
    head_perm = np.concatenate(
        [np.r_[np.arange(j * HEAD_DIM, (j + 1) * HEAD_DIM),
               np.arange((j + g4) * HEAD_DIM, (j + g4 + 1) * HEAD_DIM)] for j in range(g4)])
    qb_lo = 3 * A_WIDTH
    col_perm = np.arange(w_in.shape[2])
    col_perm[qb_lo:qb_lo + B_Q_WIDTH] = qb_lo + head_perm
    scale = HEAD_DIM ** -0.5 * float(np.log2(np.e))
    wr_t = w_router.T

    xf = x
    for l in range(depth):
        sh1, sc1, gt1, sh2, sc2, gt2 = [
            mod[l, :, i * d:(i + 1) * d].reshape(bsz, 1, d) for i in range(6)]
        w_in_l = w_in[l][:, col_perm].astype(BF16)
        gains = (
            jnp.tile(qn_a[l] * scale, A_HEADS).reshape(1, A_WIDTH),
            jnp.tile(kn_a[l], A_HEADS).reshape(1, A_WIDTH),
            jnp.tile(qn_b[l] * scale, B_Q_HEADS).reshape(1, B_Q_WIDTH),
            jnp.tile(kn_b[l], B_KV_HEADS).reshape(1, B_KV_WIDTH),
        )
        za, qb, kb, vb, ga, gb = _in_proj(xf.reshape(bsz, seq, d), sh1, sc1,
                                       g_mix[l].reshape(1, d), w_in_l, bd, gains, tables, tm)
        ya = _dilated(za)
        yb = _gqa(qb, kb, vb, 256)
        xn, hrows, e_sel, w_sel = _merge(
            xf.reshape(t, d), ya.reshape(t, A_WIDTH), yb.reshape(t, B_Q_WIDTH),
            ga.reshape(t, d), gb.reshape(t, d),
            w_pa[l].astype(BF16), w_pb[l][head_perm].astype(BF16), w_out[l].astype(BF16),
            gt1, g_ffn[l].reshape(1, d), sh2, sc2, wr_t, tm, tiles_per_seq)
        e_tok = e_sel.transpose(0, 2, 1).reshape(t, TOP_K)
        w_tok = w_sel.transpose(0, 2, 1).reshape(t, TOP_K)
        plan = _dispatch_plan(e_tok, w_tok, ts)
        yrows = _experts(hrows, plan, w_gate[l].astype(BF16), w_up[l].astype(BF16),
                         w_down[l].astype(BF16), ts)
        xf = _combine(xn, yrows, gt2, tm, tiles_per_seq)
    return xf.reshape(bsz, seq, d)
```

```python
import functools

import jax
import jax.numpy as jnp
import numpy as np
from jax import lax
from jax.experimental import pallas as pl
from jax.experimental.pallas import tpu as pltpu

D_MODEL = 1024
HEAD_DIM = 64
NORM_EPS = 1e-6
A_HEADS = 8
A_WIDTH = A_HEADS * HEAD_DIM
A_PATTERNS = ((128, 1), (512, 4), (2048, 16))
A_ROT_DIMS = HEAD_DIM // 4
A_ROPE_THETA = 500000.0
B_Q_HEADS = 8
B_KV_HEADS = 2
B_Q_WIDTH = B_Q_HEADS * HEAD_DIM
B_KV_WIDTH = B_KV_HEADS * HEAD_DIM
B_ROPE_THETA = 10000.0
GRID_W = 64
N_EXPERTS = 32
N_GROUPS = 4
EXPERTS_PER_GROUP = N_EXPERTS // N_GROUPS
TOP_K = 2
D_FF_EXPERT = D_MODEL // 2
MOE_BLOCK = 256

LANES = 128
SUBLANES = 8
ROW_CHUNKS = D_MODEL // LANES
VMEM_LIMIT = 56 * 1024 * 1024

F32 = jnp.float32
BF16 = jnp.bfloat16


def _cparams(sem):
    return pltpu.CompilerParams(dimension_semantics=sem, vmem_limit_bytes=VMEM_LIMIT)


def _adaln_kernel(c_ref, w_ref, b_ref, o_ref):
    c = c_ref[...]
    cond = c * jax.nn.sigmoid(c)
    o_ref[0] = jnp.dot(cond, w_ref[0], preferred_element_type=F32,
                       precision=lax.Precision.HIGHEST) + b_ref[0]


def _adaln(c, w_ada, b_ada):
    depth, d, n = w_ada.shape
    bsz = c.shape[0]
    tn = 1024
    return pl.pallas_call(
        _adaln_kernel,
        out_shape=jax.ShapeDtypeStruct((depth, bsz, n), F32),
        grid=(depth, n // tn),
        in_specs=[
            pl.BlockSpec((bsz, d), lambda l, j: (0, 0)),
            pl.BlockSpec((1, d, tn), lambda l, j: (l, 0, j)),
            pl.BlockSpec((1, 1, tn), lambda l, j: (l, 0, j)),
        ],
        out_specs=pl.BlockSpec((1, bsz, tn), lambda l, j: (l, 0, j)),
        compiler_params=_cparams(("arbitrary", "arbitrary")),
        name="adaln",
    )(c, w_ada, b_ada.reshape(depth, 1, n))


def _rope_tables(seq):
    pos = jnp.arange(seq, dtype=F32)
    row = jnp.floor(pos / GRID_W)
    col = pos - row * GRID_W
    d = np.arange(LANES) % HEAD_DIM

    def build(segments):
        c = jnp.ones((seq, LANES), F32)
        s1 = jnp.zeros((seq, LANES), F32)
        s2 = jnp.zeros((seq, LANES), F32)
        for lo, half, theta, p in segments:
            first = (d >= lo) & (d < lo + half)
            second = (d >= lo + half) & (d < lo + 2 * half)
            idx = np.where(first, d - lo, np.where(second, d - lo - half, 0))
            freqs = theta ** (-jnp.arange(half, dtype=F32) / half)
            ang = p[:, None] * freqs[idx][None, :]
            cs, sn = jnp.cos(ang), jnp.sin(ang)
            rot = jnp.asarray(first | second)[None, :]
            c = jnp.where(rot, cs, c)
            s1 = jnp.where(jnp.asarray(first)[None, :], -sn, s1)
            s2 = jnp.where(jnp.asarray(second)[None, :], sn, s2)
        return c, s1, s2

    ta = build([(0, A_ROT_DIMS // 2, A_ROPE_THETA, pos)])
    q = HEAD_DIM // 4
    tb = build([(0, q, B_ROPE_THETA, row), (2 * q, q, B_ROPE_THETA, col)])
    return ta + tb


def _in_proj_kernel(x_ref, sh_ref, sc_ref, g_ref, w_ref, bd_ref,
                    gqa_ref, gka_ref, gqb_ref, gkb_ref,
                    ca_ref, s1a_ref, s2a_ref, cb_ref, s1b_ref, s2b_ref,
                    za_ref, qb_ref, kb_ref, vb_ref, ga_ref, gb_ref):
    x = x_ref[0]
    ms = jnp.mean(x * x, axis=-1, keepdims=True)
    h = x * lax.rsqrt(ms + NORM_EPS) * g_ref[...]
    h = h * (1.0 + sc_ref[0]) + sh_ref[0]
    hb = h.astype(BF16)

    def seg(lo, width):
        return jnp.dot(hb, w_ref[:, lo:lo + width], preferred_element_type=F32)

    def qk_norm(z, gain_ref):
        width = z.shape[-1]
        bdw = min(width, bd_ref.shape[0])
        sq = (z * z).astype(BF16)
        parts = [jnp.dot(sq[:, lo:lo + bdw], bd_ref[:bdw, :bdw], preferred_element_type=F32)
                 for lo in range(0, width, bdw)]
        ss = parts[0] if len(parts) == 1 else jnp.concatenate(parts, axis=-1)
        return z * lax.rsqrt(ss * (1.0 / HEAD_DIM) + NORM_EPS) * gain_ref[...]

    def tile(t, width):
        reps = width // LANES
        return t if reps == 1 else jnp.concatenate([t] * reps, axis=-1)

    def rope(z, c_ref, s1_ref, s2_ref, half):
        width = z.shape[-1]
        up = pltpu.roll(z, width - half, 1)
        dn = pltpu.roll(z, half, 1)
        return (z * tile(c_ref[...], width) + up * tile(s1_ref[...], width)
                + dn * tile(s2_ref[...], width))

    ha = A_ROT_DIMS // 2
    hq = HEAD_DIM // 4
    o = 0
    qa = rope(qk_norm(seg(o, A_WIDTH), gqa_ref), ca_ref, s1a_ref, s2a_ref, ha)
    za_ref[0, :, 0:A_WIDTH] = qa
    o += A_WIDTH
    ka = rope(qk_norm(seg(o, A_WIDTH), gka_ref), ca_ref, s1a_ref, s2a_ref, ha)
    za_ref[0, :, A_WIDTH:2 * A_WIDTH] = ka
    o += A_WIDTH
    za_ref[0, :, 2 * A_WIDTH:3 * A_WIDTH] = seg(o, A_WIDTH)
    o += A_WIDTH
    qb = rope(qk_norm(seg(o, B_Q_WIDTH), gqb_ref), cb_ref, s1b_ref, s2b_ref, hq)
    qb_ref[0] = qb.astype(BF16)
    o += B_Q_WIDTH
    kb = rope(qk_norm(seg(o, B_KV_WIDTH), gkb_ref), cb_ref, s1b_ref, s2b_ref, hq)
    kb_ref[0] = kb.astype(BF16)
    o += B_KV_WIDTH
    vb_ref[0] = seg(o, B_KV_WIDTH).astype(BF16)
    o += B_KV_WIDTH
    ga_ref[0] = jax.nn.sigmoid(seg(o, D_MODEL)).astype(BF16)
    o += D_MODEL
    gb_ref[0] = jax.nn.sigmoid(seg(o, D_MODEL)).astype(BF16)


def _in_proj(x, sh, sc, g, w_in, bd, gains, tables, tm):
    bsz, seq, d = x.shape
    n_in = w_in.shape[1]
    nst = seq // tm
    row = lambda st, b: (b, st, 0)
    per_b = lambda st, b: (b, 0, 0)
    const2 = lambda st, b: (0, 0)
    tab = pl.BlockSpec((tm, LANES), lambda st, b: (st, 0))
    return pl.pallas_call(
        _in_proj_kernel,
        out_shape=(
            jax.ShapeDtypeStruct((bsz, seq, 3 * A_WIDTH), F32),
            jax.ShapeDtypeStruct((bsz, seq, B_Q_WIDTH), BF16),
            jax.ShapeDtypeStruct((bsz, seq, B_KV_WIDTH), BF16),
            jax.ShapeDtypeStruct((bsz, seq, B_KV_WIDTH), BF16),
            jax.ShapeDtypeStruct((bsz, seq, d), BF16),
            jax.ShapeDtypeStruct((bsz, seq, d), BF16),
        ),
        grid=(nst, bsz),
        in_specs=[
            pl.BlockSpec((1, tm, d), row),
            pl.BlockSpec((1, 1, d), per_b),
            pl.BlockSpec((1, 1, d), per_b),
            pl.BlockSpec((1, d), const2),
            pl.BlockSpec((d, n_in), const2),
            pl.BlockSpec(bd.shape, const2),
            pl.BlockSpec((1, A_WIDTH), const2),
            pl.BlockSpec((1, A_WIDTH), const2),
            pl.BlockSpec((1, B_Q_WIDTH), const2),
            pl.BlockSpec((1, B_KV_WIDTH), const2),
            tab, tab, tab, tab, tab, tab,
        ],
        out_specs=(
            pl.BlockSpec((1, tm, 3 * A_WIDTH), row),
            pl.BlockSpec((1, tm, B_Q_WIDTH), row),
            pl.BlockSpec((1, tm, B_KV_WIDTH), row),
            pl.BlockSpec((1, tm, B_KV_WIDTH), row),
            pl.BlockSpec((1, tm, d), row),
            pl.BlockSpec((1, tm, d), row),
        ),
        compiler_params=_cparams(("arbitrary", "arbitrary")),
        name="in_proj",
    )(x, sh, sc, g, w_in, bd, *gains, *tables)


_QB = 128
_UNROLL = 8


def _dilated_kernel(q_ref, k_ref, v_ref, o_ref, m_scr, l_scr, acc_scr, bias_scr):
    seq = q_ref.shape[1]
    lane = lax.broadcasted_iota(jnp.int32, (1, LANES), 1)
    head0 = lane < HEAD_DIM
    radius = A_PATTERNS[0][0] // (2 * A_PATTERNS[0][1])
    assert all(w // (2 * d) == radius for w, d in A_PATTERNS)

    @pl.when((pl.program_id(0) == 0) & (pl.program_id(1) == 0))
    def _():
        qrow = lax.broadcasted_iota(jnp.int32, (2 * _QB, 2 * _QB), 0) & (_QB - 1)
        kcol = lax.broadcasted_iota(jnp.int32, (2 * _QB, 2 * _QB), 1)
        for i in range(3):
            bias_scr[i] = jnp.where(jnp.abs(kcol - qrow - radius * i) <= radius, 0.0, -jnp.inf)

    def merge(t):
        return jnp.where(head0, t[:_QB], t[_QB:])

    patterns = sorted(A_PATTERNS, key=lambda wd: -wd[1])
    assert patterns[-1][1] == 1 and seq % (patterns[0][1] * _QB) == 0
    for pi, (window, dil) in enumerate(patterns):
        sub_len = seq // dil
        kw = min(2 * _QB, sub_len)
        nqb = sub_len // _QB
        nblk = dil * nqb

        def wide(t, kw=kw):
            return t if kw == LANES else jnp.concatenate([t] * (kw // LANES), axis=-1)

        def load(blk, pi=pi, dil=dil, sub_len=sub_len, kw=kw, nqb=nqb):
            r = blk // nqb
            i0q = (blk % nqb) * _QB
            i0k = jnp.clip(i0q - (kw - _QB) // 2, 0, sub_len - kw)
            if dil == 1:
                qrows = pl.ds(pl.multiple_of(i0q, _QB), _QB)
                krows = pl.ds(pl.multiple_of(i0k, SUBLANES), kw)
            else:
                qrows = pl.ds(r + dil * i0q, _QB, stride=dil)
                krows = pl.ds(r + dil * i0k, kw, stride=dil)
            q = q_ref[0, qrows, :]
            kk = k_ref[0, krows, :].astype(BF16)
            vv = v_ref[0, krows, :].astype(BF16)
            old = None
            if pi > 0:
                old = (jnp.concatenate([m_scr[0, qrows, :], m_scr[1, qrows, :]], axis=0),
                       l_scr[qrows, :], acc_scr[qrows, :])
            return qrows, (i0q - i0k) // radius, q, kk, vv, old

        def compute(mask_id, q, kk, vv, old, kw=kw, wide=wide):
            bias = bias_scr[0, :, 0:LANES] if kw == LANES else bias_scr[mask_id]
            q2 = jnp.concatenate([jnp.where(head0, q, 0.0), jnp.where(head0, 0.0, q)],
                                 axis=0).astype(BF16)
            s = lax.dot_general(q2, kk, (((1,), (1,)), ((), ())),
                                preferred_element_type=F32) + bias
            mb = jnp.broadcast_to(jnp.max(s, axis=-1, keepdims=True), (2 * _QB, LANES))
            m_new = mb if old is None else jnp.maximum(old[0], mb)
            p = jnp.exp2((s - wide(m_new)).astype(BF16))
            v_aug = jnp.concatenate([vv, jnp.ones_like(vv)], axis=-1)
            oa = jnp.dot(p, v_aug, preferred_element_type=F32)
            pv = merge(oa[:, :LANES])
            psum = merge(oa[:, LANES:])
            if old is None:
                return m_new, psum, pv
            alpha = jnp.exp2(merge(old[0]) - merge(m_new))
            return m_new, alpha * old[1] + psum, alpha * old[2] + pv

        def group(it, carry, load=load, compute=compute, last=pi == len(patterns) - 1):
            loaded = [load(it * _UNROLL + u) for u in range(_UNROLL)]
            results = [compute(*ld[1:]) for ld in loaded]
            for ld, (m_new, l_new, acc_new) in zip(loaded, results):
                qrows = ld[0]
                if last:
                    o_ref[0, qrows, :] = (acc_new / l_new).astype(o_ref.dtype)
                else:
                    m_scr[0, qrows, :] = m_new[:_QB]
                    m_scr[1, qrows, :] = m_new[_QB:]
                    l_scr[qrows, :] = l_new
                    acc_scr[qrows, :] = acc_new
            return carry

        lax.fori_loop(0, nblk // _UNROLL, group, 0)


def _dilated(za):
    bsz, seq, _ = za.shape
    nhp = A_WIDTH // LANES
    blk = (1, seq, LANES)
    return pl.pallas_call(
        _dilated_kernel,
        out_shape=jax.ShapeDtypeStruct((bsz, seq, A_WIDTH), BF16),
        grid=(bsz, nhp),
        in_specs=[
            pl.BlockSpec(blk, lambda b, j: (b, 0, j)),
            pl.BlockSpec(blk, lambda b, j: (b, 0, nhp + j)),
            pl.BlockSpec(blk, lambda b, j: (b, 0, 2 * nhp + j)),
        ],
        out_specs=pl.BlockSpec(blk, lambda b, j: (b, 0, j)),
        scratch_shapes=[
            pltpu.VMEM((2, seq, LANES), F32),
            pltpu.VMEM((seq, LANES), F32),
            pltpu.VMEM((seq, LANES), F32),
            pltpu.VMEM((3, 2 * _QB, 2 * _QB), F32),
        ],
        compiler_params=_cparams(("arbitrary", "arbitrary")),
        name="dilated_attn",
    )(za, za, za)


def _gqa_kernel(q_ref, k_ref, v_ref, o_ref):
    tq = q_ref.shape[1]
    lane = lax.broadcasted_iota(jnp.int32, (1, LANES), 1)
    head0 = lane < HEAD_DIM
    kk = k_ref[0]
    zero = jnp.zeros((), BF16)
    v_aug = jnp.concatenate([v_ref[0], jnp.ones_like(v_ref[0])], axis=-1)
    for j in range(B_Q_WIDTH // LANES):
        qj = q_ref[0, :, j * LANES:(j + 1) * LANES]
        q2 = jnp.concatenate([jnp.where(head0, qj, zero), jnp.where(head0, zero, qj)], axis=0)
        s = lax.dot_general(q2, kk, (((1,), (1,)), ((), ())), preferred_element_type=F32)
        m = jnp.max(s, axis=-1, keepdims=True)
        pb = jnp.exp2((s - m).astype(BF16))
        oa = jnp.dot(pb, v_aug, preferred_element_type=F32)
        pv = oa[:, :LANES] / oa[:, LANES:]
        o_ref[0, :, j * LANES:(j + 1) * LANES] = jnp.where(
            head0, pv[:tq], pv[tq:]).astype(o_ref.dtype)


def _gqa(qb, kb, vb, tq):
    bsz, seq, _ = qb.shape
    kv_spec = pl.BlockSpec((1, seq, B_KV_WIDTH), lambda b, i: (b, 0, 0))
    return pl.pallas_call(
        _gqa_kernel,
        out_shape=jax.ShapeDtypeStruct((bsz, seq, B_Q_WIDTH), BF16),
        grid=(bsz, seq // tq),
        in_specs=[pl.BlockSpec((1, tq, B_Q_WIDTH), lambda b, i: (b, i, 0)), kv_spec, kv_spec],
        out_specs=pl.BlockSpec((1, tq, B_Q_WIDTH), lambda b, i: (b, i, 0)),
        compiler_params=_cparams(("arbitrary", "arbitrary")),
        name="gqa_attn",
    )(qb, kb, vb)


def _merge_kernel(x_ref, ya_ref, yb_ref, ga_ref, gb_ref, wpa_ref, wpb_ref, wo_ref,
                  gt_ref, g_ref, sh_ref, sc_ref, wr_ref,
                  xn_ref, hrow_ref, e_ref, w_ref):
    tm = x_ref.shape[0]
    pa = jnp.dot(ya_ref[...], wpa_ref[...], preferred_element_type=F32)
    pb = jnp.dot(yb_ref[...], wpb_ref[...], preferred_element_type=F32)
    merged = ga_ref[...].astype(F32) * pa + gb_ref[...].astype(F32) * pb
    out = jnp.dot(merged.astype(BF16), wo_ref[...], preferred_element_type=F32)
    xn = x_ref[...] + gt_ref[0] * out
    xn_ref[...] = xn

    ms = jnp.mean(xn * xn, axis=-1, keepdims=True)
    h = xn * lax.rsqrt(ms + NORM_EPS) * g_ref[...]
    h = h * (1.0 + sc_ref[0]) + sh_ref[0]
    for c in range(ROW_CHUNKS):
        hrow_ref[pl.ds(c, tm, stride=ROW_CHUNKS), :] = h[:, c * LANES:(c + 1) * LANES]

    logits = lax.dot_general(wr_ref[...], h, (((1,), (1,)), ((), ())),
                             preferred_element_type=F32,
                             precision=lax.Precision.HIGHEST)
    mx = jnp.max(logits, axis=0, keepdims=True)
    ex = jnp.exp(logits - mx)
    probs = ex / jnp.sum(ex, axis=0, keepdims=True)
    pg = probs.reshape(N_GROUPS, EXPERTS_PER_GROUP, tm)
    sub = lax.broadcasted_iota(jnp.int32, pg.shape, 1).astype(F32)
    m1 = jnp.max(pg, axis=1, keepdims=True)
    i1 = jnp.min(jnp.where(pg == m1, sub, float(EXPERTS_PER_GROUP)), axis=1, keepdims=True)
    pg2 = jnp.where(sub == i1, -1.0, pg)
    m2 = jnp.max(pg2, axis=1, keepdims=True)
    i2 = jnp.min(jnp.where(pg2 == m2, sub, float(EXPERTS_PER_GROUP)), axis=1, keepdims=True)
    score = m1 + m2
    gid = lax.broadcasted_iota(jnp.int32, score.shape, 0).astype(F32)
    best = jnp.max(score, axis=0, keepdims=True)
    gsel = jnp.min(jnp.where(score == best, gid, float(N_GROUPS)), axis=0, keepdims=True)
    pick = gid == gsel
    w0 = jnp.sum(jnp.where(pick, m1, 0.0), axis=0)
    w1 = jnp.sum(jnp.where(pick, m2, 0.0), axis=0)
    j0 = jnp.sum(jnp.where(pick, i1, 0.0), axis=0)
    j1 = jnp.sum(jnp.where(pick, i2, 0.0), axis=0)
    base = gsel[0] * float(EXPERTS_PER_GROUP)
    tot = w0 + w1
    e_ref[0, 0:1, :] = (base + j0).astype(jnp.int32)
    e_ref[0, 1:2, :] = (base + j1).astype(jnp.int32)
    w_ref[0, 0:1, :] = w0 / tot
    w_ref[0, 1:2, :] = w1 / tot


def _merge(x, ya, yb, ga, gb, wpa, wpb, wo, gt, g, sh, sc, wr_t, tm, tiles_per_seq):
    t, d = x.shape
    nt = t // tm
    row = lambda i: (i, 0)
    per_b = lambda i: (i // tiles_per_seq, 0, 0)
    const2 = lambda i: (0, 0)
    return pl.pallas_call(
        _merge_kernel,
        out_shape=(
            jax.ShapeDtypeStruct((t, d), F32),
            jax.ShapeDtypeStruct((t * ROW_CHUNKS, LANES), F32),
            jax.ShapeDtypeStruct((nt, TOP_K, tm), jnp.int32),
            jax.ShapeDtypeStruct((nt, TOP_K, tm), F32),
        ),
        grid=(nt,),
        in_specs=[
            pl.BlockSpec((tm, d), row),
            pl.BlockSpec((tm, A_WIDTH), row),
            pl.BlockSpec((tm, B_Q_WIDTH), row),
            pl.BlockSpec((tm, d), row),
            pl.BlockSpec((tm, d), row),
            pl.BlockSpec((A_WIDTH, d), const2),
            pl.BlockSpec((B_Q_WIDTH, d), const2),
            pl.BlockSpec((d, d), const2),
            pl.BlockSpec((1, 1, d), per_b),
            pl.BlockSpec((1, d), const2),
            pl.BlockSpec((1, 1, d), per_b),
            pl.BlockSpec((1, 1, d), per_b),
            pl.BlockSpec((N_EXPERTS, d), const2),
        ],
        out_specs=(
            pl.BlockSpec((tm, d), row),
            pl.BlockSpec((tm * ROW_CHUNKS, LANES), row),
            pl.BlockSpec((1, TOP_K, tm), lambda i: (i, 0, 0)),
            pl.BlockSpec((1, TOP_K, tm), lambda i: (i, 0, 0)),
        ),
        compiler_params=_cparams(("arbitrary",)),
        name="merge_router",
    )(x, ya, yb, ga, gb, wpa, wpb, wo, gt, g, sh, sc, wr_t)


_ROW_UNROLL = 8
_BLOCKS_PER_STEP = 2


def _expert_kernel(blk_e_ref, blk_cnt_ref, blk_base_ref, tok_ref, wrow_ref, h_hbm, *refs):
    weight_refs = refs[:3 * _BLOCKS_PER_STEP]
    acc_hbm, xs, acc, gbuf, ybuf, sem = refs[3 * _BLOCKS_PER_STEP:]
    g = pl.program_id(0)
    j = pl.program_id(1)
    n_groups = pl.num_programs(0)
    n_steps = pl.num_programs(1)
    group_len = xs.shape[0]

    def group_rows(gi):
        return pl.ds(pl.multiple_of(gi * group_len, ROW_CHUNKS), group_len)

    def store_copy(gi):
        return pltpu.make_async_copy(acc.at[pl.ds(0, group_len), :],
                                     acc_hbm.at[group_rows(gi), :], sem.at[1])

    @pl.when(j == 0)
    def _():
        load = pltpu.make_async_copy(h_hbm.at[group_rows(g), :], xs, sem.at[0])
        load.start()

        @pl.when(g > 0)
        def _():
            store_copy(g - 1).wait()
        acc[...] = jnp.zeros_like(acc)

        @pl.when(g == 0)
        def _():
            gbuf[...] = jnp.zeros_like(gbuf)
        load.wait()

    for sub in range(_BLOCKS_PER_STEP):
        blk = (g * n_steps + j) * _BLOCKS_PER_STEP + sub
        pl.when(blk_cnt_ref[blk] > 0)(functools.partial(
            _expert_block, blk_cnt_ref[blk], blk_base_ref[blk], tok_ref, wrow_ref,
            *weight_refs[3 * sub:3 * sub + 3], xs, acc, gbuf, ybuf))

    @pl.when(j == n_steps - 1)
    def _():
        store_copy(g).start()

        @pl.when(g == n_groups - 1)
        def _():
            store_copy(g).wait()


def _expert_block(cnt, first, tok_ref, wrow_ref, wg_ref, wu_ref, wd_ref, xs, acc, gbuf, ybuf):
    group_len = xs.shape[0]

    def tile_at(row):
        return pl.ds(pl.multiple_of(row, ROW_CHUNKS), ROW_CHUNKS)

    nfull = cnt // _ROW_UNROLL
    tail = cnt - nfull * _ROW_UNROLL

    def gather(c, carry):
        base = c * _ROW_UNROLL
        for u in range(_ROW_UNROLL):
            gbuf[tile_at((base + u) * ROW_CHUNKS), :] = xs[
                tile_at(tok_ref[0, 0, first + base + u]), :]
        return carry
    lax.fori_loop(0, (cnt + _ROW_UNROLL - 1) // _ROW_UNROLL, gather, 0)

    xb = jnp.concatenate(
        [gbuf[pl.ds(c, MOE_BLOCK, stride=ROW_CHUNKS), :] for c in range(ROW_CHUNKS)],
        axis=-1).astype(BF16)
    gate = jnp.dot(xb, wg_ref[0], preferred_element_type=F32)
    up = jnp.dot(xb, wu_ref[0], preferred_element_type=F32)
    hid = (gate * jax.nn.sigmoid(gate) * up).astype(BF16)
    y = jnp.dot(hid, wd_ref[0], preferred_element_type=F32)
    for c in range(ROW_CHUNKS):
        ybuf[pl.ds(c, MOE_BLOCK, stride=ROW_CHUNKS), :] = y[:, c * LANES:(c + 1) * LANES]

    def scatter_chunk(base, n_real):
        new = []
        for u in range(_ROW_UNROLL):
            dst_row = tok_ref[0, 0, first + base + u]
            wgt = wrow_ref[0, 0, first + base + u]
            if n_real is not None:
                dst_row = jnp.where(u < n_real, dst_row, group_len)
                wgt = jnp.where(u < n_real, wgt, 0.0)
            dst = tile_at(dst_row)
            new.append((dst, acc[dst, :] + wgt * ybuf[tile_at((base + u) * ROW_CHUNKS), :]))
        for dst, val in new:
            acc[dst, :] = val

    def scatter_add(c, carry):
        scatter_chunk(c * _ROW_UNROLL, None)
        return carry
    lax.fori_loop(0, nfull, scatter_add, 0)

    @pl.when(tail > 0)
    def _():
        scatter_chunk(nfull * _ROW_UNROLL, tail)


def _experts(hrows, plan, wg, wu, wd, ts):
    blk_e, blk_cnt, blk_first, tok, wrow = plan
    n_groups, list_len = tok.shape[0], tok.shape[2]
    n_steps = blk_e.shape[0] // (n_groups * _BLOCKS_PER_STEP)
    d = D_MODEL
    idx_blk = (1, 1, list_len)
    idx_map = lambda g, j, be, bc, bf: (g, 0, 0)
    weight_specs = []
    for sub in range(_BLOCKS_PER_STEP):
        w_map = lambda g, j, be, bc, bf, sub=sub: (
            be[(g * n_steps + j) * _BLOCKS_PER_STEP + sub], 0, 0)
        weight_specs += [pl.BlockSpec((1, d, D_FF_EXPERT), w_map),
                         pl.BlockSpec((1, d, D_FF_EXPERT), w_map),
                         pl.BlockSpec((1, D_FF_EXPERT, d), w_map)]
    grid_spec = pltpu.PrefetchScalarGridSpec(
        num_scalar_prefetch=3,
        grid=(n_groups, n_steps),
        in_specs=[
            pl.BlockSpec(idx_blk, idx_map, memory_space=pltpu.SMEM),
            pl.BlockSpec(idx_blk, idx_map, memory_space=pltpu.SMEM),
            pl.BlockSpec(memory_space=pl.ANY),
        ] + weight_specs,
        out_specs=pl.BlockSpec(memory_space=pl.ANY),
        scratch_shapes=[
            pltpu.VMEM((ts * ROW_CHUNKS, LANES), F32),
            pltpu.VMEM(((ts + 1) * ROW_CHUNKS, LANES), F32),
            pltpu.VMEM((MOE_BLOCK * ROW_CHUNKS, LANES), F32),
            pltpu.VMEM((MOE_BLOCK * ROW_CHUNKS, LANES), F32),
            pltpu.SemaphoreType.DMA((2,)),
        ],
    )
    return pl.pallas_call(
        _expert_kernel,
        out_shape=jax.ShapeDtypeStruct(hrows.shape, F32),
        grid_spec=grid_spec,
        compiler_params=_cparams(("arbitrary", "arbitrary")),
        name="experts",
    )(blk_e, blk_cnt, blk_first, tok, wrow, hrows, *([wg, wu, wd] * _BLOCKS_PER_STEP))


def _combine_kernel(x_ref, y_ref, gt_ref, o_ref):
    tm = x_ref.shape[0]
    gt = gt_ref[0]
    for c in range(ROW_CHUNKS):
        cols = slice(c * LANES, (c + 1) * LANES)
        o_ref[:, cols] = x_ref[:, cols] + gt[:, cols] * y_ref[pl.ds(c, tm, stride=ROW_CHUNKS), :]


def _combine(xn, yrows, gt, tm, tiles_per_seq):
    t, d = xn.shape
    nt = t // tm
    return pl.pallas_call(
        _combine_kernel,
        out_shape=jax.ShapeDtypeStruct((t, d), F32),
        grid=(nt,),
        in_specs=[
            pl.BlockSpec((tm, d), lambda i: (i, 0)),
            pl.BlockSpec((tm * ROW_CHUNKS, LANES), lambda i: (i, 0)),
            pl.BlockSpec((1, 1, d), lambda i: (i // tiles_per_seq, 0, 0)),
        ],
        out_specs=pl.BlockSpec((tm, d), lambda i: (i, 0)),
        compiler_params=_cparams(("arbitrary",)),
        name="moe_combine",
    )(xn, yrows, gt)


def _dispatch_plan(e_sel, w_sel, ts):
    t = e_sel.shape[0]
    n_groups = t // ts
    na = ts * TOP_K
    nbs = na // MOE_BLOCK + N_EXPERTS
    nbs = -(-nbs // _BLOCKS_PER_STEP) * _BLOCKS_PER_STEP
    e_flat = e_sel.reshape(n_groups, ts, TOP_K).transpose(0, 2, 1).reshape(n_groups, na)
    w_flat = w_sel.reshape(n_groups, ts, TOP_K).transpose(0, 2, 1).reshape(n_groups, na)
    a_ids = jnp.broadcast_to(jnp.arange(na, dtype=jnp.int32)[None, :], (n_groups, na))
    _, a_sorted, w_sorted = lax.sort((e_flat, a_ids, w_flat), dimension=1, num_keys=1)
    experts = jnp.arange(N_EXPERTS, dtype=jnp.int32)
    counts = jnp.sum((e_flat[:, :, None] == experts[None, None, :]).astype(jnp.int32), axis=1)
    padded = (counts + MOE_BLOCK - 1) // MOE_BLOCK * MOE_BLOCK
    pad_end = jnp.cumsum(padded, axis=1)
    pad_start = pad_end - padded
    start = jnp.cumsum(counts, axis=1) - counts
    row0 = jnp.arange(nbs, dtype=jnp.int32) * MOE_BLOCK
    blk_e = jnp.minimum(
        jnp.sum((row0[None, :, None] >= pad_end[:, None, :]).astype(jnp.int32), axis=2),
        N_EXPERTS - 1)
    is_e = blk_e[:, :, None] == experts[None, None, :]
    take = lambda table: jnp.sum(jnp.where(is_e, table[:, None, :], 0), axis=2)
    off = row0[None, :] - take(pad_start)
    blk_cnt = jnp.clip(take(counts) - off, 0, MOE_BLOCK)
    blk_first = jnp.clip(take(start) + off, 0, na - 1)
    flat = lambda v: v.astype(jnp.int32).reshape(n_groups * nbs)
    filler = jnp.zeros((n_groups, LANES), jnp.int32)
    tok_rows = jnp.concatenate([(a_sorted % ts) * ROW_CHUNKS, filler], axis=1).astype(jnp.int32)
    w_list = jnp.concatenate([w_sorted.astype(F32), filler.astype(F32)], axis=1)
    return (flat(blk_e), flat(blk_cnt), flat(blk_first),
            tok_rows.reshape(n_groups, 1, na + LANES), w_list.reshape(n_groups, 1, na + LANES))


def kernel(x, c, w_ada, b_ada, g_mix, w_in, qn_a, kn_a, qn_b, kn_b, w_pa, w_pb, w_out,
           g_ffn, w_router, w_gate, w_up, w_down):
    bsz, seq, d = x.shape
    depth = w_ada.shape[0]
    t = bsz * seq
    tm = 512
    tiles_per_seq = seq // tm
    ts = min(4096, t)

    mod = _adaln(c, w_ada, b_ada)
    tables = _rope_tables(seq)
    bd_heads = 2 * LANES // HEAD_DIM
    bd = jnp.asarray(np.kron(np.eye(bd_heads), np.ones((HEAD_DIM, HEAD_DIM))), BF16)

    g4 = B_Q_HEADS // B_KV_HEADS
    head_perm = np.concatenate(
        [np.r_[np.arange(j * HEAD_DIM, (j + 1) * HEAD_DIM),
               np.arange((j + g4) * HEAD_DIM, (j + g4 + 1) * HEAD_DIM)] for j in range(g4)])
    qb_lo = 3 * A_WIDTH
    col_perm = np.arange(w_in.shape[2])
    col_perm[qb_lo:qb_lo + B_Q_WIDTH] = qb_lo + head_perm
    scale = HEAD_DIM ** -0.5 * float(np.log2(np.e))
    wr_t = w_router.T

    xf = x
    for l in range(depth):
        sh1, sc1, gt1, sh2, sc2, gt2 = [
            mod[l, :, i * d:(i + 1) * d].reshape(bsz, 1, d) for i in range(6)]
        w_in_l = w_in[l][:, col_perm].astype(BF16)
        gains = (
            jnp.tile(qn_a[l] * scale, A_HEADS).reshape(1, A_WIDTH),
            jnp.tile(kn_a[l], A_HEADS).reshape(1, A_WIDTH),
            jnp.tile(qn_b[l] * scale, B_Q_HEADS).reshape(1, B_Q_WIDTH),
            jnp.tile(kn_b[l], B_KV_HEADS).reshape(1, B_KV_WIDTH),
        )
        za, qb, kb, vb, ga, gb = _in_proj(xf.reshape(bsz, seq, d), sh1, sc1,
                                       g_mix[l].reshape(1, d), w_in_l, bd, gains, tables, tm)
        ya = _dilated(za)
        yb = _gqa(qb, kb, vb, 512)
        xn, hrows, e_sel, w_sel = _merge(
            xf.reshape(t, d), ya.reshape(t, A_WIDTH), yb.reshape(t, B_Q_WIDTH),
            ga.reshape(t, d), gb.reshape(t, d),
            w_pa[l].astype(BF16), w_pb[l][head_perm].astype(BF16), w_out[l].astype(BF16),
            gt1, g_ffn[l].reshape(1, d), sh2, sc2, wr_t, tm, tiles_per_seq)
        e_tok = e_sel.transpose(0, 2, 1).reshape(t, TOP_K)
        w_tok = w_sel.transpose(0, 2, 1).reshape(t, TOP_K)
        plan = _dispatch_plan(e_tok, w_tok, ts)
        yrows = _experts(hrows, plan, w_gate[l].astype(BF16), w_up[l].astype(BF16),
                         w_down[l].astype(BF16), ts)
        xf = _combine(xn, yrows, gt2, tm, tiles_per_seq)
    return xf.reshape(bsz, seq, d)
```

```python
import functools

import jax
import jax.numpy as jnp
import numpy as np
from jax import lax
from jax.experimental import pallas as pl
from jax.experimental.pallas import tpu as pltpu

D_MODEL = 1024
HEAD_DIM = 64
NORM_EPS = 1e-6
A_HEADS = 8
A_WIDTH = A_HEADS * HEAD_DIM
A_PATTERNS = ((128, 1), (512, 4), (2048, 16))
A_ROT_DIMS = HEAD_DIM // 4
A_ROPE_THETA = 500000.0
B_Q_HEADS = 8
B_KV_HEADS = 2
B_Q_WIDTH = B_Q_HEADS * HEAD_DIM
B_KV_WIDTH = B_KV_HEADS * HEAD_DIM
B_ROPE_THETA = 10000.0
GRID_W = 64
N_EXPERTS = 32
N_GROUPS = 4
EXPERTS_PER_GROUP = N_EXPERTS // N_GROUPS
TOP_K = 2
D_FF_EXPERT = D_MODEL // 2
MOE_BLOCK = 256

LANES = 128
SUBLANES = 8
ROW_CHUNKS = D_MODEL // LANES
VMEM_LIMIT = 56 * 1024 * 1024

F32 = jnp.float32
BF16 = jnp.bfloat16


def _cparams(sem):
    return pltpu.CompilerParams(dimension_semantics=sem, vmem_limit_bytes=VMEM_LIMIT)


def _adaln_kernel(c_ref, w_ref, b_ref, o_ref):
    c = c_ref[...]
    cond = c * jax.nn.sigmoid(c)
    o_ref[0] = jnp.dot(cond, w_ref[0], preferred_element_type=F32,
                       precision=lax.Precision.HIGHEST) + b_ref[0]


def _adaln(c, w_ada, b_ada):
    depth, d, n = w_ada.shape
    bsz = c.shape[0]
    tn = 1024
    return pl.pallas_call(
        _adaln_kernel,
        out_shape=jax.ShapeDtypeStruct((depth, bsz, n), F32),
        grid=(depth, n // tn),
        in_specs=[
            pl.BlockSpec((bsz, d), lambda l, j: (0, 0)),
            pl.BlockSpec((1, d, tn), lambda l, j: (l, 0, j)),
            pl.BlockSpec((1, 1, tn), lambda l, j: (l, 0, j)),
        ],
        out_specs=pl.BlockSpec((1, bsz, tn), lambda l, j: (l, 0, j)),
        compiler_params=_cparams(("arbitrary", "arbitrary")),
        name="adaln",
    )(c, w_ada, b_ada.reshape(depth, 1, n))


def _rope_tables(seq):
    pos = jnp.arange(seq, dtype=F32)
    row = jnp.floor(pos / GRID_W)
    col = pos - row * GRID_W
    d = np.arange(LANES) % HEAD_DIM

    def build(segments):
        c = jnp.ones((seq, LANES), F32)
        s1 = jnp.zeros((seq, LANES), F32)
        s2 = jnp.zeros((seq, LANES), F32)
        for lo, half, theta, p in segments:
            first = (d >= lo) & (d < lo + half)
            second = (d >= lo + half) & (d < lo + 2 * half)
            idx = np.where(first, d - lo, np.where(second, d - lo - half, 0))
            freqs = theta ** (-jnp.arange(half, dtype=F32) / half)
            ang = p[:, None] * freqs[idx][None, :]
            cs, sn = jnp.cos(ang), jnp.sin(ang)
            rot = jnp.asarray(first | second)[None, :]
            c = jnp.where(rot, cs, c)
            s1 = jnp.where(jnp.asarray(first)[None, :], -sn, s1)
            s2 = jnp.where(jnp.asarray(second)[None, :], sn, s2)
        return c, s1, s2

    ta = build([(0, A_ROT_DIMS // 2, A_ROPE_THETA, pos)])
    q = HEAD_DIM // 4
    tb = build([(0, q, B_ROPE_THETA, row), (2 * q, q, B_ROPE_THETA, col)])
    return ta + tb


def _in_proj_kernel(x_ref, sh_ref, sc_ref, g_ref, w_ref, bd_ref,
                    gqa_ref, gka_ref, gqb_ref, gkb_ref,
                    ca_ref, s1a_ref, s2a_ref, cb_ref, s1b_ref, s2b_ref,
                    za_ref, qb_ref, kb_ref, vb_ref, ga_ref, gb_ref):
    x = x_ref[0]
    ms = jnp.mean(x * x, axis=-1, keepdims=True)
    h = x * lax.rsqrt(ms + NORM_EPS) * g_ref[...]
    h = h * (1.0 + sc_ref[0]) + sh_ref[0]
    hb = h.astype(BF16)

    def seg(lo, width):
        return jnp.dot(hb, w_ref[:, lo:lo + width], preferred_element_type=F32)

    def qk_norm(z, gain_ref):
        width = z.shape[-1]
        bdw = min(width, bd_ref.shape[0])
        sq = (z * z).astype(BF16)
        parts = [jnp.dot(sq[:, lo:lo + bdw], bd_ref[:bdw, :bdw], preferred_element_type=F32)
                 for lo in range(0, width, bdw)]
        ss = parts[0] if len(parts) == 1 else jnp.concatenate(parts, axis=-1)
        return z * lax.rsqrt(ss * (1.0 / HEAD_DIM) + NORM_EPS) * gain_ref[...]

    def tile(t, width):
        reps = width // LANES
        return t if reps == 1 else jnp.concatenate([t] * reps, axis=-1)

    def rope(z, c_ref, s1_ref, s2_ref, half):
        width = z.shape[-1]
        up = pltpu.roll(z, width - half, 1)
        dn = pltpu.roll(z, half, 1)
        return (z * tile(c_ref[...], width) + up * tile(s1_ref[...], width)
                + dn * tile(s2_ref[...], width))

    ha = A_ROT_DIMS // 2
    hq = HEAD_DIM // 4
    o = 0
    qa = rope(qk_norm(seg(o, A_WIDTH), gqa_ref), ca_ref, s1a_ref, s2a_ref, ha)
    za_ref[0, :, 0:A_WIDTH] = qa
    o += A_WIDTH
    ka = rope(qk_norm(seg(o, A_WIDTH), gka_ref), ca_ref, s1a_ref, s2a_ref, ha)
    za_ref[0, :, A_WIDTH:2 * A_WIDTH] = ka
    o += A_WIDTH
    za_ref[0, :, 2 * A_WIDTH:3 * A_WIDTH] = seg(o, A_WIDTH)
    o += A_WIDTH
    qb = rope(qk_norm(seg(o, B_Q_WIDTH), gqb_ref), cb_ref, s1b_ref, s2b_ref, hq)
    qb_ref[0] = qb.astype(BF16)
    o += B_Q_WIDTH
    kb = rope(qk_norm(seg(o, B_KV_WIDTH), gkb_ref), cb_ref, s1b_ref, s2b_ref, hq)
    kb_ref[0] = kb.astype(BF16)
    o += B_KV_WIDTH
    vb_ref[0] = seg(o, B_KV_WIDTH).astype(BF16)
    o += B_KV_WIDTH
    ga_ref[0] = jax.nn.sigmoid(seg(o, D_MODEL)).astype(BF16)
    o += D_MODEL
    gb_ref[0] = jax.nn.sigmoid(seg(o, D_MODEL)).astype(BF16)


def _in_proj(x, sh, sc, g, w_in, bd, gains, tables, tm):
    bsz, seq, d = x.shape
    n_in = w_in.shape[1]
    nst = seq // tm
    row = lambda st, b: (b, st, 0)
    per_b = lambda st, b: (b, 0, 0)
    const2 = lambda st, b: (0, 0)
    tab = pl.BlockSpec((tm, LANES), lambda st, b: (st, 0))
    return pl.pallas_call(
        _in_proj_kernel,
        out_shape=(
            jax.ShapeDtypeStruct((bsz, seq, 3 * A_WIDTH), F32),
            jax.ShapeDtypeStruct((bsz, seq, B_Q_WIDTH), BF16),
            jax.ShapeDtypeStruct((bsz, seq, B_KV_WIDTH), BF16),
            jax.ShapeDtypeStruct((bsz, seq, B_KV_WIDTH), BF16),
            jax.ShapeDtypeStruct((bsz, seq, d), BF16),
            jax.ShapeDtypeStruct((bsz, seq, d), BF16),
        ),
        grid=(nst, bsz),
        in_specs=[
            pl.BlockSpec((1, tm, d), row),
            pl.BlockSpec((1, 1, d), per_b),
            pl.BlockSpec((1, 1, d), per_b),
            pl.BlockSpec((1, d), const2),
            pl.BlockSpec((d, n_in), const2),
            pl.BlockSpec(bd.shape, const2),
            pl.BlockSpec((1, A_WIDTH), const2),
            pl.BlockSpec((1, A_WIDTH), const2),
            pl.BlockSpec((1, B_Q_WIDTH), const2),
            pl.BlockSpec((1, B_KV_WIDTH), const2),
            tab, tab, tab, tab, tab, tab,
        ],
        out_specs=(
            pl.BlockSpec((1, tm, 3 * A_WIDTH), row),
            pl.BlockSpec((1, tm, B_Q_WIDTH), row),
            pl.BlockSpec((1, tm, B_KV_WIDTH), row),
            pl.BlockSpec((1, tm, B_KV_WIDTH), row),
            pl.BlockSpec((1, tm, d), row),
            pl.BlockSpec((1, tm, d), row),
        ),
        compiler_params=_cparams(("arbitrary", "arbitrary")),
        name="in_proj",
    )(x, sh, sc, g, w_in, bd, *gains, *tables)


_QB = 128
_UNROLL = 8


def _dilated_kernel(q_ref, k_ref, v_ref, o_ref, m_scr, l_scr, acc_scr, bias_scr):
    seq = q_ref.shape[1]
    lane = lax.broadcasted_iota(jnp.int32, (1, LANES), 1)
    head0 = lane < HEAD_DIM
    radius = A_PATTERNS[0][0] // (2 * A_PATTERNS[0][1])
    assert all(w // (2 * d) == radius for w, d in A_PATTERNS)

    @pl.when((pl.program_id(0) == 0) & (pl.program_id(1) == 0))
    def _():
        qrow = lax.broadcasted_iota(jnp.int32, (2 * _QB, 2 * _QB), 0) & (_QB - 1)
        kcol = lax.broadcasted_iota(jnp.int32, (2 * _QB, 2 * _QB), 1)
        for i in range(3):
            bias_scr[i] = jnp.where(jnp.abs(kcol - qrow - radius * i) <= radius, 0.0, -jnp.inf)

    def merge(t):
        return jnp.where(head0, t[:_QB], t[_QB:])

    patterns = sorted(A_PATTERNS, key=lambda wd: -wd[1])
    assert patterns[-1][1] == 1 and seq % (patterns[0][1] * _QB) == 0
    for pi, (window, dil) in enumerate(patterns):
        sub_len = seq // dil
        kw = min(2 * _QB, sub_len)
        nqb = sub_len // _QB
        nblk = dil * nqb

        def wide(t, kw=kw):
            return t if kw == LANES else jnp.concatenate([t] * (kw // LANES), axis=-1)

        def load(blk, pi=pi, dil=dil, sub_len=sub_len, kw=kw, nqb=nqb):
            r = blk // nqb
            i0q = (blk % nqb) * _QB
            i0k = jnp.clip(i0q - (kw - _QB) // 2, 0, sub_len - kw)
            if dil == 1:
                qrows = pl.ds(pl.multiple_of(i0q, _QB), _QB)
                krows = pl.ds(pl.multiple_of(i0k, SUBLANES), kw)
            else:
                qrows = pl.ds(r + dil * i0q, _QB, stride=dil)
                krows = pl.ds(r + dil * i0k, kw, stride=dil)
            q = q_ref[0, qrows, :]
            kk = k_ref[0, krows, :].astype(BF16)
            vv = v_ref[0, krows, :].astype(BF16)
            old = None
            if pi > 0:
                old = (jnp.concatenate([m_scr[0, qrows, :], m_scr[1, qrows, :]], axis=0),
                       l_scr[qrows, :], acc_scr[qrows, :])
            return qrows, (i0q - i0k) // radius, q, kk, vv, old

        def compute(mask_id, q, kk, vv, old, kw=kw, wide=wide):
            bias = bias_scr[0, :, 0:LANES] if kw == LANES else bias_scr[mask_id]
            q2 = jnp.concatenate([jnp.where(head0, q, 0.0), jnp.where(head0, 0.0, q)],
                                 axis=0).astype(BF16)
            s = lax.dot_general(q2, kk, (((1,), (1,)), ((), ())),
                                preferred_element_type=F32) + bias
            mb = jnp.broadcast_to(jnp.max(s, axis=-1, keepdims=True), (2 * _QB, LANES))
            m_new = mb if old is None else jnp.maximum(old[0], mb)
            p = jnp.exp2((s - wide(m_new)).astype(BF16))
            v_aug = jnp.concatenate([vv, jnp.ones_like(vv)], axis=-1)
            oa = jnp.dot(p, v_aug, preferred_element_type=F32)
            pv = merge(oa[:, :LANES])
            psum = merge(oa[:, LANES:])
            if old is None:
                return m_new, psum, pv
            alpha = jnp.exp2(merge(old[0]) - merge(m_new))
            return m_new, alpha * old[1] + psum, alpha * old[2] + pv

        def group(it, carry, load=load, compute=compute, last=pi == len(patterns) - 1):
            loaded = [load(it * _UNROLL + u) for u in range(_UNROLL)]
            results = [compute(*ld[1:]) for ld in loaded]
            for ld, (m_new, l_new, acc_new) in zip(loaded, results):
                qrows = ld[0]
                if last:
                    o_ref[0, qrows, :] = (acc_new / l_new).astype(o_ref.dtype)
                else:
                    m_scr[0, qrows, :] = m_new[:_QB]
                    m_scr[1, qrows, :] = m_new[_QB:]
                    l_scr[qrows, :] = l_new
                    acc_scr[qrows, :] = acc_new
            return carry

        lax.fori_loop(0, nblk // _UNROLL, group, 0)


def _dilated(za):
    bsz, seq, _ = za.shape
    nhp = A_WIDTH // LANES
    blk = (1, seq, LANES)
    return pl.pallas_call(
        _dilated_kernel,
        out_shape=jax.ShapeDtypeStruct((bsz, seq, A_WIDTH), BF16),
        grid=(bsz, nhp),
        in_specs=[
            pl.BlockSpec(blk, lambda b, j: (b, 0, j)),
            pl.BlockSpec(blk, lambda b, j: (b, 0, nhp + j)),
            pl.BlockSpec(blk, lambda b, j: (b, 0, 2 * nhp + j)),
        ],
        out_specs=pl.BlockSpec(blk, lambda b, j: (b, 0, j)),
        scratch_shapes=[
            pltpu.VMEM((2, seq, LANES), F32),
            pltpu.VMEM((seq, LANES), F32),
            pltpu.VMEM((seq, LANES), F32),
            pltpu.VMEM((3, 2 * _QB, 2 * _QB), F32),
        ],
        compiler_params=_cparams(("arbitrary", "arbitrary")),
        name="dilated_attn",
    )(za, za, za)


def _gqa_kernel(q_ref, k_ref, v_ref, o_ref):
    tq = q_ref.shape[1]
    lane = lax.broadcasted_iota(jnp.int32, (1, LANES), 1)
    head0 = lane < HEAD_DIM
    kk = k_ref[0]
    zero = jnp.zeros((), BF16)
    v_aug = jnp.concatenate([v_ref[0], jnp.ones_like(v_ref[0])], axis=-1)
    scores = []
    for j in range(B_Q_WIDTH // LANES):
        qj = q_ref[0, :, j * LANES:(j + 1) * LANES]
        q2 = jnp.concatenate([jnp.where(head0, qj, zero), jnp.where(head0, zero, qj)], axis=0)
        scores.append(lax.dot_general(q2, kk, (((1,), (1,)), ((), ())),
                                      preferred_element_type=F32))
    for j, s in enumerate(scores):
        m = jnp.max(s, axis=-1, keepdims=True)
        pb = jnp.exp2((s - m).astype(BF16))
        oa = jnp.dot(pb, v_aug, preferred_element_type=F32)
        pv = oa[:, :LANES] / oa[:, LANES:]
        o_ref[0, :, j * LANES:(j + 1) * LANES] = jnp.where(
            head0, pv[:tq], pv[tq:]).astype(o_ref.dtype)


def _gqa(qb, kb, vb, tq):
    bsz, seq, _ = qb.shape
    kv_spec = pl.BlockSpec((1, seq, B_KV_WIDTH), lambda b, i: (b, 0, 0))
    return pl.pallas_call(
        _gqa_kernel,
        out_shape=jax.ShapeDtypeStruct((bsz, seq, B_Q_WIDTH), BF16),
        grid=(bsz, seq // tq),
        in_specs=[pl.BlockSpec((1, tq, B_Q_WIDTH), lambda b, i: (b, i, 0)), kv_spec, kv_spec],
        out_specs=pl.BlockSpec((1, tq, B_Q_WIDTH), lambda b, i: (b, i, 0)),
        compiler_params=_cparams(("arbitrary", "arbitrary")),
        name="gqa_attn",
    )(qb, kb, vb)


def _merge_kernel(x_ref, ya_ref, yb_ref, ga_ref, gb_ref, wpa_ref, wpb_ref, wo_ref,
                  gt_ref, g_ref, sh_ref, sc_ref, wr_ref,
                  xn_ref, hrow_ref, e_ref, w_ref):
    tm = x_ref.shape[0]
    pa = jnp.dot(ya_ref[...], wpa_ref[...], preferred_element_type=F32)
    pb = jnp.dot(yb_ref[...], wpb_ref[...], preferred_element_type=F32)
    merged = ga_ref[...].astype(F32) * pa + gb_ref[...].astype(F32) * pb
    out = jnp.dot(merged.astype(BF16), wo_ref[...], preferred_element_type=F32)
    xn = x_ref[...] + gt_ref[0] * out
    xn_ref[...] = xn

    ms = jnp.mean(xn * xn, axis=-1, keepdims=True)
    h = xn * lax.rsqrt(ms + NORM_EPS) * g_ref[...]
    h = h * (1.0 + sc_ref[0]) + sh_ref[0]
    for c in range(ROW_CHUNKS):
        hrow_ref[pl.ds(c, tm, stride=ROW_CHUNKS), :] = h[:, c * LANES:(c + 1) * LANES]

    logits = lax.dot_general(wr_ref[...], h, (((1,), (1,)), ((), ())),
                             preferred_element_type=F32,
                             precision=lax.Precision.HIGHEST)
    mx = jnp.max(logits, axis=0, keepdims=True)
    ex = jnp.exp(logits - mx)
    probs = ex / jnp.sum(ex, axis=0, keepdims=True)
    pg = probs.reshape(N_GROUPS, EXPERTS_PER_GROUP, tm)
    sub = lax.broadcasted_iota(jnp.int32, pg.shape, 1).astype(F32)
    m1 = jnp.max(pg, axis=1, keepdims=True)
    i1 = jnp.min(jnp.where(pg == m1, sub, float(EXPERTS_PER_GROUP)), axis=1, keepdims=True)
    pg2 = jnp.where(sub == i1, -1.0, pg)
    m2 = jnp.max(pg2, axis=1, keepdims=True)
    i2 = jnp.min(jnp.where(pg2 == m2, sub, float(EXPERTS_PER_GROUP)), axis=1, keepdims=True)
    score = m1 + m2
    gid = lax.broadcasted_iota(jnp.int32, score.shape, 0).astype(F32)
    best = jnp.max(score, axis=0, keepdims=True)
    gsel = jnp.min(jnp.where(score == best, gid, float(N_GROUPS)), axis=0, keepdims=True)
    pick = gid == gsel
    w0 = jnp.sum(jnp.where(pick, m1, 0.0), axis=0)
    w1 = jnp.sum(jnp.where(pick, m2, 0.0), axis=0)
    j0 = jnp.sum(jnp.where(pick, i1, 0.0), axis=0)
    j1 = jnp.sum(jnp.where(pick, i2, 0.0), axis=0)
    base = gsel[0] * float(EXPERTS_PER_GROUP)
    tot = w0 + w1
    e_ref[0, 0:1, :] = (base + j0).astype(jnp.int32)
    e_ref[0, 1:2, :] = (base + j1).astype(jnp.int32)
    w_ref[0, 0:1, :] = w0 / tot
    w_ref[0, 1:2, :] = w1 / tot


def _merge(x, ya, yb, ga, gb, wpa, wpb, wo, gt, g, sh, sc, wr_t, tm, tiles_per_seq):
    t, d = x.shape
    nt = t // tm
    row = lambda i: (i, 0)
    per_b = lambda i: (i // tiles_per_seq, 0, 0)
    const2 = lambda i: (0, 0)
    return pl.pallas_call(
        _merge_kernel,
        out_shape=(
            jax.ShapeDtypeStruct((t, d), F32),
            jax.ShapeDtypeStruct((t * ROW_CHUNKS, LANES), F32),
            jax.ShapeDtypeStruct((nt, TOP_K, tm), jnp.int32),
            jax.ShapeDtypeStruct((nt, TOP_K, tm), F32),
        ),
        grid=(nt,),
        in_specs=[
            pl.BlockSpec((tm, d), row),
            pl.BlockSpec((tm, A_WIDTH), row),
            pl.BlockSpec((tm, B_Q_WIDTH), row),
            pl.BlockSpec((tm, d), row),
            pl.BlockSpec((tm, d), row),
            pl.BlockSpec((A_WIDTH, d), const2),
            pl.BlockSpec((B_Q_WIDTH, d), const2),
            pl.BlockSpec((d, d), const2),
            pl.BlockSpec((1, 1, d), per_b),
            pl.BlockSpec((1, d), const2),
            pl.BlockSpec((1, 1, d), per_b),
            pl.BlockSpec((1, 1, d), per_b),
            pl.BlockSpec((N_EXPERTS, d), const2),
        ],
        out_specs=(
            pl.BlockSpec((tm, d), row),
            pl.BlockSpec((tm * ROW_CHUNKS, LANES), row),
            pl.BlockSpec((1, TOP_K, tm), lambda i: (i, 0, 0)),
            pl.BlockSpec((1, TOP_K, tm), lambda i: (i, 0, 0)),
        ),
        compiler_params=_cparams(("arbitrary",)),
        name="merge_router",
    )(x, ya, yb, ga, gb, wpa, wpb, wo, gt, g, sh, sc, wr_t)


_ROW_UNROLL = 8
_BLOCKS_PER_STEP = 2
_FFN_ROWS = (64, MOE_BLOCK)


def _expert_kernel(blk_e_ref, blk_cnt_ref, blk_base_ref, tok_ref, wrow_ref, h_hbm, *refs):
    weight_refs = refs[:3 * _BLOCKS_PER_STEP]
    acc_hbm, xs, acc, gbuf, ybuf, sem = refs[3 * _BLOCKS_PER_STEP:]
    g = pl.program_id(0)
    j = pl.program_id(1)
    n_groups = pl.num_programs(0)
    n_steps = pl.num_programs(1)
    group_len = xs.shape[0]

    def group_rows(gi):
        return pl.ds(pl.multiple_of(gi * group_len, ROW_CHUNKS), group_len)

    def store_copy(gi):
        return pltpu.make_async_copy(acc.at[pl.ds(0, group_len), :],
                                     acc_hbm.at[group_rows(gi), :], sem.at[1])

    @pl.when(j == 0)
    def _():
        load = pltpu.make_async_copy(h_hbm.at[group_rows(g), :], xs, sem.at[0])
        load.start()

        @pl.when(g > 0)
        def _():
            store_copy(g - 1).wait()
        acc[...] = jnp.zeros_like(acc)

        @pl.when(g == 0)
        def _():
            gbuf[...] = jnp.zeros_like(gbuf)
        load.wait()

    for sub in range(_BLOCKS_PER_STEP):
        blk = (g * n_steps + j) * _BLOCKS_PER_STEP + sub
        pl.when(blk_cnt_ref[blk] > 0)(functools.partial(
            _expert_block, blk_cnt_ref[blk], blk_base_ref[blk], tok_ref, wrow_ref,
            *weight_refs[3 * sub:3 * sub + 3], xs, acc, gbuf, ybuf))

    @pl.when(j == n_steps - 1)
    def _():
        store_copy(g).start()

        @pl.when(g == n_groups - 1)
        def _():
            store_copy(g).wait()


def _expert_block(cnt, first, tok_ref, wrow_ref, wg_ref, wu_ref, wd_ref, xs, acc, gbuf, ybuf):
    group_len = xs.shape[0]

    def tile_at(row):
        return pl.ds(pl.multiple_of(row, ROW_CHUNKS), ROW_CHUNKS)

    nfull = cnt // _ROW_UNROLL
    tail = cnt - nfull * _ROW_UNROLL

    def gather(c, carry):
        base = c * _ROW_UNROLL
        for u in range(_ROW_UNROLL):
            gbuf[tile_at((base + u) * ROW_CHUNKS), :] = xs[
                tile_at(tok_ref[0, 0, first + base + u]), :]
        return carry
    lax.fori_loop(0, (cnt + _ROW_UNROLL - 1) // _ROW_UNROLL, gather, 0)

    def ffn(rows):
        xb = jnp.concatenate(
            [gbuf[pl.ds(c, rows, stride=ROW_CHUNKS), :] for c in range(ROW_CHUNKS)],
            axis=-1).astype(BF16)
        gate = jnp.dot(xb, wg_ref[0], preferred_element_type=F32)
        up = jnp.dot(xb, wu_ref[0], preferred_element_type=F32)
        hid = (gate * jax.nn.sigmoid(gate) * up).astype(BF16)
        y = jnp.dot(hid, wd_ref[0], preferred_element_type=F32)
        for c in range(ROW_CHUNKS):
            ybuf[pl.ds(c, rows, stride=ROW_CHUNKS), :] = y[:, c * LANES:(c + 1) * LANES]

    lo = 0
    for rows in _FFN_ROWS:
        pl.when((cnt > lo) & (cnt <= rows))(functools.partial(ffn, rows))
        lo = rows

    def scatter_chunk(base, n_real):
        new = []
        for u in range(_ROW_UNROLL):
            dst_row = tok_ref[0, 0, first + base + u]
            wgt = wrow_ref[0, 0, first + base + u]
            if n_real is not None:
                dst_row = jnp.where(u < n_real, dst_row, group_len)
                wgt = jnp.where(u < n_real, wgt, 0.0)
            dst = tile_at(dst_row)
            new.append((dst, acc[dst, :] + wgt * ybuf[tile_at((base + u) * ROW_CHUNKS), :]))
        for dst, val in new:
            acc[dst, :] = val

    def scatter_add(c, carry):
        scatter_chunk(c * _ROW_UNROLL, None)
        return carry
    lax.fori_loop(0, nfull, scatter_add, 0)

    @pl.when(tail > 0)
    def _():
        scatter_chunk(nfull * _ROW_UNROLL, tail)


def _experts(hrows, plan, wg, wu, wd, ts):
    blk_e, blk_cnt, blk_first, tok, wrow = plan
    n_groups, list_len = tok.shape[0], tok.shape[2]
    n_steps = blk_e.shape[0] // (n_groups * _BLOCKS_PER_STEP)
    d = D_MODEL
    idx_blk = (1, 1, list_len)
    idx_map = lambda g, j, be, bc, bf: (g, 0, 0)
    weight_specs = []
    for sub in range(_BLOCKS_PER_STEP):
        w_map = lambda g, j, be, bc, bf, sub=sub: (
            be[(g * n_steps + j) * _BLOCKS_PER_STEP + sub], 0, 0)
        weight_specs += [pl.BlockSpec((1, d, D_FF_EXPERT), w_map),
                         pl.BlockSpec((1, d, D_FF_EXPERT), w_map),
                         pl.BlockSpec((1, D_FF_EXPERT, d), w_map)]
    grid_spec = pltpu.PrefetchScalarGridSpec(
        num_scalar_prefetch=3,
        grid=(n_groups, n_steps),
        in_specs=[
            pl.BlockSpec(idx_blk, idx_map, memory_space=pltpu.SMEM),
            pl.BlockSpec(idx_blk, idx_map, memory_space=pltpu.SMEM),
            pl.BlockSpec(memory_space=pl.ANY),
        ] + weight_specs,
        out_specs=pl.BlockSpec(memory_space=pl.ANY),
        scratch_shapes=[
            pltpu.VMEM((ts * ROW_CHUNKS, LANES), F32),
            pltpu.VMEM(((ts + 1) * ROW_CHUNKS, LANES), F32),
            pltpu.VMEM((MOE_BLOCK * ROW_CHUNKS, LANES), F32),
            pltpu.VMEM((MOE_BLOCK * ROW_CHUNKS, LANES), F32),
            pltpu.SemaphoreType.DMA((2,)),
        ],
    )
    return pl.pallas_call(
        _expert_kernel,
        out_shape=jax.ShapeDtypeStruct(hrows.shape, F32),
        grid_spec=grid_spec,
        compiler_params=_cparams(("arbitrary", "arbitrary")),
        name="experts",
    )(blk_e, blk_cnt, blk_first, tok, wrow, hrows, *([wg, wu, wd] * _BLOCKS_PER_STEP))


def _combine_kernel(x_ref, y_ref, gt_ref, o_ref):
    tm = x_ref.shape[0]
    gt = gt_ref[0]
    for c in range(ROW_CHUNKS):
        cols = slice(c * LANES, (c + 1) * LANES)
        o_ref[:, cols] = x_ref[:, cols] + gt[:, cols] * y_ref[pl.ds(c, tm, stride=ROW_CHUNKS), :]


def _combine(xn, yrows, gt, tm, tiles_per_seq):
    t, d = xn.shape
    nt = t // tm
    return pl.pallas_call(
        _combine_kernel,
        out_shape=jax.ShapeDtypeStruct((t, d), F32),
        grid=(nt,),
        in_specs=[
            pl.BlockSpec((tm, d), lambda i: (i, 0)),
            pl.BlockSpec((tm * ROW_CHUNKS, LANES), lambda i: (i, 0)),
            pl.BlockSpec((1, 1, d), lambda i: (i // tiles_per_seq, 0, 0)),
        ],
        out_specs=pl.BlockSpec((tm, d), lambda i: (i, 0)),
        compiler_params=_cparams(("arbitrary",)),
        name="moe_combine",
    )(xn, yrows, gt)


def _dispatch_plan(e_sel, w_sel, ts):
    t = e_sel.shape[0]
    n_groups = t // ts
    na = ts * TOP_K
    nbs = na // MOE_BLOCK + N_EXPERTS
    nbs = -(-nbs // _BLOCKS_PER_STEP) * _BLOCKS_PER_STEP
    e_flat = e_sel.reshape(n_groups, ts, TOP_K).transpose(0, 2, 1).reshape(n_groups, na)
    w_flat = w_sel.reshape(n_groups, ts, TOP_K).transpose(0, 2, 1).reshape(n_groups, na)
    a_ids = jnp.broadcast_to(jnp.arange(na, dtype=jnp.int32)[None, :], (n_groups, na))
    _, a_sorted, w_sorted = lax.sort((e_flat, a_ids, w_flat), dimension=1, num_keys=1)
    experts = jnp.arange(N_EXPERTS, dtype=jnp.int32)
    counts = jnp.sum((e_flat[:, :, None] == experts[None, None, :]).astype(jnp.int32), axis=1)
    padded = (counts + MOE_BLOCK - 1) // MOE_BLOCK * MOE_BLOCK
    pad_end = jnp.cumsum(padded, axis=1)
    pad_start = pad_end - padded
    start = jnp.cumsum(counts, axis=1) - counts
    row0 = jnp.arange(nbs, dtype=jnp.int32) * MOE_BLOCK
    blk_e = jnp.minimum(
        jnp.sum((row0[None, :, None] >= pad_end[:, None, :]).astype(jnp.int32), axis=2),
        N_EXPERTS - 1)
    is_e = blk_e[:, :, None] == experts[None, None, :]
    take = lambda table: jnp.sum(jnp.where(is_e, table[:, None, :], 0), axis=2)
    off = row0[None, :] - take(pad_start)
    blk_cnt = jnp.clip(take(counts) - off, 0, MOE_BLOCK)
    blk_first = jnp.clip(take(start) + off, 0, na - 1)
    flat = lambda v: v.astype(jnp.int32).reshape(n_groups * nbs)
    filler = jnp.zeros((n_groups, LANES), jnp.int32)
    tok_rows = jnp.concatenate([(a_sorted % ts) * ROW_CHUNKS, filler], axis=1).astype(jnp.int32)
    w_list = jnp.concatenate([w_sorted.astype(F32), filler.astype(F32)], axis=1)
    return (flat(blk_e), flat(blk_cnt), flat(blk_first),
            tok_rows.reshape(n_groups, 1, na + LANES), w_list.reshape(n_groups, 1, na + LANES))


def kernel(x, c, w_ada, b_ada, g_mix, w_in, qn_a, kn_a, qn_b, kn_b, w_pa, w_pb, w_out,
           g_ffn, w_router, w_gate, w_up, w_down):
    bsz, seq, d = x.shape
    depth = w_ada.shape[0]
    t = bsz * seq
    tm = 512
    tiles_per_seq = seq // tm
    ts = min(4096, t)

    mod = _adaln(c, w_ada, b_ada)
    tables = _rope_tables(seq)
    bd_heads = 2 * LANES // HEAD_DIM
    bd = jnp.asarray(np.kron(np.eye(bd_heads), np.ones((HEAD_DIM, HEAD_DIM))), BF16)

    g4 = B_Q_HEADS // B_KV_HEADS
    head_perm = np.concatenate(
        [np.r_[np.arange(j * HEAD_DIM, (j + 1) * HEAD_DIM),
               np.arange((j + g4) * HEAD_DIM, (j + g4 + 1) * HEAD_DIM)] for j in range(g4)])
    qb_lo = 3 * A_WIDTH
    col_perm = np.arange(w_in.shape[2])
    col_perm[qb_lo:qb_lo + B_Q_WIDTH] = qb_lo + head_perm
    scale = HEAD_DIM ** -0.5 * float(np.log2(np.e))
    wr_t = w_router.T

    xf = x
    for l in range(depth):
        sh1, sc1, gt1, sh2, sc2, gt2 = [
            mod[l, :, i * d:(i + 1) * d].reshape(bsz, 1, d) for i in range(6)]
        w_in_l = w_in[l][:, col_perm].astype(BF16)
        gains = (
            jnp.tile(qn_a[l] * scale, A_HEADS).reshape(1, A_WIDTH),
            jnp.tile(kn_a[l], A_HEADS).reshape(1, A_WIDTH),
            jnp.tile(qn_b[l] * scale, B_Q_HEADS).reshape(1, B_Q_WIDTH),
            jnp.tile(kn_b[l], B_KV_HEADS).reshape(1, B_KV_WIDTH),
        )
        za, qb, kb, vb, ga, gb = _in_proj(xf.reshape(bsz, seq, d), sh1, sc1,
                                       g_mix[l].reshape(1, d), w_in_l, bd, gains, tables, tm)
        ya = _dilated(za)
        yb = _gqa(qb, kb, vb, 256)
        xn, hrows, e_sel, w_sel = _merge(
            xf.reshape(t, d), ya.reshape(t, A_WIDTH), yb.reshape(t, B_Q_WIDTH),
            ga.reshape(t, d), gb.reshape(t, d),
            w_pa[l].astype(BF16), w_pb[l][head_perm].astype(BF16), w_out[l].astype(BF16),
            gt1, g_ffn[l].reshape(1, d), sh2, sc2, wr_t, tm, tiles_per_seq)
        e_tok = e_sel.transpose(0, 2, 1).reshape(t, TOP_K)
        w_tok = w_sel.transpose(0, 2, 1).reshape(t, TOP_K)
        plan = _dispatch_plan(e_tok, w_tok, ts)
        yrows = _experts(hrows, plan, w_gate[l].astype(BF16), w_up[l].astype(BF16),
                         w_down[l].astype(BF16), ts)
        xf = _combine(xn, yrows, gt2, tm, tiles_per_seq)
    return xf.reshape(bsz, seq, d)
```

```python
import functools

import jax
import jax.numpy as jnp
import numpy as np
from jax import lax
from jax.experimental import pallas as pl
from jax.experimental.pallas import tpu as pltpu

D_MODEL = 1024
HEAD_DIM = 64
NORM_EPS = 1e-6
A_HEADS = 8
A_WIDTH = A_HEADS * HEAD_DIM
A_PATTERNS = ((128, 1), (512, 4), (2048, 16))
A_ROT_DIMS = HEAD_DIM // 4
A_ROPE_THETA = 500000.0
B_Q_HEADS = 8
B_KV_HEADS = 2
B_Q_WIDTH = B_Q_HEADS * HEAD_DIM
B_KV_WIDTH = B_KV_HEADS * HEAD_DIM
B_ROPE_THETA = 10000.0
GRID_W = 64
N_EXPERTS = 32
N_GROUPS = 4
EXPERTS_PER_GROUP = N_EXPERTS // N_GROUPS
TOP_K = 2
D_FF_EXPERT = D_MODEL // 2
MOE_BLOCK = 256

LANES = 128
SUBLANES = 8
ROW_CHUNKS = D_MODEL // LANES
VMEM_LIMIT = 56 * 1024 * 1024

F32 = jnp.float32
BF16 = jnp.bfloat16


def _cparams(sem):
    return pltpu.CompilerParams(dimension_semantics=sem, vmem_limit_bytes=VMEM_LIMIT)


def _adaln_kernel(c_ref, w_ref, b_ref, o_ref):
    c = c_ref[...]
    cond = c * jax.nn.sigmoid(c)
    o_ref[0] = jnp.dot(cond, w_ref[0], preferred_element_type=F32,
                       precision=lax.Precision.HIGHEST) + b_ref[0]


def _adaln(c, w_ada, b_ada):
    depth, d, n = w_ada.shape
    bsz = c.shape[0]
    tn = 1024
    return pl.pallas_call(
        _adaln_kernel,
        out_shape=jax.ShapeDtypeStruct((depth, bsz, n), F32),
        grid=(depth, n // tn),
        in_specs=[
            pl.BlockSpec((bsz, d), lambda l, j: (0, 0)),
            pl.BlockSpec((1, d, tn), lambda l, j: (l, 0, j)),
            pl.BlockSpec((1, 1, tn), lambda l, j: (l, 0, j)),
        ],
        out_specs=pl.BlockSpec((1, bsz, tn), lambda l, j: (l, 0, j)),
        compiler_params=_cparams(("arbitrary", "arbitrary")),
        name="adaln",
    )(c, w_ada, b_ada.reshape(depth, 1, n))


def _rope_tables(seq):
    pos = jnp.arange(seq, dtype=F32)
    row = jnp.floor(pos / GRID_W)
    col = pos - row * GRID_W
    d = np.arange(LANES) % HEAD_DIM

    def build(segments):
        c = jnp.ones((seq, LANES), F32)
        s1 = jnp.zeros((seq, LANES), F32)
        s2 = jnp.zeros((seq, LANES), F32)
        for lo, half, theta, p in segments:
            first = (d >= lo) & (d < lo + half)
            second = (d >= lo + half) & (d < lo + 2 * half)
            idx = np.where(first, d - lo, np.where(second, d - lo - half, 0))
            freqs = theta ** (-jnp.arange(half, dtype=F32) / half)
            ang = p[:, None] * freqs[idx][None, :]
            cs, sn = jnp.cos(ang), jnp.sin(ang)
            rot = jnp.asarray(first | second)[None, :]
            c = jnp.where(rot, cs, c)
            s1 = jnp.where(jnp.asarray(first)[None, :], -sn, s1)
            s2 = jnp.where(jnp.asarray(second)[None, :], sn, s2)
        return c, s1, s2

    ta = build([(0, A_ROT_DIMS // 2, A_ROPE_THETA, pos)])
    q = HEAD_DIM // 4
    tb = build([(0, q, B_ROPE_THETA, row), (2 * q, q, B_ROPE_THETA, col)])
    return ta + tb


def _in_proj_kernel(prev_moe, *refs):
    if prev_moe:
        xn_ref, y_ref, gtp_ref, *refs = refs
        xo_ref = refs.pop()
        tm = xn_ref.shape[1]
        moe = jnp.concatenate([y_ref[pl.ds(c, tm, stride=ROW_CHUNKS), :]
                               for c in range(ROW_CHUNKS)], axis=-1)
        x = xn_ref[0] + gtp_ref[0] * moe
        xo_ref[0] = x
    else:
        x_ref, *refs = refs
        x = x_ref[0]
    (sh_ref, sc_ref, g_ref, w_ref, bd_ref, gqa_ref, gka_ref, gqb_ref, gkb_ref,
     ca_ref, s1a_ref, s2a_ref, cb_ref, s1b_ref, s2b_ref,
     za_ref, qb_ref, kb_ref, vb_ref, ga_ref, gb_ref) = refs
    ms = jnp.mean(x * x, axis=-1, keepdims=True)
    h = x * lax.rsqrt(ms + NORM_EPS) * g_ref[...]
    h = h * (1.0 + sc_ref[0]) + sh_ref[0]
    hb = h.astype(BF16)

    def seg(lo, width):
        return jnp.dot(hb, w_ref[:, lo:lo + width], preferred_element_type=F32)

    def qk_norm(z, gain_ref):
        width = z.shape[-1]
        bdw = min(width, bd_ref.shape[0])
        sq = (z * z).astype(BF16)
        parts = [jnp.dot(sq[:, lo:lo + bdw], bd_ref[:bdw, :bdw], preferred_element_type=F32)
                 for lo in range(0, width, bdw)]
        ss = parts[0] if len(parts) == 1 else jnp.concatenate(parts, axis=-1)
        return z * lax.rsqrt(ss * (1.0 / HEAD_DIM) + NORM_EPS) * gain_ref[...]

    def tile(t, width):
        reps = width // LANES
        return t if reps == 1 else jnp.concatenate([t] * reps, axis=-1)

    def rope(z, c_ref, s1_ref, s2_ref, half):
        width = z.shape[-1]
        up = pltpu.roll(z, width - half, 1)
        dn = pltpu.roll(z, half, 1)
        return (z * tile(c_ref[...], width) + up * tile(s1_ref[...], width)
                + dn * tile(s2_ref[...], width))

    ha = A_ROT_DIMS // 2
    hq = HEAD_DIM // 4
    o = 0
    qa = rope(qk_norm(seg(o, A_WIDTH), gqa_ref), ca_ref, s1a_ref, s2a_ref, ha)
    za_ref[0, :, 0:A_WIDTH] = qa
    o += A_WIDTH
    ka = rope(qk_norm(seg(o, A_WIDTH), gka_ref), ca_ref, s1a_ref, s2a_ref, ha)
    za_ref[0, :, A_WIDTH:2 * A_WIDTH] = ka
    o += A_WIDTH
    za_ref[0, :, 2 * A_WIDTH:3 * A_WIDTH] = seg(o, A_WIDTH)
    o += A_WIDTH
    qb = rope(qk_norm(seg(o, B_Q_WIDTH), gqb_ref), cb_ref, s1b_ref, s2b_ref, hq)
    qb_ref[0] = qb.astype(BF16)
    o += B_Q_WIDTH
    kb = rope(qk_norm(seg(o, B_KV_WIDTH), gkb_ref), cb_ref, s1b_ref, s2b_ref, hq)
    kb_ref[0] = kb.astype(BF16)
    o += B_KV_WIDTH
    vb_ref[0] = seg(o, B_KV_WIDTH).astype(BF16)
    o += B_KV_WIDTH
    ga_ref[0] = jax.nn.sigmoid(seg(o, D_MODEL)).astype(BF16)
    o += D_MODEL
    gb_ref[0] = jax.nn.sigmoid(seg(o, D_MODEL)).astype(BF16)


def _in_proj(x, prev_moe, sh, sc, g, w_in, bd, gains, tables, tm):
    bsz, seq, d = x.shape
    n_in = w_in.shape[1]
    nst = seq // tm
    row = lambda st, b: (b, st, 0)
    per_b = lambda st, b: (b, 0, 0)
    const2 = lambda st, b: (0, 0)
    tab = pl.BlockSpec((tm, LANES), lambda st, b: (st, 0))
    x_specs = [pl.BlockSpec((1, tm, d), row)]
    x_args = [x]
    extra_shape, extra_spec = (), ()
    if prev_moe is not None:
        x_specs += [pl.BlockSpec((tm * ROW_CHUNKS, LANES), lambda st, b: (b * nst + st, 0)),
                    pl.BlockSpec((1, 1, d), per_b)]
        x_args += list(prev_moe)
        extra_shape = (jax.ShapeDtypeStruct((bsz, seq, d), F32),)
        extra_spec = (pl.BlockSpec((1, tm, d), row),)
    return pl.pallas_call(
        functools.partial(_in_proj_kernel, prev_moe is not None),
        out_shape=(
            jax.ShapeDtypeStruct((bsz, seq, 3 * A_WIDTH), F32),
            jax.ShapeDtypeStruct((bsz, seq, B_Q_WIDTH), BF16),
            jax.ShapeDtypeStruct((bsz, seq, B_KV_WIDTH), BF16),
            jax.ShapeDtypeStruct((bsz, seq, B_KV_WIDTH), BF16),
            jax.ShapeDtypeStruct((bsz, seq, d), BF16),
            jax.ShapeDtypeStruct((bsz, seq, d), BF16),
        ) + extra_shape,
        grid=(nst, bsz),
        in_specs=x_specs + [
            pl.BlockSpec((1, 1, d), per_b),
            pl.BlockSpec((1, 1, d), per_b),
            pl.BlockSpec((1, d), const2),
            pl.BlockSpec((d, n_in), const2),
            pl.BlockSpec(bd.shape, const2),
            pl.BlockSpec((1, A_WIDTH), const2),
            pl.BlockSpec((1, A_WIDTH), const2),
            pl.BlockSpec((1, B_Q_WIDTH), const2),
            pl.BlockSpec((1, B_KV_WIDTH), const2),
            tab, tab, tab, tab, tab, tab,
        ],
        out_specs=(
            pl.BlockSpec((1, tm, 3 * A_WIDTH), row),
            pl.BlockSpec((1, tm, B_Q_WIDTH), row),
            pl.BlockSpec((1, tm, B_KV_WIDTH), row),
            pl.BlockSpec((1, tm, B_KV_WIDTH), row),
            pl.BlockSpec((1, tm, d), row),
            pl.BlockSpec((1, tm, d), row),
        ) + extra_spec,
        compiler_params=_cparams(("arbitrary", "arbitrary")),
        name="in_proj",
    )(*x_args, sh, sc, g, w_in, bd, *gains, *tables)


_QB = 128
_UNROLL = 8


def _dilated_kernel(q_ref, k_ref, v_ref, o_ref, m_scr, l_scr, acc_scr, bias_scr):
    seq = q_ref.shape[1]
    lane = lax.broadcasted_iota(jnp.int32, (1, LANES), 1)
    head0 = lane < HEAD_DIM
    radius = A_PATTERNS[0][0] // (2 * A_PATTERNS[0][1])
    assert all(w // (2 * d) == radius for w, d in A_PATTERNS)

    @pl.when((pl.program_id(0) == 0) & (pl.program_id(1) == 0))
    def _():
        qrow = lax.broadcasted_iota(jnp.int32, (2 * _QB, 2 * _QB), 0) & (_QB - 1)
        kcol = lax.broadcasted_iota(jnp.int32, (2 * _QB, 2 * _QB), 1)
        for i in range(3):
            bias_scr[i] = jnp.where(jnp.abs(kcol - qrow - radius * i) <= radius, 0.0, -jnp.inf)

    def merge(t):
        return jnp.where(head0, t[:_QB], t[_QB:])

    patterns = sorted(A_PATTERNS, key=lambda wd: -wd[1])
    assert patterns[-1][1] == 1 and seq % (patterns[0][1] * _QB) == 0
    for pi, (window, dil) in enumerate(patterns):
        sub_len = seq // dil
        kw = min(2 * _QB, sub_len)
        nqb = sub_len // _QB
        nblk = dil * nqb

        def wide(t, kw=kw):
            return t if kw == LANES else jnp.concatenate([t] * (kw // LANES), axis=-1)

        def load(blk, pi=pi, dil=dil, sub_len=sub_len, kw=kw, nqb=nqb):
            r = blk // nqb
            i0q = (blk % nqb) * _QB
            i0k = jnp.clip(i0q - (kw - _QB) // 2, 0, sub_len - kw)
            if dil == 1:
                qrows = pl.ds(pl.multiple_of(i0q, _QB), _QB)
                krows = pl.ds(pl.multiple_of(i0k, SUBLANES), kw)
            else:
                qrows = pl.ds(r + dil * i0q, _QB, stride=dil)
                krows = pl.ds(r + dil * i0k, kw, stride=dil)
            q = q_ref[0, qrows, :]
            kk = k_ref[0, krows, :].astype(BF16)
            vv = v_ref[0, krows, :].astype(BF16)
            old = None
            if pi > 0:
                old = (jnp.concatenate([m_scr[0, qrows, :], m_scr[1, qrows, :]], axis=0),
                       l_scr[qrows, :], acc_scr[qrows, :])
            return qrows, (i0q - i0k) // radius, q, kk, vv, old

        def compute(mask_id, q, kk, vv, old, kw=kw, wide=wide):
            bias = bias_scr[0, :, 0:LANES] if kw == LANES else bias_scr[mask_id]
            q2 = jnp.concatenate([jnp.where(head0, q, 0.0), jnp.where(head0, 0.0, q)],
                                 axis=0).astype(BF16)
            s = lax.dot_general(q2, kk, (((1,), (1,)), ((), ())),
                                preferred_element_type=F32) + bias
            mb = jnp.broadcast_to(jnp.max(s, axis=-1, keepdims=True), (2 * _QB, LANES))
            m_new = mb if old is None else jnp.maximum(old[0], mb)
            p = jnp.exp2((s - wide(m_new)).astype(BF16))
            v_aug = jnp.concatenate([vv, jnp.ones_like(vv)], axis=-1)
            oa = jnp.dot(p, v_aug, preferred_element_type=F32)
            pv = merge(oa[:, :LANES])
            psum = merge(oa[:, LANES:])
            if old is None:
                return m_new, psum, pv
            alpha = jnp.exp2(merge(old[0]) - merge(m_new))
            return m_new, alpha * old[1] + psum, alpha * old[2] + pv

        def group(it, carry, load=load, compute=compute, last=pi == len(patterns) - 1):
            loaded = [load(it * _UNROLL + u) for u in range(_UNROLL)]
            results = [compute(*ld[1:]) for ld in loaded]
            for ld, (m_new, l_new, acc_new) in zip(loaded, results):
                qrows = ld[0]
                if last:
                    o_ref[0, qrows, :] = (acc_new / l_new).astype(o_ref.dtype)
                else:
                    m_scr[0, qrows, :] = m_new[:_QB]
                    m_scr[1, qrows, :] = m_new[_QB:]
                    l_scr[qrows, :] = l_new
                    acc_scr[qrows, :] = acc_new
            return carry

        lax.fori_loop(0, nblk // _UNROLL, group, 0)


def _dilated(za):
    bsz, seq, _ = za.shape
    nhp = A_WIDTH // LANES
    blk = (1, seq, LANES)
    return pl.pallas_call(
        _dilated_kernel,
        out_shape=jax.ShapeDtypeStruct((bsz, seq, A_WIDTH), BF16),
        grid=(bsz, nhp),
        in_specs=[
            pl.BlockSpec(blk, lambda b, j: (b, 0, j)),
            pl.BlockSpec(blk, lambda b, j: (b, 0, nhp + j)),
            pl.BlockSpec(blk, lambda b, j: (b, 0, 2 * nhp + j)),
        ],
        out_specs=pl.BlockSpec(blk, lambda b, j: (b, 0, j)),
        scratch_shapes=[
            pltpu.VMEM((2, seq, LANES), F32),
            pltpu.VMEM((seq, LANES), F32),
            pltpu.VMEM((seq, LANES), F32),
            pltpu.VMEM((3, 2 * _QB, 2 * _QB), F32),
        ],
        compiler_params=_cparams(("arbitrary", "arbitrary")),
        name="dilated_attn",
    )(za, za, za)


def _gqa_kernel(q_ref, k_ref, v_ref, o_ref):
    tq = q_ref.shape[1]
    lane = lax.broadcasted_iota(jnp.int32, (1, LANES), 1)
    head0 = lane < HEAD_DIM
    kk = k_ref[0]
    zero = jnp.zeros((), BF16)
    v_aug = jnp.concatenate([v_ref[0], jnp.ones_like(v_ref[0])], axis=-1)
    scores = []
    for j in range(B_Q_WIDTH // LANES):
        qj = q_ref[0, :, j * LANES:(j + 1) * LANES]
        q2 = jnp.concatenate([jnp.where(head0, qj, zero), jnp.where(head0, zero, qj)], axis=0)
        scores.append(lax.dot_general(q2, kk, (((1,), (1,)), ((), ())),
                                      preferred_element_type=F32))
    for j, s in enumerate(scores):
        m = jnp.max(s, axis=-1, keepdims=True)
        pb = jnp.exp2((s - m).astype(BF16))
        oa = jnp.dot(pb, v_aug, preferred_element_type=F32)
        pv = oa[:, :LANES] / oa[:, LANES:]
        o_ref[0, :, j * LANES:(j + 1) * LANES] = jnp.where(
            head0, pv[:tq], pv[tq:]).astype(o_ref.dtype)


def _gqa(qb, kb, vb, tq):
    bsz, seq, _ = qb.shape
    kv_spec = pl.BlockSpec((1, seq, B_KV_WIDTH), lambda b, i: (b, 0, 0))
    return pl.pallas_call(
        _gqa_kernel,
        out_shape=jax.ShapeDtypeStruct((bsz, seq, B_Q_WIDTH), BF16),
        grid=(bsz, seq // tq),
        in_specs=[pl.BlockSpec((1, tq, B_Q_WIDTH), lambda b, i: (b, i, 0)), kv_spec, kv_spec],
        out_specs=pl.BlockSpec((1, tq, B_Q_WIDTH), lambda b, i: (b, i, 0)),
        compiler_params=_cparams(("arbitrary", "arbitrary")),
        name="gqa_attn",
    )(qb, kb, vb)


def _merge_kernel(x_ref, ya_ref, yb_ref, ga_ref, gb_ref, wpa_ref, wpb_ref, wo_ref,
                  gt_ref, g_ref, sh_ref, sc_ref, wr_ref,
                  xn_ref, hrow_ref, e_ref, w_ref):
    tm = x_ref.shape[0]
    pa = jnp.dot(ya_ref[...], wpa_ref[...], preferred_element_type=F32)
    pb = jnp.dot(yb_ref[...], wpb_ref[...], preferred_element_type=F32)
    merged = ga_ref[...].astype(F32) * pa + gb_ref[...].astype(F32) * pb
    out = jnp.dot(merged.astype(BF16), wo_ref[...], preferred_element_type=F32)
    xn = x_ref[...] + gt_ref[0] * out
    xn_ref[...] = xn

    ms = jnp.mean(xn * xn, axis=-1, keepdims=True)
    h = xn * lax.rsqrt(ms + NORM_EPS) * g_ref[...]
    h = h * (1.0 + sc_ref[0]) + sh_ref[0]
    for c in range(ROW_CHUNKS):
        hrow_ref[pl.ds(c, tm, stride=ROW_CHUNKS), :] = h[:, c * LANES:(c + 1) * LANES]

    logits = lax.dot_general(wr_ref[...], h, (((1,), (1,)), ((), ())),
                             preferred_element_type=F32,
                             precision=lax.Precision.HIGHEST)
    mx = jnp.max(logits, axis=0, keepdims=True)
    ex = jnp.exp(logits - mx)
    probs = ex / jnp.sum(ex, axis=0, keepdims=True)
    pg = probs.reshape(N_GROUPS, EXPERTS_PER_GROUP, tm)
    sub = lax.broadcasted_iota(jnp.int32, pg.shape, 1).astype(F32)
    m1 = jnp.max(pg, axis=1, keepdims=True)
    i1 = jnp.min(jnp.where(pg == m1, sub, float(EXPERTS_PER_GROUP)), axis=1, keepdims=True)
    pg2 = jnp.where(sub == i1, -1.0, pg)
    m2 = jnp.max(pg2, axis=1, keepdims=True)
    i2 = jnp.min(jnp.where(pg2 == m2, sub, float(EXPERTS_PER_GROUP)), axis=1, keepdims=True)
    score = m1 + m2
    gid = lax.broadcasted_iota(jnp.int32, score.shape, 0).astype(F32)
    best = jnp.max(score, axis=0, keepdims=True)
    gsel = jnp.min(jnp.where(score == best, gid, float(N_GROUPS)), axis=0, keepdims=True)
    pick = gid == gsel
    w0 = jnp.sum(jnp.where(pick, m1, 0.0), axis=0)
    w1 = jnp.sum(jnp.where(pick, m2, 0.0), axis=0)
    j0 = jnp.sum(jnp.where(pick, i1, 0.0), axis=0)
    j1 = jnp.sum(jnp.where(pick, i2, 0.0), axis=0)
    base = gsel[0] * float(EXPERTS_PER_GROUP)
    tot = w0 + w1
    e_ref[0, 0:1, :] = (base + j0).astype(jnp.int32)
    e_ref[0, 1:2, :] = (base + j1).astype(jnp.int32)
    w_ref[0, 0:1, :] = w0 / tot
    w_ref[0, 1:2, :] = w1 / tot


def _merge(x, ya, yb, ga, gb, wpa, wpb, wo, gt, g, sh, sc, wr_t, tm, tiles_per_seq):
    t, d = x.shape
    nt = t // tm
    row = lambda i: (i, 0)
    per_b = lambda i: (i // tiles_per_seq, 0, 0)
    const2 = lambda i: (0, 0)
    return pl.pallas_call(
        _merge_kernel,
        out_shape=(
            jax.ShapeDtypeStruct((t, d), F32),
            jax.ShapeDtypeStruct((t * ROW_CHUNKS, LANES), F32),
            jax.ShapeDtypeStruct((nt, TOP_K, tm), jnp.int32),
            jax.ShapeDtypeStruct((nt, TOP_K, tm), F32),
        ),
        grid=(nt,),
        in_specs=[
            pl.BlockSpec((tm, d), row),
            pl.BlockSpec((tm, A_WIDTH), row),
            pl.BlockSpec((tm, B_Q_WIDTH), row),
            pl.BlockSpec((tm, d), row),
            pl.BlockSpec((tm, d), row),
            pl.BlockSpec((A_WIDTH, d), const2),
            pl.BlockSpec((B_Q_WIDTH, d), const2),
            pl.BlockSpec((d, d), const2),
            pl.BlockSpec((1, 1, d), per_b),
            pl.BlockSpec((1, d), const2),
            pl.BlockSpec((1, 1, d), per_b),
            pl.BlockSpec((1, 1, d), per_b),
            pl.BlockSpec((N_EXPERTS, d), const2),
        ],
        out_specs=(
            pl.BlockSpec((tm, d), row),
            pl.BlockSpec((tm * ROW_CHUNKS, LANES), row),
            pl.BlockSpec((1, TOP_K, tm), lambda i: (i, 0, 0)),
            pl.BlockSpec((1, TOP_K, tm), lambda i: (i, 0, 0)),
        ),
        compiler_params=_cparams(("arbitrary",)),
        name="merge_router",
    )(x, ya, yb, ga, gb, wpa, wpb, wo, gt, g, sh, sc, wr_t)


_ROW_UNROLL = 8
_BLOCKS_PER_STEP = 2
_FFN_ROWS = (64, MOE_BLOCK)


def _expert_kernel(blk_e_ref, blk_cnt_ref, blk_base_ref, tok_ref, wrow_ref, h_hbm, *refs):
    weight_refs = refs[:3 * _BLOCKS_PER_STEP]
    acc_hbm, xs, acc, gbuf, ybuf, sem = refs[3 * _BLOCKS_PER_STEP:]
    g = pl.program_id(0)
    j = pl.program_id(1)
    n_groups = pl.num_programs(0)
    n_steps = pl.num_programs(1)
    group_len = xs.shape[0]

    def group_rows(gi):
        return pl.ds(pl.multiple_of(gi * group_len, ROW_CHUNKS), group_len)

    def store_copy(gi):
        return pltpu.make_async_copy(acc.at[pl.ds(0, group_len), :],
                                     acc_hbm.at[group_rows(gi), :], sem.at[1])

    @pl.when(j == 0)
    def _():
        load = pltpu.make_async_copy(h_hbm.at[group_rows(g), :], xs, sem.at[0])
        load.start()

        @pl.when(g > 0)
        def _():
            store_copy(g - 1).wait()
        acc[...] = jnp.zeros_like(acc)

        @pl.when(g == 0)
        def _():
            gbuf[...] = jnp.zeros_like(gbuf)
        load.wait()

    for sub in range(_BLOCKS_PER_STEP):
        blk = (g * n_steps + j) * _BLOCKS_PER_STEP + sub
        pl.when(blk_cnt_ref[blk] > 0)(functools.partial(
            _expert_block, blk_cnt_ref[blk], blk_base_ref[blk], tok_ref, wrow_ref,
            *weight_refs[3 * sub:3 * sub + 3], xs, acc, gbuf, ybuf))

    @pl.when(j == n_steps - 1)
    def _():
        store_copy(g).start()

        @pl.when(g == n_groups - 1)
        def _():
            store_copy(g).wait()


def _expert_block(cnt, first, tok_ref, wrow_ref, wg_ref, wu_ref, wd_ref, xs, acc, gbuf, ybuf):
    group_len = xs.shape[0]

    def tile_at(row):
        return pl.ds(pl.multiple_of(row, ROW_CHUNKS), ROW_CHUNKS)

    nfull = cnt // _ROW_UNROLL
    tail = cnt - nfull * _ROW_UNROLL

    def gather(c, carry):
        base = c * _ROW_UNROLL
        for u in range(_ROW_UNROLL):
            gbuf[tile_at((base + u) * ROW_CHUNKS), :] = xs[
                tile_at(tok_ref[0, 0, first + base + u]), :]
        return carry
    lax.fori_loop(0, (cnt + _ROW_UNROLL - 1) // _ROW_UNROLL, gather, 0)

    def ffn(rows):
        xb = jnp.concatenate(
            [gbuf[pl.ds(c, rows, stride=ROW_CHUNKS), :] for c in range(ROW_CHUNKS)],
            axis=-1).astype(BF16)
        gate = jnp.dot(xb, wg_ref[0], preferred_element_type=F32)
        up = jnp.dot(xb, wu_ref[0], preferred_element_type=F32)
        hid = (gate * jax.nn.sigmoid(gate) * up).astype(BF16)
        y = jnp.dot(hid, wd_ref[0], preferred_element_type=F32)
        for c in range(ROW_CHUNKS):
            ybuf[pl.ds(c, rows, stride=ROW_CHUNKS), :] = y[:, c * LANES:(c + 1) * LANES]

    lo = 0
    for rows in _FFN_ROWS:
        pl.when((cnt > lo) & (cnt <= rows))(functools.partial(ffn, rows))
        lo = rows

    def scatter_chunk(base, n_real):
        new = []
        for u in range(_ROW_UNROLL):
            dst_row = tok_ref[0, 0, first + base + u]
            wgt = wrow_ref[0, 0, first + base + u]
            if n_real is not None:
                dst_row = jnp.where(u < n_real, dst_row, group_len)
                wgt = jnp.where(u < n_real, wgt, 0.0)
            dst = tile_at(dst_row)
            new.append((dst, acc[dst, :] + wgt * ybuf[tile_at((base + u) * ROW_CHUNKS), :]))
        for dst, val in new:
            acc[dst, :] = val

    def scatter_add(c, carry):
        scatter_chunk(c * _ROW_UNROLL, None)
        return carry
    lax.fori_loop(0, nfull, scatter_add, 0)

    @pl.when(tail > 0)
    def _():
        scatter_chunk(nfull * _ROW_UNROLL, tail)


def _experts(hrows, plan, wg, wu, wd, ts):
    blk_e, blk_cnt, blk_first, tok, wrow = plan
    n_groups, list_len = tok.shape[0], tok.shape[2]
    n_steps = blk_e.shape[0] // (n_groups * _BLOCKS_PER_STEP)
    d = D_MODEL
    idx_blk = (1, 1, list_len)
    idx_map = lambda g, j, be, bc, bf: (g, 0, 0)
    weight_specs = []
    for sub in range(_BLOCKS_PER_STEP):
        w_map = lambda g, j, be, bc, bf, sub=sub: (
            be[(g * n_steps + j) * _BLOCKS_PER_STEP + sub], 0, 0)
        weight_specs += [pl.BlockSpec((1, d, D_FF_EXPERT), w_map),
                         pl.BlockSpec((1, d, D_FF_EXPERT), w_map),
                         pl.BlockSpec((1, D_FF_EXPERT, d), w_map)]
    grid_spec = pltpu.PrefetchScalarGridSpec(
        num_scalar_prefetch=3,
        grid=(n_groups, n_steps),
        in_specs=[
            pl.BlockSpec(idx_blk, idx_map, memory_space=pltpu.SMEM),
            pl.BlockSpec(idx_blk, idx_map, memory_space=pltpu.SMEM),
            pl.BlockSpec(memory_space=pl.ANY),
        ] + weight_specs,
        out_specs=pl.BlockSpec(memory_space=pl.ANY),
        scratch_shapes=[
            pltpu.VMEM((ts * ROW_CHUNKS, LANES), F32),
            pltpu.VMEM(((ts + 1) * ROW_CHUNKS, LANES), F32),
            pltpu.VMEM((MOE_BLOCK * ROW_CHUNKS, LANES), F32),
            pltpu.VMEM((MOE_BLOCK * ROW_CHUNKS, LANES), F32),
            pltpu.SemaphoreType.DMA((2,)),
        ],
    )
    return pl.pallas_call(
        _expert_kernel,
        out_shape=jax.ShapeDtypeStruct(hrows.shape, F32),
        grid_spec=grid_spec,
        compiler_params=_cparams(("arbitrary", "arbitrary")),
        name="experts",
    )(blk_e, blk_cnt, blk_first, tok, wrow, hrows, *([wg, wu, wd] * _BLOCKS_PER_STEP))


def _combine_kernel(x_ref, y_ref, gt_ref, o_ref):
    tm = x_ref.shape[0]
    gt = gt_ref[0]
    for c in range(ROW_CHUNKS):
        cols = slice(c * LANES, (c + 1) * LANES)
        o_ref[:, cols] = x_ref[:, cols] + gt[:, cols] * y_ref[pl.ds(c, tm, stride=ROW_CHUNKS), :]


def _combine(xn, yrows, gt, tm, tiles_per_seq):
    t, d = xn.shape
    nt = t // tm
    return pl.pallas_call(
        _combine_kernel,
        out_shape=jax.ShapeDtypeStruct((t, d), F32),
        grid=(nt,),
        in_specs=[
            pl.BlockSpec((tm, d), lambda i: (i, 0)),
            pl.BlockSpec((tm * ROW_CHUNKS, LANES), lambda i: (i, 0)),
            pl.BlockSpec((1, 1, d), lambda i: (i // tiles_per_seq, 0, 0)),
        ],
        out_specs=pl.BlockSpec((tm, d), lambda i: (i, 0)),
        compiler_params=_cparams(("arbitrary",)),
        name="moe_combine",
    )(xn, yrows, gt)


def _dispatch_plan(e_sel, w_sel, ts):
    t = e_sel.shape[0]
    n_groups = t // ts
    na = ts * TOP_K
    nbs = na // MOE_BLOCK + N_EXPERTS
    nbs = -(-nbs // _BLOCKS_PER_STEP) * _BLOCKS_PER_STEP
    e_flat = e_sel.reshape(n_groups, ts, TOP_K).transpose(0, 2, 1).reshape(n_groups, na)
    w_flat = w_sel.reshape(n_groups, ts, TOP_K).transpose(0, 2, 1).reshape(n_groups, na)
    a_ids = jnp.broadcast_to(jnp.arange(na, dtype=jnp.int32)[None, :], (n_groups, na))
    _, a_sorted, w_sorted = lax.sort((e_flat, a_ids, w_flat), dimension=1, num_keys=1)
    experts = jnp.arange(N_EXPERTS, dtype=jnp.int32)
    counts = jnp.sum((e_flat[:, :, None] == experts[None, None, :]).astype(jnp.int32), axis=1)
    padded = (counts + MOE_BLOCK - 1) // MOE_BLOCK * MOE_BLOCK
    pad_end = jnp.cumsum(padded, axis=1)
    pad_start = pad_end - padded
    start = jnp.cumsum(counts, axis=1) - counts
    row0 = jnp.arange(nbs, dtype=jnp.int32) * MOE_BLOCK
    blk_e = jnp.minimum(
        jnp.sum((row0[None, :, None] >= pad_end[:, None, :]).astype(jnp.int32), axis=2),
        N_EXPERTS - 1)
    is_e = blk_e[:, :, None] == experts[None, None, :]
    take = lambda table: jnp.sum(jnp.where(is_e, table[:, None, :], 0), axis=2)
    off = row0[None, :] - take(pad_start)
    blk_cnt = jnp.clip(take(counts) - off, 0, MOE_BLOCK)
    blk_first = jnp.clip(take(start) + off, 0, na - 1)
    flat = lambda v: v.astype(jnp.int32).reshape(n_groups * nbs)
    filler = jnp.zeros((n_groups, LANES), jnp.int32)
    tok_rows = jnp.concatenate([(a_sorted % ts) * ROW_CHUNKS, filler], axis=1).astype(jnp.int32)
    w_list = jnp.concatenate([w_sorted.astype(F32), filler.astype(F32)], axis=1)
    return (flat(blk_e), flat(blk_cnt), flat(blk_first),
            tok_rows.reshape(n_groups, 1, na + LANES), w_list.reshape(n_groups, 1, na + LANES))


def _tile_sizes(bsz, seq):
    tm = min(512, seq)
    tm_merge = min(1024, seq)
    tq = min(256, seq)
    ts = min(4096, bsz * seq)
    assert seq % tm == 0 and seq % tm_merge == 0 and seq % tq == 0
    assert (bsz * seq) % ts == 0 and ts % tm_merge == 0
    return tm, tm_merge, tq, ts


def kernel(x, c, w_ada, b_ada, g_mix, w_in, qn_a, kn_a, qn_b, kn_b, w_pa, w_pb, w_out,
           g_ffn, w_router, w_gate, w_up, w_down):
    bsz, seq, d = x.shape
    depth = w_ada.shape[0]
    t = bsz * seq
    tm, tm_merge, tq, ts = _tile_sizes(bsz, seq)
    tiles_per_seq = seq // tm

    mod = _adaln(c, w_ada, b_ada)
    tables = _rope_tables(seq)
    bd_heads = 2 * LANES // HEAD_DIM
    bd = jnp.asarray(np.kron(np.eye(bd_heads), np.ones((HEAD_DIM, HEAD_DIM))), BF16)

    g4 = B_Q_HEADS // B_KV_HEADS
    head_perm = np.concatenate(
        [np.r_[np.arange(j * HEAD_DIM, (j + 1) * HEAD_DIM),
               np.arange((j + g4) * HEAD_DIM, (j + g4 + 1) * HEAD_DIM)] for j in range(g4)])
    qb_lo = 3 * A_WIDTH
    col_perm = np.arange(w_in.shape[2])
    col_perm[qb_lo:qb_lo + B_Q_WIDTH] = qb_lo + head_perm
    scale = HEAD_DIM ** -0.5 * float(np.log2(np.e))
    wr_t = w_router.T

    xf = x
    prev_moe = None
    for l in range(depth):
        sh1, sc1, gt1, sh2, sc2, gt2 = [
            mod[l, :, i * d:(i + 1) * d].reshape(bsz, 1, d) for i in range(6)]
        w_in_l = w_in[l][:, col_perm].astype(BF16)
        gains = (
            jnp.tile(qn_a[l] * scale, A_HEADS).reshape(1, A_WIDTH),
            jnp.tile(kn_a[l], A_HEADS).reshape(1, A_WIDTH),
            jnp.tile(qn_b[l] * scale, B_Q_HEADS).reshape(1, B_Q_WIDTH),
            jnp.tile(kn_b[l], B_KV_HEADS).reshape(1, B_KV_WIDTH),
        )
        outs = _in_proj(xf, prev_moe, sh1, sc1, g_mix[l].reshape(1, d), w_in_l, bd, gains,
                        tables, tm)
        za, qb, kb, vb, ga, gb = outs[:6]
        if prev_moe is not None:
            xf = outs[6]
        ya = _dilated(za)
        yb = _gqa(qb, kb, vb, tq)
        xn, hrows, e_sel, w_sel = _merge(
            xf.reshape(t, d), ya.reshape(t, A_WIDTH), yb.reshape(t, B_Q_WIDTH),
            ga.reshape(t, d), gb.reshape(t, d),
            w_pa[l].astype(BF16), w_pb[l][head_perm].astype(BF16), w_out[l].astype(BF16),
            gt1, g_ffn[l].reshape(1, d), sh2, sc2, wr_t, tm_merge, seq // tm_merge)
        e_tok = e_sel.transpose(0, 2, 1).reshape(t, TOP_K)
        w_tok = w_sel.transpose(0, 2, 1).reshape(t, TOP_K)
        plan = _dispatch_plan(e_tok, w_tok, ts)
        yrows = _experts(hrows, plan, w_gate[l].astype(BF16), w_up[l].astype(BF16),
                         w_down[l].astype(BF16), ts)
        xf = xn.reshape(bsz, seq, d)
        prev_moe = (yrows, gt2)
    out = _combine(xf.reshape(t, d), prev_moe[0], prev_moe[1], tm, tiles_per_seq)
    return out.reshape(bsz, seq, d)
```

```python
import functools

import jax
import jax.numpy as jnp
import numpy as np
from jax import lax
from jax.experimental import pallas as pl
from jax.experimental.pallas import tpu as pltpu

D_MODEL = 1024
HEAD_DIM = 64
NORM_EPS = 1e-6
A_HEADS = 8
A_WIDTH = A_HEADS * HEAD_DIM
A_PATTERNS = ((128, 1), (512, 4), (2048, 16))
A_ROT_DIMS = HEAD_DIM // 4
A_ROPE_THETA = 500000.0
B_Q_HEADS = 8
B_KV_HEADS = 2
B_Q_WIDTH = B_Q_HEADS * HEAD_DIM
B_KV_WIDTH = B_KV_HEADS * HEAD_DIM
B_ROPE_THETA = 10000.0
GRID_W = 64
N_EXPERTS = 32
N_GROUPS = 4
EXPERTS_PER_GROUP = N_EXPERTS // N_GROUPS
TOP_K = 2
D_FF_EXPERT = D_MODEL // 2
MOE_BLOCK = 256

LANES = 128
SUBLANES = 8
ROW_CHUNKS = D_MODEL // LANES
VMEM_LIMIT = 56 * 1024 * 1024

F32 = jnp.float32
BF16 = jnp.bfloat16


def _cparams(sem):
    return pltpu.CompilerParams(dimension_semantics=sem, vmem_limit_bytes=VMEM_LIMIT)


def _adaln_kernel(c_ref, w_ref, b_ref, o_ref):
    c = c_ref[...]
    cond = c * jax.nn.sigmoid(c)
    o_ref[0] = jnp.dot(cond, w_ref[0], preferred_element_type=F32,
                       precision=lax.Precision.HIGHEST) + b_ref[0]


def _adaln(c, w_ada, b_ada):
    depth, d, n = w_ada.shape
    bsz = c.shape[0]
    tn = 1024
    return pl.pallas_call(
        _adaln_kernel,
        out_shape=jax.ShapeDtypeStruct((depth, bsz, n), F32),
        grid=(depth, n // tn),
        in_specs=[
            pl.BlockSpec((bsz, d), lambda l, j: (0, 0)),
            pl.BlockSpec((1, d, tn), lambda l, j: (l, 0, j)),
            pl.BlockSpec((1, 1, tn), lambda l, j: (l, 0, j)),
        ],
        out_specs=pl.BlockSpec((1, bsz, tn), lambda l, j: (l, 0, j)),
        compiler_params=_cparams(("arbitrary", "arbitrary")),
        name="adaln",
    )(c, w_ada, b_ada.reshape(depth, 1, n))


def _rope_tables(seq):
    pos = jnp.arange(seq, dtype=F32)
    row = jnp.floor(pos / GRID_W)
    col = pos - row * GRID_W
    d = np.arange(LANES) % HEAD_DIM

    def build(segments):
        c = jnp.ones((seq, LANES), F32)
        s1 = jnp.zeros((seq, LANES), F32)
        s2 = jnp.zeros((seq, LANES), F32)
        for lo, half, theta, p in segments:
            first = (d >= lo) & (d < lo + half)
            second = (d >= lo + half) & (d < lo + 2 * half)
            idx = np.where(first, d - lo, np.where(second, d - lo - half, 0))
            freqs = theta ** (-jnp.arange(half, dtype=F32) / half)
            ang = p[:, None] * freqs[idx][None, :]
            cs, sn = jnp.cos(ang), jnp.sin(ang)
            rot = jnp.asarray(first | second)[None, :]
            c = jnp.where(rot, cs, c)
            s1 = jnp.where(jnp.asarray(first)[None, :], -sn, s1)
            s2 = jnp.where(jnp.asarray(second)[None, :], sn, s2)
        return c, s1, s2

    ta = build([(0, A_ROT_DIMS // 2, A_ROPE_THETA, pos)])
    q = HEAD_DIM // 4
    tb = build([(0, q, B_ROPE_THETA, row), (2 * q, q, B_ROPE_THETA, col)])
    return ta + tb


def _in_proj_kernel(prev_moe, *refs):
    if prev_moe:
        xn_ref, y_ref, gtp_ref, *refs = refs
        xo_ref = refs.pop()
        tm = xn_ref.shape[1]
        moe = jnp.concatenate([y_ref[pl.ds(c, tm, stride=ROW_CHUNKS), :]
                               for c in range(ROW_CHUNKS)], axis=-1)
        x = xn_ref[0] + gtp_ref[0] * moe
        xo_ref[0] = x
    else:
        x_ref, *refs = refs
        x = x_ref[0]
    (sh_ref, sc_ref, g_ref, w_ref, bd_ref, gqa_ref, gka_ref, gqb_ref, gkb_ref,
     ca_ref, s1a_ref, s2a_ref, cb_ref, s1b_ref, s2b_ref,
     za_ref, qb_ref, kb_ref, vb_ref, ga_ref, gb_ref) = refs
    ms = jnp.mean(x * x, axis=-1, keepdims=True)
    h = x * lax.rsqrt(ms + NORM_EPS) * g_ref[...]
    h = h * (1.0 + sc_ref[0]) + sh_ref[0]
    hb = h.astype(BF16)

    def seg(lo, width):
        return jnp.dot(hb, w_ref[:, lo:lo + width], preferred_element_type=F32)

    def qk_norm(z, gain_ref):
        width = z.shape[-1]
        bdw = min(width, bd_ref.shape[0])
        sq = (z * z).astype(BF16)
        parts = [jnp.dot(sq[:, lo:lo + bdw], bd_ref[:bdw, :bdw], preferred_element_type=F32)
                 for lo in range(0, width, bdw)]
        ss = parts[0] if len(parts) == 1 else jnp.concatenate(parts, axis=-1)
        return z * lax.rsqrt(ss * (1.0 / HEAD_DIM) + NORM_EPS) * gain_ref[...]

    def tile(t, width):
        reps = width // LANES
        return t if reps == 1 else jnp.concatenate([t] * reps, axis=-1)

    def rope(z, c_ref, s1_ref, s2_ref, half):
        width = z.shape[-1]
        up = pltpu.roll(z, width - half, 1)
        dn = pltpu.roll(z, half, 1)
        return (z * tile(c_ref[...], width) + up * tile(s1_ref[...], width)
                + dn * tile(s2_ref[...], width))

    ha = A_ROT_DIMS // 2
    hq = HEAD_DIM // 4
    o = 0
    qa = rope(qk_norm(seg(o, A_WIDTH), gqa_ref), ca_ref, s1a_ref, s2a_ref, ha)
    za_ref[0, :, 0:A_WIDTH] = qa
    o += A_WIDTH
    ka = rope(qk_norm(seg(o, A_WIDTH), gka_ref), ca_ref, s1a_ref, s2a_ref, ha)
    za_ref[0, :, A_WIDTH:2 * A_WIDTH] = ka
    o += A_WIDTH
    za_ref[0, :, 2 * A_WIDTH:3 * A_WIDTH] = seg(o, A_WIDTH)
    o += A_WIDTH
    qb = rope(qk_norm(seg(o, B_Q_WIDTH), gqb_ref), cb_ref, s1b_ref, s2b_ref, hq)
    qb_ref[0] = qb.astype(BF16)
    o += B_Q_WIDTH
    kb = rope(qk_norm(seg(o, B_KV_WIDTH), gkb_ref), cb_ref, s1b_ref, s2b_ref, hq)
    kb_ref[0] = kb.astype(BF16)
    o += B_KV_WIDTH
    vb_ref[0] = seg(o, B_KV_WIDTH).astype(BF16)
    o += B_KV_WIDTH
    ga_ref[0] = jax.nn.sigmoid(seg(o, D_MODEL)).astype(BF16)
    o += D_MODEL
    gb_ref[0] = jax.nn.sigmoid(seg(o, D_MODEL)).astype(BF16)


def _in_proj(x, prev_moe, sh, sc, g, w_in, bd, gains, tables, tm):
    bsz, seq, d = x.shape
    n_in = w_in.shape[1]
    nst = seq // tm
    row = lambda st, b: (b, st, 0)
    per_b = lambda st, b: (b, 0, 0)
    const2 = lambda st, b: (0, 0)
    tab = pl.BlockSpec((tm, LANES), lambda st, b: (st, 0))
    x_specs = [pl.BlockSpec((1, tm, d), row)]
    x_args = [x]
    extra_shape, extra_spec = (), ()
    if prev_moe is not None:
        x_specs += [pl.BlockSpec((tm * ROW_CHUNKS, LANES), lambda st, b: (b * nst + st, 0)),
                    pl.BlockSpec((1, 1, d), per_b)]
        x_args += list(prev_moe)
        extra_shape = (jax.ShapeDtypeStruct((bsz, seq, d), F32),)
        extra_spec = (pl.BlockSpec((1, tm, d), row),)
    return pl.pallas_call(
        functools.partial(_in_proj_kernel, prev_moe is not None),
        out_shape=(
            jax.ShapeDtypeStruct((bsz, seq, 3 * A_WIDTH), F32),
            jax.ShapeDtypeStruct((bsz, seq, B_Q_WIDTH), BF16),
            jax.ShapeDtypeStruct((bsz, seq, B_KV_WIDTH), BF16),
            jax.ShapeDtypeStruct((bsz, seq, B_KV_WIDTH), BF16),
            jax.ShapeDtypeStruct((bsz, seq, d), BF16),
            jax.ShapeDtypeStruct((bsz, seq, d), BF16),
        ) + extra_shape,
        grid=(nst, bsz),
        in_specs=x_specs + [
            pl.BlockSpec((1, 1, d), per_b),
            pl.BlockSpec((1, 1, d), per_b),
            pl.BlockSpec((1, d), const2),
            pl.BlockSpec((d, n_in), const2),
            pl.BlockSpec(bd.shape, const2),
            pl.BlockSpec((1, A_WIDTH), const2),
            pl.BlockSpec((1, A_WIDTH), const2),
            pl.BlockSpec((1, B_Q_WIDTH), const2),
            pl.BlockSpec((1, B_KV_WIDTH), const2),
            tab, tab, tab, tab, tab, tab,
        ],
        out_specs=(
            pl.BlockSpec((1, tm, 3 * A_WIDTH), row),
            pl.BlockSpec((1, tm, B_Q_WIDTH), row),
            pl.BlockSpec((1, tm, B_KV_WIDTH), row),
            pl.BlockSpec((1, tm, B_KV_WIDTH), row),
            pl.BlockSpec((1, tm, d), row),
            pl.BlockSpec((1, tm, d), row),
        ) + extra_spec,
        compiler_params=_cparams(("arbitrary", "arbitrary")),
        name="in_proj",
    )(*x_args, sh, sc, g, w_in, bd, *gains, *tables)


_QB = 128
_UNROLL = 16


def _dilated_kernel(q_ref, k_ref, v_ref, o_ref, m_scr, l_scr, acc_scr, bias_scr):
    seq = q_ref.shape[1]
    lane = lax.broadcasted_iota(jnp.int32, (1, LANES), 1)
    head0 = lane < HEAD_DIM
    radius = A_PATTERNS[0][0] // (2 * A_PATTERNS[0][1])
    assert all(w // (2 * d) == radius for w, d in A_PATTERNS)

    @pl.when((pl.program_id(0) == 0) & (pl.program_id(1) == 0))
    def _():
        qrow = lax.broadcasted_iota(jnp.int32, (2 * _QB, 2 * _QB), 0) & (_QB - 1)
        kcol = lax.broadcasted_iota(jnp.int32, (2 * _QB, 2 * _QB), 1)
        for i in range(3):
            bias_scr[i] = jnp.where(jnp.abs(kcol - qrow - radius * i) <= radius, 0.0, -jnp.inf)

    def merge(t):
        return jnp.where(head0, t[:_QB], t[_QB:])

    patterns = sorted(A_PATTERNS, key=lambda wd: -wd[1])
    assert patterns[-1][1] == 1 and seq % (patterns[0][1] * _QB) == 0
    for pi, (window, dil) in enumerate(patterns):
        sub_len = seq // dil
        kw = min(2 * _QB, sub_len)
        nqb = sub_len // _QB
        nblk = dil * nqb

        def wide(t, kw=kw):
            return t if kw == LANES else jnp.concatenate([t] * (kw // LANES), axis=-1)

        def load(blk, pi=pi, dil=dil, sub_len=sub_len, kw=kw, nqb=nqb):
            r = blk // nqb
            i0q = (blk % nqb) * _QB
            i0k = jnp.clip(i0q - (kw - _QB) // 2, 0, sub_len - kw)
            if dil == 1:
                qrows = pl.ds(pl.multiple_of(i0q, _QB), _QB)
                krows = pl.ds(pl.multiple_of(i0k, SUBLANES), kw)
            else:
                qrows = pl.ds(r + dil * i0q, _QB, stride=dil)
                krows = pl.ds(r + dil * i0k, kw, stride=dil)
            q = q_ref[0, qrows, :]
            kk = k_ref[0, krows, :].astype(BF16)
            vv = v_ref[0, krows, :].astype(BF16)
            old = None
            if pi > 0:
                old = (jnp.concatenate([m_scr[0, qrows, :], m_scr[1, qrows, :]], axis=0),
                       l_scr[qrows, :], acc_scr[qrows, :])
            return qrows, (i0q - i0k) // radius, q, kk, vv, old

        def compute(mask_id, q, kk, vv, old, kw=kw, wide=wide):
            bias = bias_scr[0, :, 0:LANES] if kw == LANES else bias_scr[mask_id]
            q2 = jnp.concatenate([jnp.where(head0, q, 0.0), jnp.where(head0, 0.0, q)],
                                 axis=0).astype(BF16)
            s = lax.dot_general(q2, kk, (((1,), (1,)), ((), ())),
                                preferred_element_type=F32) + bias
            mb = jnp.broadcast_to(jnp.max(s, axis=-1, keepdims=True), (2 * _QB, LANES))
            m_new = mb if old is None else jnp.maximum(old[0], mb)
            p = jnp.exp2((s - wide(m_new)).astype(BF16))
            v_aug = jnp.concatenate([vv, jnp.ones_like(vv)], axis=-1)
            oa = jnp.dot(p, v_aug, preferred_element_type=F32)
            pv = merge(oa[:, :LANES])
            psum = merge(oa[:, LANES:])
            if old is None:
                return m_new, psum, pv
            alpha = jnp.exp2(merge(old[0]) - merge(m_new))
            return m_new, alpha * old[1] + psum, alpha * old[2] + pv

        def group(it, carry, load=load, compute=compute, last=pi == len(patterns) - 1):
            loaded = [load(it * _UNROLL + u) for u in range(_UNROLL)]
            results = [compute(*ld[1:]) for ld in loaded]
            for ld, (m_new, l_new, acc_new) in zip(loaded, results):
                qrows = ld[0]
                if last:
                    o_ref[0, qrows, :] = (acc_new / l_new).astype(o_ref.dtype)
                else:
                    m_scr[0, qrows, :] = m_new[:_QB]
                    m_scr[1, qrows, :] = m_new[_QB:]
                    l_scr[qrows, :] = l_new
                    acc_scr[qrows, :] = acc_new
            return carry

        lax.fori_loop(0, nblk // _UNROLL, group, 0)


def _dilated(za):
    bsz, seq, _ = za.shape
    nhp = A_WIDTH // LANES
    blk = (1, seq, LANES)
    return pl.pallas_call(
        _dilated_kernel,
        out_shape=jax.ShapeDtypeStruct((bsz, seq, A_WIDTH), BF16),
        grid=(bsz, nhp),
        in_specs=[
            pl.BlockSpec(blk, lambda b, j: (b, 0, j)),
            pl.BlockSpec(blk, lambda b, j: (b, 0, nhp + j)),
            pl.BlockSpec(blk, lambda b, j: (b, 0, 2 * nhp + j)),
        ],
        out_specs=pl.BlockSpec(blk, lambda b, j: (b, 0, j)),
        scratch_shapes=[
            pltpu.VMEM((2, seq, LANES), F32),
            pltpu.VMEM((seq, LANES), F32),
            pltpu.VMEM((seq, LANES), F32),
            pltpu.VMEM((3, 2 * _QB, 2 * _QB), F32),
        ],
        compiler_params=_cparams(("arbitrary", "arbitrary")),
        name="dilated_attn",
    )(za, za, za)


def _gqa_kernel(q_ref, k_ref, v_ref, o_ref):
    tq = q_ref.shape[1]
    lane = lax.broadcasted_iota(jnp.int32, (1, LANES), 1)
    head0 = lane < HEAD_DIM
    kk = k_ref[0]
    zero = jnp.zeros((), BF16)
    v_aug = jnp.concatenate([v_ref[0], jnp.ones_like(v_ref[0])], axis=-1)
    scores = []
    for j in range(B_Q_WIDTH // LANES):
        qj = q_ref[0, :, j * LANES:(j + 1) * LANES]
        q2 = jnp.concatenate([jnp.where(head0, qj, zero), jnp.where(head0, zero, qj)], axis=0)
        scores.append(lax.dot_general(q2, kk, (((1,), (1,)), ((), ())),
                                      preferred_element_type=F32))
    for j, s in enumerate(scores):
        m = jnp.max(s, axis=-1, keepdims=True)
        pb = jnp.exp2((s - m).astype(BF16))
        oa = jnp.dot(pb, v_aug, preferred_element_type=F32)
        pv = oa[:, :LANES] / oa[:, LANES:]
        o_ref[0, :, j * LANES:(j + 1) * LANES] = jnp.where(
            head0, pv[:tq], pv[tq:]).astype(o_ref.dtype)


def _gqa(qb, kb, vb, tq):
    bsz, seq, _ = qb.shape
    kv_spec = pl.BlockSpec((1, seq, B_KV_WIDTH), lambda b, i: (b, 0, 0))
    return pl.pallas_call(
        _gqa_kernel,
        out_shape=jax.ShapeDtypeStruct((bsz, seq, B_Q_WIDTH), BF16),
        grid=(bsz, seq // tq),
        in_specs=[pl.BlockSpec((1, tq, B_Q_WIDTH), lambda b, i: (b, i, 0)), kv_spec, kv_spec],
        out_specs=pl.BlockSpec((1, tq, B_Q_WIDTH), lambda b, i: (b, i, 0)),
        compiler_params=_cparams(("arbitrary", "arbitrary")),
        name="gqa_attn",
    )(qb, kb, vb)


def _merge_kernel(x_ref, ya_ref, yb_ref, ga_ref, gb_ref, wpa_ref, wpb_ref, wo_ref,
                  gt_ref, g_ref, sh_ref, sc_ref, wr_ref,
                  xn_ref, hrow_ref, e_ref, w_ref):
    tm = x_ref.shape[0]
    pa = jnp.dot(ya_ref[...], wpa_ref[...], preferred_element_type=F32)
    pb = jnp.dot(yb_ref[...], wpb_ref[...], preferred_element_type=F32)
    merged = ga_ref[...].astype(F32) * pa + gb_ref[...].astype(F32) * pb
    out = jnp.dot(merged.astype(BF16), wo_ref[...], preferred_element_type=F32)
    xn = x_ref[...] + gt_ref[0] * out
    xn_ref[...] = xn

    ms = jnp.mean(xn * xn, axis=-1, keepdims=True)
    h = xn * lax.rsqrt(ms + NORM_EPS) * g_ref[...]
    h = h * (1.0 + sc_ref[0]) + sh_ref[0]
    for c in range(ROW_CHUNKS):
        hrow_ref[pl.ds(c, tm, stride=ROW_CHUNKS), :] = h[:, c * LANES:(c + 1) * LANES]

    logits = lax.dot_general(wr_ref[...], h, (((1,), (1,)), ((), ())),
                             preferred_element_type=F32,
                             precision=lax.Precision.HIGHEST)
    mx = jnp.max(logits, axis=0, keepdims=True)
    ex = jnp.exp(logits - mx)
    probs = ex / jnp.sum(ex, axis=0, keepdims=True)
    pg = probs.reshape(N_GROUPS, EXPERTS_PER_GROUP, tm)
    sub = lax.broadcasted_iota(jnp.int32, pg.shape, 1).astype(F32)
    m1 = jnp.max(pg, axis=1, keepdims=True)
    i1 = jnp.min(jnp.where(pg == m1, sub, float(EXPERTS_PER_GROUP)), axis=1, keepdims=True)
    pg2 = jnp.where(sub == i1, -1.0, pg)
    m2 = jnp.max(pg2, axis=1, keepdims=True)
    i2 = jnp.min(jnp.where(pg2 == m2, sub, float(EXPERTS_PER_GROUP)), axis=1, keepdims=True)
    score = m1 + m2
    gid = lax.broadcasted_iota(jnp.int32, score.shape, 0).astype(F32)
    best = jnp.max(score, axis=0, keepdims=True)
    gsel = jnp.min(jnp.where(score == best, gid, float(N_GROUPS)), axis=0, keepdims=True)
    pick = gid == gsel
    w0 = jnp.sum(jnp.where(pick, m1, 0.0), axis=0)
    w1 = jnp.sum(jnp.where(pick, m2, 0.0), axis=0)
    j0 = jnp.sum(jnp.where(pick, i1, 0.0), axis=0)
    j1 = jnp.sum(jnp.where(pick, i2, 0.0), axis=0)
    base = gsel[0] * float(EXPERTS_PER_GROUP)
    tot = w0 + w1
    e_ref[0, 0:1, :] = (base + j0).astype(jnp.int32)
    e_ref[0, 1:2, :] = (base + j1).astype(jnp.int32)
    w_ref[0, 0:1, :] = w0 / tot
    w_ref[0, 1:2, :] = w1 / tot


def _merge(x, ya, yb, ga, gb, wpa, wpb, wo, gt, g, sh, sc, wr_t, tm, tiles_per_seq):
    t, d = x.shape
    nt = t // tm
    row = lambda i: (i, 0)
    per_b = lambda i: (i // tiles_per_seq, 0, 0)
    const2 = lambda i: (0, 0)
    return pl.pallas_call(
        _merge_kernel,
        out_shape=(
            jax.ShapeDtypeStruct((t, d), F32),
            jax.ShapeDtypeStruct((t * ROW_CHUNKS, LANES), F32),
            jax.ShapeDtypeStruct((nt, TOP_K, tm), jnp.int32),
            jax.ShapeDtypeStruct((nt, TOP_K, tm), F32),
        ),
        grid=(nt,),
        in_specs=[
            pl.BlockSpec((tm, d), row),
            pl.BlockSpec((tm, A_WIDTH), row),
            pl.BlockSpec((tm, B_Q_WIDTH), row),
            pl.BlockSpec((tm, d), row),
            pl.BlockSpec((tm, d), row),
            pl.BlockSpec((A_WIDTH, d), const2),
            pl.BlockSpec((B_Q_WIDTH, d), const2),
            pl.BlockSpec((d, d), const2),
            pl.BlockSpec((1, 1, d), per_b),
            pl.BlockSpec((1, d), const2),
            pl.BlockSpec((1, 1, d), per_b),
            pl.BlockSpec((1, 1, d), per_b),
            pl.BlockSpec((N_EXPERTS, d), const2),
        ],
        out_specs=(
            pl.BlockSpec((tm, d), row),
            pl.BlockSpec((tm * ROW_CHUNKS, LANES), row),
            pl.BlockSpec((1, TOP_K, tm), lambda i: (i, 0, 0)),
            pl.BlockSpec((1, TOP_K, tm), lambda i: (i, 0, 0)),
        ),
        compiler_params=_cparams(("arbitrary",)),
        name="merge_router",
    )(x, ya, yb, ga, gb, wpa, wpb, wo, gt, g, sh, sc, wr_t)


_ROW_UNROLL = 8
_BLOCKS_PER_STEP = 2
_STEP_ROWS = _BLOCKS_PER_STEP * MOE_BLOCK
_FFN_ROWS = (64, MOE_BLOCK)


def _expert_kernel(step_e_ref, step_cnt_ref, step_base_ref, tok_ref, wrow_ref, h_hbm,
                   wg_ref, wu_ref, wd_ref, acc_hbm, xs, acc, gbuf, ybuf, sem):
    g = pl.program_id(0)
    j = pl.program_id(1)
    n_groups = pl.num_programs(0)
    n_steps = pl.num_programs(1)
    group_len = xs.shape[0]

    def group_rows(gi):
        return pl.ds(pl.multiple_of(gi * group_len, ROW_CHUNKS), group_len)

    def store_copy(gi):
        return pltpu.make_async_copy(acc.at[pl.ds(0, group_len), :],
                                     acc_hbm.at[group_rows(gi), :], sem.at[1])

    @pl.when(j == 0)
    def _():
        load = pltpu.make_async_copy(h_hbm.at[group_rows(g), :], xs, sem.at[0])
        load.start()

        @pl.when(g > 0)
        def _():
            store_copy(g - 1).wait()
        acc[...] = jnp.zeros_like(acc)

        @pl.when(g == 0)
        def _():
            gbuf[...] = jnp.zeros_like(gbuf)
        load.wait()

    step = g * n_steps + j
    for sub in range(_BLOCKS_PER_STEP):
        cnt = jnp.clip(step_cnt_ref[step] - sub * MOE_BLOCK, 0, MOE_BLOCK)
        pl.when(cnt > 0)(functools.partial(
            _expert_block, cnt, step_base_ref[step] + sub * MOE_BLOCK, tok_ref, wrow_ref,
            wg_ref, wu_ref, wd_ref, xs, acc, gbuf, ybuf))

    @pl.when(j == n_steps - 1)
    def _():
        store_copy(g).start()

        @pl.when(g == n_groups - 1)
        def _():
            store_copy(g).wait()


def _expert_block(cnt, first, tok_ref, wrow_ref, wg_ref, wu_ref, wd_ref, xs, acc, gbuf, ybuf):
    group_len = xs.shape[0]

    def tile_at(row):
        return pl.ds(pl.multiple_of(row, ROW_CHUNKS), ROW_CHUNKS)

    nfull = cnt // _ROW_UNROLL
    tail = cnt - nfull * _ROW_UNROLL

    def gather(c, carry):
        base = c * _ROW_UNROLL
        for u in range(_ROW_UNROLL):
            gbuf[tile_at((base + u) * ROW_CHUNKS), :] = xs[
                tile_at(tok_ref[0, 0, first + base + u]), :]
        return carry
    lax.fori_loop(0, (cnt + _ROW_UNROLL - 1) // _ROW_UNROLL, gather, 0)

    def ffn(rows):
        xb = jnp.concatenate(
            [gbuf[pl.ds(c, rows, stride=ROW_CHUNKS), :] for c in range(ROW_CHUNKS)],
            axis=-1).astype(BF16)
        gate = jnp.dot(xb, wg_ref[0], preferred_element_type=F32)
        up = jnp.dot(xb, wu_ref[0], preferred_element_type=F32)
        hid = (gate * jax.nn.sigmoid(gate) * up).astype(BF16)
        y = jnp.dot(hid, wd_ref[0], preferred_element_type=F32)
        for c in range(ROW_CHUNKS):
            ybuf[pl.ds(c, rows, stride=ROW_CHUNKS), :] = y[:, c * LANES:(c + 1) * LANES]

    lo = 0
    for rows in _FFN_ROWS:
        pl.when((cnt > lo) & (cnt <= rows))(functools.partial(ffn, rows))
        lo = rows

    def scatter_chunk(base, n_real):
        new = []
        for u in range(_ROW_UNROLL):
            dst_row = tok_ref[0, 0, first + base + u]
            wgt = wrow_ref[0, 0, first + base + u]
            if n_real is not None:
                dst_row = jnp.where(u < n_real, dst_row, group_len)
                wgt = jnp.where(u < n_real, wgt, 0.0)
            dst = tile_at(dst_row)
            new.append((dst, acc[dst, :] + wgt * ybuf[tile_at((base + u) * ROW_CHUNKS), :]))
        for dst, val in new:
            acc[dst, :] = val

    def scatter_add(c, carry):
        scatter_chunk(c * _ROW_UNROLL, None)
        return carry
    lax.fori_loop(0, nfull, scatter_add, 0)

    @pl.when(tail > 0)
    def _():
        scatter_chunk(nfull * _ROW_UNROLL, tail)


def _experts(hrows, plan, wg, wu, wd, ts):
    step_e, step_cnt, step_first, tok, wrow = plan
    n_groups, list_len = tok.shape[0], tok.shape[2]
    n_steps = step_e.shape[0] // n_groups
    d = D_MODEL
    idx_blk = (1, 1, list_len)
    idx_map = lambda g, j, se, sc, sf: (g, 0, 0)
    w_map = lambda g, j, se, sc, sf: (se[g * n_steps + j], 0, 0)
    grid_spec = pltpu.PrefetchScalarGridSpec(
        num_scalar_prefetch=3,
        grid=(n_groups, n_steps),
        in_specs=[
            pl.BlockSpec(idx_blk, idx_map, memory_space=pltpu.SMEM),
            pl.BlockSpec(idx_blk, idx_map, memory_space=pltpu.SMEM),
            pl.BlockSpec(memory_space=pl.ANY),
            pl.BlockSpec((1, d, D_FF_EXPERT), w_map),
            pl.BlockSpec((1, d, D_FF_EXPERT), w_map),
            pl.BlockSpec((1, D_FF_EXPERT, d), w_map),
        ],
        out_specs=pl.BlockSpec(memory_space=pl.ANY),
        scratch_shapes=[
            pltpu.VMEM((ts * ROW_CHUNKS, LANES), F32),
            pltpu.VMEM(((ts + 1) * ROW_CHUNKS, LANES), F32),
            pltpu.VMEM((MOE_BLOCK * ROW_CHUNKS, LANES), F32),
            pltpu.VMEM((MOE_BLOCK * ROW_CHUNKS, LANES), F32),
            pltpu.SemaphoreType.DMA((2,)),
        ],
    )
    return pl.pallas_call(
        _expert_kernel,
        out_shape=jax.ShapeDtypeStruct(hrows.shape, F32),
        grid_spec=grid_spec,
        compiler_params=_cparams(("arbitrary", "arbitrary")),
        name="experts",
    )(step_e, step_cnt, step_first, tok, wrow, hrows, wg, wu, wd)


def _combine_kernel(x_ref, y_ref, gt_ref, o_ref):
    tm = x_ref.shape[0]
    gt = gt_ref[0]
    for c in range(ROW_CHUNKS):
        cols = slice(c * LANES, (c + 1) * LANES)
        o_ref[:, cols] = x_ref[:, cols] + gt[:, cols] * y_ref[pl.ds(c, tm, stride=ROW_CHUNKS), :]


def _combine(xn, yrows, gt, tm, tiles_per_seq):
    t, d = xn.shape
    nt = t // tm
    return pl.pallas_call(
        _combine_kernel,
        out_shape=jax.ShapeDtypeStruct((t, d), F32),
        grid=(nt,),
        in_specs=[
            pl.BlockSpec((tm, d), lambda i: (i, 0)),
            pl.BlockSpec((tm * ROW_CHUNKS, LANES), lambda i: (i, 0)),
            pl.BlockSpec((1, 1, d), lambda i: (i // tiles_per_seq, 0, 0)),
        ],
        out_specs=pl.BlockSpec((tm, d), lambda i: (i, 0)),
        compiler_params=_cparams(("arbitrary",)),
        name="moe_combine",
    )(xn, yrows, gt)


def _dispatch_plan(e_sel, w_sel, ts):
    t = e_sel.shape[0]
    n_groups = t // ts
    na = ts * TOP_K
    n_steps = na // _STEP_ROWS + N_EXPERTS
    e_flat = e_sel.reshape(n_groups, ts, TOP_K).transpose(0, 2, 1).reshape(n_groups, na)
    w_flat = w_sel.reshape(n_groups, ts, TOP_K).transpose(0, 2, 1).reshape(n_groups, na)
    a_ids = jnp.broadcast_to(jnp.arange(na, dtype=jnp.int32)[None, :], (n_groups, na))
    _, a_sorted, w_sorted = lax.sort((e_flat, a_ids, w_flat), dimension=1, num_keys=1)
    experts = jnp.arange(N_EXPERTS, dtype=jnp.int32)
    counts = jnp.sum((e_flat[:, :, None] == experts[None, None, :]).astype(jnp.int32), axis=1)
    padded = (counts + _STEP_ROWS - 1) // _STEP_ROWS * _STEP_ROWS
    pad_end = jnp.cumsum(padded, axis=1)
    pad_start = pad_end - padded
    start = jnp.cumsum(counts, axis=1) - counts
    row0 = jnp.arange(n_steps, dtype=jnp.int32) * _STEP_ROWS
    step_e = jnp.minimum(
        jnp.sum((row0[None, :, None] >= pad_end[:, None, :]).astype(jnp.int32), axis=2),
        N_EXPERTS - 1)
    is_e = step_e[:, :, None] == experts[None, None, :]
    take = lambda table: jnp.sum(jnp.where(is_e, table[:, None, :], 0), axis=2)
    off = row0[None, :] - take(pad_start)
    step_cnt = jnp.clip(take(counts) - off, 0, _STEP_ROWS)
    step_first = jnp.clip(take(start) + off, 0, na - 1)
    flat = lambda v: v.astype(jnp.int32).reshape(n_groups * n_steps)
    filler = jnp.zeros((n_groups, LANES), jnp.int32)
    tok_rows = jnp.concatenate([(a_sorted % ts) * ROW_CHUNKS, filler], axis=1).astype(jnp.int32)
    w_list = jnp.concatenate([w_sorted.astype(F32), filler.astype(F32)], axis=1)
    return (flat(step_e), flat(step_cnt), flat(step_first),
            tok_rows.reshape(n_groups, 1, na + LANES), w_list.reshape(n_groups, 1, na + LANES))


def _tile_sizes(bsz, seq):
    tm = min(512, seq)
    tm_merge = min(1024, seq)
    tq = min(256, seq)
    ts = min(4096, bsz * seq)
    assert seq % tm == 0 and seq % tm_merge == 0 and seq % tq == 0
    assert (bsz * seq) % ts == 0 and ts % tm_merge == 0
    return tm, tm_merge, tq, ts


def kernel(x, c, w_ada, b_ada, g_mix, w_in, qn_a, kn_a, qn_b, kn_b, w_pa, w_pb, w_out,
           g_ffn, w_router, w_gate, w_up, w_down):
    bsz, seq, d = x.shape
    depth = w_ada.shape[0]
    t = bsz * seq
    tm, tm_merge, tq, ts = _tile_sizes(bsz, seq)
    tiles_per_seq = seq // tm

    mod = _adaln(c, w_ada, b_ada)
    tables = _rope_tables(seq)
    bd_heads = 2 * LANES // HEAD_DIM
    bd = jnp.asarray(np.kron(np.eye(bd_heads), np.ones((HEAD_DIM, HEAD_DIM))), BF16)

    g4 = B_Q_HEADS // B_KV_HEADS
    head_perm = np.concatenate(
        [np.r_[np.arange(j * HEAD_DIM, (j + 1) * HEAD_DIM),
               np.arange((j + g4) * HEAD_DIM, (j + g4 + 1) * HEAD_DIM)] for j in range(g4)])
    qb_lo = 3 * A_WIDTH
    col_perm = np.arange(w_in.shape[2])
    col_perm[qb_lo:qb_lo + B_Q_WIDTH] = qb_lo + head_perm
    scale = HEAD_DIM ** -0.5 * float(np.log2(np.e))
    wr_t = w_router.T

    xf = x
    prev_moe = None
    for l in range(depth):
        sh1, sc1, gt1, sh2, sc2, gt2 = [
            mod[l, :, i * d:(i + 1) * d].reshape(bsz, 1, d) for i in range(6)]
        w_in_l = w_in[l][:, col_perm].astype(BF16)
        gains = (
            jnp.tile(qn_a[l] * scale, A_HEADS).reshape(1, A_WIDTH),
            jnp.tile(kn_a[l], A_HEADS).reshape(1, A_WIDTH),
            jnp.tile(qn_b[l] * scale, B_Q_HEADS).reshape(1, B_Q_WIDTH),
            jnp.tile(kn_b[l], B_KV_HEADS).reshape(1, B_KV_WIDTH),
        )
        outs = _in_proj(xf, prev_moe, sh1, sc1, g_mix[l].reshape(1, d), w_in_l, bd, gains,
                        tables, tm)
        za, qb, kb, vb, ga, gb = outs[:6]
        if prev_moe is not None:
            xf = outs[6]
        ya = _dilated(za)
        yb = _gqa(qb, kb, vb, tq)
        xn, hrows, e_sel, w_sel = _merge(
            xf.reshape(t, d), ya.reshape(t, A_WIDTH), yb.reshape(t, B_Q_WIDTH),
            ga.reshape(t, d), gb.reshape(t, d),
            w_pa[l].astype(BF16), w_pb[l][head_perm].astype(BF16), w_out[l].astype(BF16),
            gt1, g_ffn[l].reshape(1, d), sh2, sc2, wr_t, tm_merge, seq // tm_merge)
        e_tok = e_sel.transpose(0, 2, 1).reshape(t, TOP_K)
        w_tok = w_sel.transpose(0, 2, 1).reshape(t, TOP_K)
        plan = _dispatch_plan(e_tok, w_tok, ts)
        yrows = _experts(hrows, plan, w_gate[l].astype(BF16), w_up[l].astype(BF16),
                         w_down[l].astype(BF16), ts)
        xf = xn.reshape(bsz, seq, d)
        prev_moe = (yrows, gt2)
    out = _combine(xf.reshape(t, d), prev_moe[0], prev_moe[1], tm, tiles_per_seq)
    return out.reshape(bsz, seq, d)
```

```python
import functools

import jax
import jax.numpy as jnp
import numpy as np
from jax import lax
from jax.experimental import pallas as pl
from jax.experimental.pallas import tpu as pltpu

D_MODEL = 1024
HEAD_DIM = 64
NORM_EPS = 1e-6
A_HEADS = 8
A_WIDTH = A_HEADS * HEAD_DIM
A_PATTERNS = ((128, 1), (512, 4), (2048, 16))
A_ROT_DIMS = HEAD_DIM // 4
A_ROPE_THETA = 500000.0
B_Q_HEADS = 8
B_KV_HEADS = 2
B_Q_WIDTH = B_Q_HEADS * HEAD_DIM
B_KV_WIDTH = B_KV_HEADS * HEAD_DIM
B_ROPE_THETA = 10000.0
GRID_W = 64
N_EXPERTS = 32
N_GROUPS = 4
EXPERTS_PER_GROUP = N_EXPERTS // N_GROUPS
TOP_K = 2
D_FF_EXPERT = D_MODEL // 2
MOE_BLOCK = 256

LANES = 128
SUBLANES = 8
ROW_CHUNKS = D_MODEL // LANES
assert ROW_CHUNKS == SUBLANES
VMEM_LIMIT = 56 * 1024 * 1024

F32 = jnp.float32
BF16 = jnp.bfloat16


def _cparams(sem):
    return pltpu.CompilerParams(dimension_semantics=sem, vmem_limit_bytes=VMEM_LIMIT)


def _adaln_kernel(c_ref, w_ref, b_ref, o_ref):
    c = c_ref[...]
    cond = c * jax.nn.sigmoid(c)
    o_ref[0] = jnp.dot(cond, w_ref[0], preferred_element_type=F32,
                       precision=lax.Precision.HIGHEST) + b_ref[0]


def _adaln(c, w_ada, b_ada):
    depth, d, n = w_ada.shape
    bsz = c.shape[0]
    tn = 1024
    return pl.pallas_call(
        _adaln_kernel,
        out_shape=jax.ShapeDtypeStruct((depth, bsz, n), F32),
        grid=(depth, n // tn),
        in_specs=[
            pl.BlockSpec((bsz, d), lambda l, j: (0, 0)),
            pl.BlockSpec((1, d, tn), lambda l, j: (l, 0, j)),
            pl.BlockSpec((1, 1, tn), lambda l, j: (l, 0, j)),
        ],
        out_specs=pl.BlockSpec((1, bsz, tn), lambda l, j: (l, 0, j)),
        compiler_params=_cparams(("arbitrary", "arbitrary")),
        name="adaln",
    )(c, w_ada, b_ada.reshape(depth, 1, n))


def _rope_tables(seq):
    pos = jnp.arange(seq, dtype=F32)
    row = jnp.floor(pos / GRID_W)
    col = pos - row * GRID_W
    d = np.arange(LANES) % HEAD_DIM

    def build(segments):
        c = jnp.ones((seq, LANES), F32)
        s1 = jnp.zeros((seq, LANES), F32)
        s2 = jnp.zeros((seq, LANES), F32)
        for lo, half, theta, p in segments:
            first = (d >= lo) & (d < lo + half)
            second = (d >= lo + half) & (d < lo + 2 * half)
            idx = np.where(first, d - lo, np.where(second, d - lo - half, 0))
            freqs = theta ** (-jnp.arange(half, dtype=F32) / half)
            ang = p[:, None] * freqs[idx][None, :]
            cs, sn = jnp.cos(ang), jnp.sin(ang)
            rot = jnp.asarray(first | second)[None, :]
            c = jnp.where(rot, cs, c)
            s1 = jnp.where(jnp.asarray(first)[None, :], -sn, s1)
            s2 = jnp.where(jnp.asarray(second)[None, :], sn, s2)
        return c, s1, s2

    ta = build([(0, A_ROT_DIMS // 2, A_ROPE_THETA, pos)])
    q = HEAD_DIM // 4
    tb = build([(0, q, B_ROPE_THETA, row), (2 * q, q, B_ROPE_THETA, col)])
    return ta + tb


def _in_proj_kernel(prev_moe, *refs):
    if prev_moe:
        xn_ref, y_ref, gtp_ref, *refs = refs
        xo_ref = refs.pop()
        tm = xn_ref.shape[1]
        moe = jnp.concatenate([y_ref[pl.ds(c, tm, stride=ROW_CHUNKS), :]
                               for c in range(ROW_CHUNKS)], axis=-1)
        x = xn_ref[0] + gtp_ref[0] * moe
        xo_ref[0] = x
    else:
        x_ref, *refs = refs
        x = x_ref[0]
    (sh_ref, sc_ref, g_ref, w_ref, bd_ref, gqa_ref, gka_ref, gqb_ref, gkb_ref,
     ca_ref, s1a_ref, s2a_ref, cb_ref, s1b_ref, s2b_ref,
     za_ref, qb_ref, kb_ref, vb_ref, ga_ref, gb_ref) = refs
    ms = jnp.mean(x * x, axis=-1, keepdims=True)
    h = x * lax.rsqrt(ms + NORM_EPS) * g_ref[...]
    h = h * (1.0 + sc_ref[0]) + sh_ref[0]
    hb = h.astype(BF16)

    def seg(lo, width):
        return jnp.dot(hb, w_ref[:, lo:lo + width], preferred_element_type=F32)

    def qk_norm(z, gain_ref):
        width = z.shape[-1]
        bdw = min(width, bd_ref.shape[0])
        sq = (z * z).astype(BF16)
        parts = [jnp.dot(sq[:, lo:lo + bdw], bd_ref[:bdw, :bdw], preferred_element_type=F32)
                 for lo in range(0, width, bdw)]
        ss = parts[0] if len(parts) == 1 else jnp.concatenate(parts, axis=-1)
        return z * lax.rsqrt(ss * (1.0 / HEAD_DIM) + NORM_EPS) * gain_ref[...]

    def tile(t, width):
        reps = width // LANES
        return t if reps == 1 else jnp.concatenate([t] * reps, axis=-1)

    def rope(z, c_ref, s1_ref, s2_ref, half):
        width = z.shape[-1]
        up = pltpu.roll(z, width - half, 1)
        dn = pltpu.roll(z, half, 1)
        return (z * tile(c_ref[...], width) + up * tile(s1_ref[...], width)
                + dn * tile(s2_ref[...], width))

    ha = A_ROT_DIMS // 2
    hq = HEAD_DIM // 4
    o = 0
    qa = rope(qk_norm(seg(o, A_WIDTH), gqa_ref), ca_ref, s1a_ref, s2a_ref, ha)
    za_ref[0, :, 0:A_WIDTH] = qa
    o += A_WIDTH
    ka = rope(qk_norm(seg(o, A_WIDTH), gka_ref), ca_ref, s1a_ref, s2a_ref, ha)
    za_ref[0, :, A_WIDTH:2 * A_WIDTH] = ka
    o += A_WIDTH
    za_ref[0, :, 2 * A_WIDTH:3 * A_WIDTH] = seg(o, A_WIDTH)
    o += A_WIDTH
    qb = rope(qk_norm(seg(o, B_Q_WIDTH), gqb_ref), cb_ref, s1b_ref, s2b_ref, hq)
    qb_ref[0] = qb.astype(BF16)
    o += B_Q_WIDTH
    kb = rope(qk_norm(seg(o, B_KV_WIDTH), gkb_ref), cb_ref, s1b_ref, s2b_ref, hq)
    kb_ref[0] = kb.astype(BF16)
    o += B_KV_WIDTH
    vb_ref[0] = seg(o, B_KV_WIDTH).astype(BF16)
    o += B_KV_WIDTH
    ga_ref[0] = jax.nn.sigmoid(seg(o, D_MODEL)).astype(BF16)
    o += D_MODEL
    gb_ref[0] = jax.nn.sigmoid(seg(o, D_MODEL)).astype(BF16)


def _in_proj(x, prev_moe, sh, sc, g, w_in, bd, gains, tables, tm):
    bsz, seq, d = x.shape
    n_in = w_in.shape[1]
    nst = seq // tm
    row = lambda st, b: (b, st, 0)
    per_b = lambda st, b: (b, 0, 0)
    const2 = lambda st, b: (0, 0)
    tab = pl.BlockSpec((tm, LANES), lambda st, b: (st, 0))
    x_specs = [pl.BlockSpec((1, tm, d), row)]
    x_args = [x]
    extra_shape, extra_spec = (), ()
    if prev_moe is not None:
        x_specs += [pl.BlockSpec((tm * ROW_CHUNKS, LANES), lambda st, b: (b * nst + st, 0)),
                    pl.BlockSpec((1, 1, d), per_b)]
        x_args += list(prev_moe)
        extra_shape = (jax.ShapeDtypeStruct((bsz, seq, d), F32),)
        extra_spec = (pl.BlockSpec((1, tm, d), row),)
    return pl.pallas_call(
        functools.partial(_in_proj_kernel, prev_moe is not None),
        out_shape=(
            jax.ShapeDtypeStruct((bsz, seq, 3 * A_WIDTH), F32),
            jax.ShapeDtypeStruct((bsz, seq, B_Q_WIDTH), BF16),
            jax.ShapeDtypeStruct((bsz, seq, B_KV_WIDTH), BF16),
            jax.ShapeDtypeStruct((bsz, seq, B_KV_WIDTH), BF16),
            jax.ShapeDtypeStruct((bsz, seq, d), BF16),
            jax.ShapeDtypeStruct((bsz, seq, d), BF16),
        ) + extra_shape,
        grid=(nst, bsz),
        in_specs=x_specs + [
            pl.BlockSpec((1, 1, d), per_b),
            pl.BlockSpec((1, 1, d), per_b),
            pl.BlockSpec((1, d), const2),
            pl.BlockSpec((d, n_in), const2),
            pl.BlockSpec(bd.shape, const2),
            pl.BlockSpec((1, A_WIDTH), const2),
            pl.BlockSpec((1, A_WIDTH), const2),
            pl.BlockSpec((1, B_Q_WIDTH), const2),
            pl.BlockSpec((1, B_KV_WIDTH), const2),
            tab, tab, tab, tab, tab, tab,
        ],
        out_specs=(
            pl.BlockSpec((1, tm, 3 * A_WIDTH), row),
            pl.BlockSpec((1, tm, B_Q_WIDTH), row),
            pl.BlockSpec((1, tm, B_KV_WIDTH), row),
            pl.BlockSpec((1, tm, B_KV_WIDTH), row),
            pl.BlockSpec((1, tm, d), row),
            pl.BlockSpec((1, tm, d), row),
        ) + extra_spec,
        compiler_params=_cparams(("arbitrary", "arbitrary")),
        name="in_proj",
    )(*x_args, sh, sc, g, w_in, bd, *gains, *tables)


_QB = 128
_UNROLL = 16


def _dilated_kernel(q_ref, k_ref, v_ref, o_ref, m_scr, l_scr, acc_scr, bias_scr):
    seq = q_ref.shape[1]
    lane = lax.broadcasted_iota(jnp.int32, (1, LANES), 1)
    head0 = lane < HEAD_DIM
    radius = A_PATTERNS[0][0] // (2 * A_PATTERNS[0][1])
    assert all(w // (2 * d) == radius for w, d in A_PATTERNS) and 2 * radius == _QB

    @pl.when((pl.program_id(0) == 0) & (pl.program_id(1) == 0))
    def _():
        qrow = lax.broadcasted_iota(jnp.int32, (2 * _QB, 2 * _QB), 0) & (_QB - 1)
        kcol = lax.broadcasted_iota(jnp.int32, (2 * _QB, 2 * _QB), 1)
        for i in range(3):
            bias_scr[i] = jnp.where(jnp.abs(kcol - qrow - radius * i) <= radius, 0.0, -jnp.inf)

    def merge(t):
        return jnp.where(head0, t[:_QB], t[_QB:])

    patterns = sorted(A_PATTERNS, key=lambda wd: -wd[1])
    assert patterns[-1][1] == 1 and seq % (patterns[0][1] * _QB) == 0
    for pi, (window, dil) in enumerate(patterns):
        sub_len = seq // dil
        kw = min(2 * _QB, sub_len)
        nqb = sub_len // _QB
        nblk = dil * nqb

        def wide(t, kw=kw):
            return t if kw == LANES else jnp.concatenate([t] * (kw // LANES), axis=-1)

        def load(blk, pi=pi, dil=dil, sub_len=sub_len, kw=kw, nqb=nqb):
            r = blk // nqb
            i0q = (blk % nqb) * _QB
            i0k = jnp.clip(i0q - (kw - _QB) // 2, 0, sub_len - kw)
            if dil == 1:
                qrows = pl.ds(pl.multiple_of(i0q, _QB), _QB)
                krows = pl.ds(pl.multiple_of(i0k, SUBLANES), kw)
            else:
                qrows = pl.ds(r + dil * i0q, _QB, stride=dil)
                krows = pl.ds(r + dil * i0k, kw, stride=dil)
            q = q_ref[0, qrows, :]
            kk = k_ref[0, krows, :].astype(BF16)
            vv = v_ref[0, krows, :].astype(BF16)
            old = None
            if pi > 0:
                old = (jnp.concatenate([m_scr[0, qrows, :], m_scr[1, qrows, :]], axis=0),
                       l_scr[qrows, :], acc_scr[qrows, :])
            return qrows, (i0q - i0k) // radius, q, kk, vv, old

        def compute(mask_id, q, kk, vv, old, kw=kw, wide=wide):
            bias = bias_scr[0, :, 0:LANES] if kw == LANES else bias_scr[mask_id]
            q2 = jnp.concatenate([jnp.where(head0, q, 0.0), jnp.where(head0, 0.0, q)],
                                 axis=0).astype(BF16)
            s = lax.dot_general(q2, kk, (((1,), (1,)), ((), ())),
                                preferred_element_type=F32) + bias
            mb = jnp.broadcast_to(jnp.max(s, axis=-1, keepdims=True), (2 * _QB, LANES))
            m_new = mb if old is None else jnp.maximum(old[0], mb)
            p = jnp.exp2((s - wide(m_new)).astype(BF16))
            v_aug = jnp.concatenate([vv, jnp.ones_like(vv)], axis=-1)
            oa = jnp.dot(p, v_aug, preferred_element_type=F32)
            pv = merge(oa[:, :LANES])
            psum = merge(oa[:, LANES:])
            if old is None:
                return m_new, psum, pv
            alpha = jnp.exp2(merge(old[0]) - merge(m_new))
            return m_new, alpha * old[1] + psum, alpha * old[2] + pv

        def group(it, carry, load=load, compute=compute, last=pi == len(patterns) - 1):
            loaded = [load(it * _UNROLL + u) for u in range(_UNROLL)]
            results = [compute(*ld[1:]) for ld in loaded]
            for ld, (m_new, l_new, acc_new) in zip(loaded, results):
                qrows = ld[0]
                if last:
                    o_ref[0, qrows, :] = (acc_new / l_new).astype(o_ref.dtype)
                else:
                    m_scr[0, qrows, :] = m_new[:_QB]
                    m_scr[1, qrows, :] = m_new[_QB:]
                    l_scr[qrows, :] = l_new
                    acc_scr[qrows, :] = acc_new
            return carry

        lax.fori_loop(0, nblk // _UNROLL, group, 0)


def _dilated(za):
    bsz, seq, _ = za.shape
    nhp = A_WIDTH // LANES
    blk = (1, seq, LANES)
    return pl.pallas_call(
        _dilated_kernel,
        out_shape=jax.ShapeDtypeStruct((bsz, seq, A_WIDTH), BF16),
        grid=(bsz, nhp),
        in_specs=[
            pl.BlockSpec(blk, lambda b, j: (b, 0, j)),
            pl.BlockSpec(blk, lambda b, j: (b, 0, nhp + j)),
            pl.BlockSpec(blk, lambda b, j: (b, 0, 2 * nhp + j)),
        ],
        out_specs=pl.BlockSpec(blk, lambda b, j: (b, 0, j)),
        scratch_shapes=[
            pltpu.VMEM((2, seq, LANES), F32),
            pltpu.VMEM((seq, LANES), F32),
            pltpu.VMEM((seq, LANES), F32),
            pltpu.VMEM((3, 2 * _QB, 2 * _QB), F32),
        ],
        compiler_params=_cparams(("arbitrary", "arbitrary")),
        name="dilated_attn",
    )(za, za, za)


def _gqa_kernel(q_ref, k_ref, v_ref, o_ref):
    tq = q_ref.shape[1]
    lane = lax.broadcasted_iota(jnp.int32, (1, LANES), 1)
    head0 = lane < HEAD_DIM
    kk = k_ref[0]
    zero = jnp.zeros((), BF16)
    v_aug = jnp.concatenate([v_ref[0], jnp.ones_like(v_ref[0])], axis=-1)
    scores = []
    for j in range(B_Q_WIDTH // LANES):
        qj = q_ref[0, :, j * LANES:(j + 1) * LANES]
        q2 = jnp.concatenate([jnp.where(head0, qj, zero), jnp.where(head0, zero, qj)], axis=0)
        scores.append(lax.dot_general(q2, kk, (((1,), (1,)), ((), ())),
                                      preferred_element_type=F32))
    for j, s in enumerate(scores):
        m = jnp.max(s, axis=-1, keepdims=True)
        pb = jnp.exp2((s - m).astype(BF16))
        oa = jnp.dot(pb, v_aug, preferred_element_type=F32)
        pv = oa[:, :LANES] / oa[:, LANES:]
        o_ref[0, :, j * LANES:(j + 1) * LANES] = jnp.where(
            head0, pv[:tq], pv[tq:]).astype(o_ref.dtype)


def _gqa(qb, kb, vb, tq):
    bsz, seq, _ = qb.shape
    kv_spec = pl.BlockSpec((1, seq, B_KV_WIDTH), lambda b, i: (b, 0, 0))
    return pl.pallas_call(
        _gqa_kernel,
        out_shape=jax.ShapeDtypeStruct((bsz, seq, B_Q_WIDTH), BF16),
        grid=(bsz, seq // tq),
        in_specs=[pl.BlockSpec((1, tq, B_Q_WIDTH), lambda b, i: (b, i, 0)), kv_spec, kv_spec],
        out_specs=pl.BlockSpec((1, tq, B_Q_WIDTH), lambda b, i: (b, i, 0)),
        compiler_params=_cparams(("arbitrary", "arbitrary")),
        name="gqa_attn",
    )(qb, kb, vb)


def _merge_kernel(x_ref, ya_ref, yb_ref, ga_ref, gb_ref, wpa_ref, wpb_ref, wo_ref,
                  gt_ref, g_ref, sh_ref, sc_ref, wr_ref,
                  xn_ref, hrow_ref, e_ref, w_ref):
    tm = x_ref.shape[0]
    pa = jnp.dot(ya_ref[...], wpa_ref[...], preferred_element_type=F32)
    pb = jnp.dot(yb_ref[...], wpb_ref[...], preferred_element_type=F32)
    merged = ga_ref[...].astype(F32) * pa + gb_ref[...].astype(F32) * pb
    out = jnp.dot(merged.astype(BF16), wo_ref[...], preferred_element_type=F32)
    xn = x_ref[...] + gt_ref[0] * out
    xn_ref[...] = xn

    ms = jnp.mean(xn * xn, axis=-1, keepdims=True)
    h = xn * lax.rsqrt(ms + NORM_EPS) * g_ref[...]
    h = h * (1.0 + sc_ref[0]) + sh_ref[0]
    for c in range(ROW_CHUNKS):
        hrow_ref[pl.ds(c, tm, stride=ROW_CHUNKS), :] = h[:, c * LANES:(c + 1) * LANES]

    logits = lax.dot_general(wr_ref[...], h, (((1,), (1,)), ((), ())),
                             preferred_element_type=F32,
                             precision=lax.Precision.HIGHEST)
    mx = jnp.max(logits, axis=0, keepdims=True)
    ex = jnp.exp(logits - mx)
    probs = ex / jnp.sum(ex, axis=0, keepdims=True)
    pg = probs.reshape(N_GROUPS, EXPERTS_PER_GROUP, tm)
    sub = lax.broadcasted_iota(jnp.int32, pg.shape, 1).astype(F32)
    m1 = jnp.max(pg, axis=1, keepdims=True)
    i1 = jnp.min(jnp.where(pg == m1, sub, float(EXPERTS_PER_GROUP)), axis=1, keepdims=True)
    pg2 = jnp.where(sub == i1, -1.0, pg)
    m2 = jnp.max(pg2, axis=1, keepdims=True)
    i2 = jnp.min(jnp.where(pg2 == m2, sub, float(EXPERTS_PER_GROUP)), axis=1, keepdims=True)
    score = m1 + m2
    gid = lax.broadcasted_iota(jnp.int32, score.shape, 0).astype(F32)
    best = jnp.max(score, axis=0, keepdims=True)
    gsel = jnp.min(jnp.where(score == best, gid, float(N_GROUPS)), axis=0, keepdims=True)
    pick = gid == gsel
    w0 = jnp.sum(jnp.where(pick, m1, 0.0), axis=0)
    w1 = jnp.sum(jnp.where(pick, m2, 0.0), axis=0)
    j0 = jnp.sum(jnp.where(pick, i1, 0.0), axis=0)
    j1 = jnp.sum(jnp.where(pick, i2, 0.0), axis=0)
    base = gsel[0] * float(EXPERTS_PER_GROUP)
    tot = w0 + w1
    e_ref[0, 0:1, :] = (base + j0).astype(jnp.int32)
    e_ref[0, 1:2, :] = (base + j1).astype(jnp.int32)
    w_ref[0, 0:1, :] = w0 / tot
    w_ref[0, 1:2, :] = w1 / tot


def _merge(x, ya, yb, ga, gb, wpa, wpb, wo, gt, g, sh, sc, wr_t, tm, tiles_per_seq):
    t, d = x.shape
    nt = t // tm
    row = lambda i: (i, 0)
    per_b = lambda i: (i // tiles_per_seq, 0, 0)
    const2 = lambda i: (0, 0)
    return pl.pallas_call(
        _merge_kernel,
        out_shape=(
            jax.ShapeDtypeStruct((t, d), F32),
            jax.ShapeDtypeStruct((t * ROW_CHUNKS, LANES), F32),
            jax.ShapeDtypeStruct((nt, TOP_K, tm), jnp.int32),
            jax.ShapeDtypeStruct((nt, TOP_K, tm), F32),
        ),
        grid=(nt,),
        in_specs=[
            pl.BlockSpec((tm, d), row),
            pl.BlockSpec((tm, A_WIDTH), row),
            pl.BlockSpec((tm, B_Q_WIDTH), row),
            pl.BlockSpec((tm, d), row),
            pl.BlockSpec((tm, d), row),
            pl.BlockSpec((A_WIDTH, d), const2),
            pl.BlockSpec((B_Q_WIDTH, d), const2),
            pl.BlockSpec((d, d), const2),
            pl.BlockSpec((1, 1, d), per_b),
            pl.BlockSpec((1, d), const2),
            pl.BlockSpec((1, 1, d), per_b),
            pl.BlockSpec((1, 1, d), per_b),
            pl.BlockSpec((N_EXPERTS, d), const2),
        ],
        out_specs=(
            pl.BlockSpec((tm, d), row),
            pl.BlockSpec((tm * ROW_CHUNKS, LANES), row),
            pl.BlockSpec((1, TOP_K, tm), lambda i: (i, 0, 0)),
            pl.BlockSpec((1, TOP_K, tm), lambda i: (i, 0, 0)),
        ),
        compiler_params=_cparams(("arbitrary",)),
        name="merge_router",
    )(x, ya, yb, ga, gb, wpa, wpb, wo, gt, g, sh, sc, wr_t)


_ROW_UNROLL = 8
_STEP_ROWS = 2 * MOE_BLOCK
_FFN_ROWS = (64, MOE_BLOCK, MOE_BLOCK + 64, _STEP_ROWS)


def _expert_kernel(step_e_ref, step_cnt_ref, step_base_ref, tok_ref, wrow_ref, h_hbm,
                   wg_ref, wu_ref, wd_ref, acc_hbm, xs, acc, gbuf, ybuf, sem):
    g = pl.program_id(0)
    j = pl.program_id(1)
    n_groups = pl.num_programs(0)
    n_steps = pl.num_programs(1)
    group_len = xs.shape[0]

    def group_rows(gi):
        return pl.ds(pl.multiple_of(gi * group_len, ROW_CHUNKS), group_len)

    def store_copy(gi):
        return pltpu.make_async_copy(acc.at[pl.ds(0, group_len), :],
                                     acc_hbm.at[group_rows(gi), :], sem.at[1])

    @pl.when(j == 0)
    def _():
        load = pltpu.make_async_copy(h_hbm.at[group_rows(g), :], xs, sem.at[0])
        load.start()

        @pl.when(g > 0)
        def _():
            store_copy(g - 1).wait()
        acc[...] = jnp.zeros_like(acc)

        @pl.when(g == 0)
        def _():
            gbuf[...] = jnp.zeros_like(gbuf)
        load.wait()

    step = g * n_steps + j
    pl.when(step_cnt_ref[step] > 0)(functools.partial(
        _expert_block, step_cnt_ref[step], step_base_ref[step], tok_ref, wrow_ref,
        wg_ref, wu_ref, wd_ref, xs, acc, gbuf, ybuf))

    @pl.when(j == n_steps - 1)
    def _():
        store_copy(g).start()

        @pl.when(g == n_groups - 1)
        def _():
            store_copy(g).wait()


def _expert_block(cnt, first, tok_ref, wrow_ref, wg_ref, wu_ref, wd_ref, xs, acc, gbuf, ybuf):
    group_len = xs.shape[0]

    def tile_at(row):
        return pl.ds(pl.multiple_of(row, ROW_CHUNKS), ROW_CHUNKS)

    nfull = cnt // _ROW_UNROLL
    tail = cnt - nfull * _ROW_UNROLL

    def gather(c, carry):
        base = c * _ROW_UNROLL
        for u in range(_ROW_UNROLL):
            gbuf[tile_at((base + u) * ROW_CHUNKS), :] = xs[
                tile_at(tok_ref[0, 0, first + base + u]), :]
        return carry
    lax.fori_loop(0, (cnt + _ROW_UNROLL - 1) // _ROW_UNROLL, gather, 0)

    def ffn(rows):
        xb = jnp.concatenate(
            [gbuf[pl.ds(c, rows, stride=ROW_CHUNKS), :] for c in range(ROW_CHUNKS)],
            axis=-1).astype(BF16)
        gate = jnp.dot(xb, wg_ref[0], preferred_element_type=F32)
        up = jnp.dot(xb, wu_ref[0], preferred_element_type=F32)
        hid = (gate * jax.nn.sigmoid(gate) * up).astype(BF16)
        y = jnp.dot(hid, wd_ref[0], preferred_element_type=F32)
        for c in range(ROW_CHUNKS):
            ybuf[pl.ds(c, rows, stride=ROW_CHUNKS), :] = y[:, c * LANES:(c + 1) * LANES]

    lo = 0
    for rows in _FFN_ROWS:
        pl.when((cnt > lo) & (cnt <= rows))(functools.partial(ffn, rows))
        lo = rows

    def scatter_chunk(base, n_real):
        new = []
        for u in range(_ROW_UNROLL):
            dst_row = tok_ref[0, 0, first + base + u]
            wgt = wrow_ref[0, 0, first + base + u]
            if n_real is not None:
                dst_row = jnp.where(u < n_real, dst_row, group_len)
                wgt = jnp.where(u < n_real, wgt, 0.0)
            dst = tile_at(dst_row)
            new.append((dst, acc[dst, :] + wgt * ybuf[tile_at((base + u) * ROW_CHUNKS), :]))
        for dst, val in new:
            acc[dst, :] = val

    def scatter_add(c, carry):
        scatter_chunk(c * _ROW_UNROLL, None)
        return carry
    lax.fori_loop(0, nfull, scatter_add, 0)

    @pl.when(tail > 0)
    def _():
        scatter_chunk(nfull * _ROW_UNROLL, tail)


def _experts(hrows, plan, wg, wu, wd, ts):
    step_e, step_cnt, step_first, tok, wrow = plan
    n_groups, list_len = tok.shape[0], tok.shape[2]
    n_steps = step_e.shape[0] // n_groups
    d = D_MODEL
    idx_blk = (1, 1, list_len)
    idx_map = lambda g, j, se, sc, sf: (g, 0, 0)
    w_map = lambda g, j, se, sc, sf: (se[g * n_steps + j], 0, 0)
    grid_spec = pltpu.PrefetchScalarGridSpec(
        num_scalar_prefetch=3,
        grid=(n_groups, n_steps),
        in_specs=[
            pl.BlockSpec(idx_blk, idx_map, memory_space=pltpu.SMEM),
            pl.BlockSpec(idx_blk, idx_map, memory_space=pltpu.SMEM),
            pl.BlockSpec(memory_space=pl.ANY),
            pl.BlockSpec((1, d, D_FF_EXPERT), w_map),
            pl.BlockSpec((1, d, D_FF_EXPERT), w_map),
            pl.BlockSpec((1, D_FF_EXPERT, d), w_map),
        ],
        out_specs=pl.BlockSpec(memory_space=pl.ANY),
        scratch_shapes=[
            pltpu.VMEM((ts * ROW_CHUNKS, LANES), F32),
            pltpu.VMEM(((ts + 1) * ROW_CHUNKS, LANES), F32),
            pltpu.VMEM((_STEP_ROWS * ROW_CHUNKS, LANES), F32),
            pltpu.VMEM((_STEP_ROWS * ROW_CHUNKS, LANES), F32),
            pltpu.SemaphoreType.DMA((2,)),
        ],
    )
    return pl.pallas_call(
        _expert_kernel,
        out_shape=jax.ShapeDtypeStruct(hrows.shape, F32),
        grid_spec=grid_spec,
        compiler_params=_cparams(("arbitrary", "arbitrary")),
        name="experts",
    )(step_e, step_cnt, step_first, tok, wrow, hrows, wg, wu, wd)


def _combine_kernel(x_ref, y_ref, gt_ref, o_ref):
    tm = x_ref.shape[0]
    gt = gt_ref[0]
    for c in range(ROW_CHUNKS):
        cols = slice(c * LANES, (c + 1) * LANES)
        o_ref[:, cols] = x_ref[:, cols] + gt[:, cols] * y_ref[pl.ds(c, tm, stride=ROW_CHUNKS), :]


def _combine(xn, yrows, gt, tm, tiles_per_seq):
    t, d = xn.shape
    nt = t // tm
    return pl.pallas_call(
        _combine_kernel,
        out_shape=jax.ShapeDtypeStruct((t, d), F32),
        grid=(nt,),
        in_specs=[
            pl.BlockSpec((tm, d), lambda i: (i, 0)),
            pl.BlockSpec((tm * ROW_CHUNKS, LANES), lambda i: (i, 0)),
            pl.BlockSpec((1, 1, d), lambda i: (i // tiles_per_seq, 0, 0)),
        ],
        out_specs=pl.BlockSpec((tm, d), lambda i: (i, 0)),
        compiler_params=_cparams(("arbitrary",)),
        name="moe_combine",
    )(xn, yrows, gt)


def _dispatch_plan(e_sel, w_sel, ts):
    t = e_sel.shape[0]
    n_groups = t // ts
    na = ts * TOP_K
    n_steps = na // _STEP_ROWS + N_EXPERTS
    e_flat = e_sel.reshape(n_groups, ts, TOP_K).transpose(0, 2, 1).reshape(n_groups, na)
    w_flat = w_sel.reshape(n_groups, ts, TOP_K).transpose(0, 2, 1).reshape(n_groups, na)
    a_ids = jnp.broadcast_to(jnp.arange(na, dtype=jnp.int32)[None, :], (n_groups, na))
    _, a_sorted, w_sorted = lax.sort((e_flat, a_ids, w_flat), dimension=1, num_keys=1)
    experts = jnp.arange(N_EXPERTS, dtype=jnp.int32)
    counts = jnp.sum((e_flat[:, :, None] == experts[None, None, :]).astype(jnp.int32), axis=1)
    padded = (counts + _STEP_ROWS - 1) // _STEP_ROWS * _STEP_ROWS
    pad_end = jnp.cumsum(padded, axis=1)
    pad_start = pad_end - padded
    start = jnp.cumsum(counts, axis=1) - counts
    row0 = jnp.arange(n_steps, dtype=jnp.int32) * _STEP_ROWS
    step_e = jnp.minimum(
        jnp.sum((row0[None, :, None] >= pad_end[:, None, :]).astype(jnp.int32), axis=2),
        N_EXPERTS - 1)
    is_e = step_e[:, :, None] == experts[None, None, :]
    take = lambda table: jnp.sum(jnp.where(is_e, table[:, None, :], 0), axis=2)
    off = row0[None, :] - take(pad_start)
    step_cnt = jnp.clip(take(counts) - off, 0, _STEP_ROWS)
    step_first = jnp.clip(take(start) + off, 0, na - 1)
    flat = lambda v: v.astype(jnp.int32).reshape(n_groups * n_steps)
    filler = jnp.zeros((n_groups, LANES), jnp.int32)
    tok_rows = jnp.concatenate([(a_sorted % ts) * ROW_CHUNKS, filler], axis=1).astype(jnp.int32)
    w_list = jnp.concatenate([w_sorted.astype(F32), filler.astype(F32)], axis=1)
    return (flat(step_e), flat(step_cnt), flat(step_first),
            tok_rows.reshape(n_groups, 1, na + LANES), w_list.reshape(n_groups, 1, na + LANES))


def _tile_sizes(bsz, seq):
    tm = min(512, seq)
    tm_merge = min(1024, seq)
    tq = min(256, seq)
    ts = min(4096, bsz * seq)
    assert seq % tm == 0 and seq % tm_merge == 0 and seq % tq == 0
    assert (bsz * seq) % ts == 0 and ts % tm_merge == 0
    return tm, tm_merge, tq, ts


def kernel(x, c, w_ada, b_ada, g_mix, w_in, qn_a, kn_a, qn_b, kn_b, w_pa, w_pb, w_out,
           g_ffn, w_router, w_gate, w_up, w_down):
    bsz, seq, d = x.shape
    depth = w_ada.shape[0]
    t = bsz * seq
    tm, tm_merge, tq, ts = _tile_sizes(bsz, seq)
    tiles_per_seq = seq // tm

    mod = _adaln(c, w_ada, b_ada)
    tables = _rope_tables(seq)
    bd_heads = 2 * LANES // HEAD_DIM
    bd = jnp.asarray(np.kron(np.eye(bd_heads), np.ones((HEAD_DIM, HEAD_DIM))), BF16)

    g4 = B_Q_HEADS // B_KV_HEADS
    head_perm = np.concatenate(
        [np.r_[np.arange(j * HEAD_DIM, (j + 1) * HEAD_DIM),
               np.arange((j + g4) * HEAD_DIM, (j + g4 + 1) * HEAD_DIM)] for j in range(g4)])
    qb_lo = 3 * A_WIDTH
    col_perm = np.arange(w_in.shape[2])
    col_perm[qb_lo:qb_lo + B_Q_WIDTH] = qb_lo + head_perm
    scale = HEAD_DIM ** -0.5 * float(np.log2(np.e))
    wr_t = w_router.T

    xf = x
    prev_moe = None
    for l in range(depth):
        sh1, sc1, gt1, sh2, sc2, gt2 = [
            mod[l, :, i * d:(i + 1) * d].reshape(bsz, 1, d) for i in range(6)]
        w_in_l = w_in[l][:, col_perm].astype(BF16)
        gains = (
            jnp.tile(qn_a[l] * scale, A_HEADS).reshape(1, A_WIDTH),
            jnp.tile(kn_a[l], A_HEADS).reshape(1, A_WIDTH),
            jnp.tile(qn_b[l] * scale, B_Q_HEADS).reshape(1, B_Q_WIDTH),
            jnp.tile(kn_b[l], B_KV_HEADS).reshape(1, B_KV_WIDTH),
        )
        outs = _in_proj(xf, prev_moe, sh1, sc1, g_mix[l].reshape(1, d), w_in_l, bd, gains,
                        tables, tm)
        za, qb, kb, vb, ga, gb = outs[:6]
        if prev_moe is not None:
            xf = outs[6]
        ya = _dilated(za)
        yb = _gqa(qb, kb, vb, tq)
        xn, hrows, e_sel, w_sel = _merge(
            xf.reshape(t, d), ya.reshape(t, A_WIDTH), yb.reshape(t, B_Q_WIDTH),
            ga.reshape(t, d), gb.reshape(t, d),
            w_pa[l].astype(BF16), w_pb[l][head_perm].astype(BF16), w_out[l].astype(BF16),
            gt1, g_ffn[l].reshape(1, d), sh2, sc2, wr_t, tm_merge, seq // tm_merge)
        e_tok = e_sel.transpose(0, 2, 1).reshape(t, TOP_K)
        w_tok = w_sel.transpose(0, 2, 1).reshape(t, TOP_K)
        plan = _dispatch_plan(e_tok, w_tok, ts)
        yrows = _experts(hrows, plan, w_gate[l].astype(BF16), w_up[l].astype(BF16),
                         w_down[l].astype(BF16), ts)
        xf = xn.reshape(bsz, seq, d)
        prev_moe = (yrows, gt2)
    out = _combine(xf.reshape(t, d), prev_moe[0], prev_moe[1], tm, tiles_per_seq)
    return out.reshape(bsz, seq, d)
```

```python
import functools

import jax
import jax.numpy as jnp
import numpy as np
from jax import lax
from jax.experimental import pallas as pl
from jax.experimental.pallas import tpu as pltpu

D_MODEL = 1024
HEAD_DIM = 64
NORM_EPS = 1e-6
A_HEADS = 8
A_WIDTH = A_HEADS * HEAD_DIM
A_PATTERNS = ((128, 1), (512, 4), (2048, 16))
A_ROT_DIMS = HEAD_DIM // 4
A_ROPE_THETA = 500000.0
B_Q_HEADS = 8
B_KV_HEADS = 2
B_Q_WIDTH = B_Q_HEADS * HEAD_DIM
B_KV_WIDTH = B_KV_HEADS * HEAD_DIM
B_ROPE_THETA = 10000.0
GRID_W = 64
N_EXPERTS = 32
N_GROUPS = 4
EXPERTS_PER_GROUP = N_EXPERTS // N_GROUPS
TOP_K = 2
D_FF_EXPERT = D_MODEL // 2
MOE_BLOCK = 256

LANES = 128
SUBLANES = 8
ROW_CHUNKS = D_MODEL // LANES
assert ROW_CHUNKS == SUBLANES
VMEM_LIMIT = 56 * 1024 * 1024

F32 = jnp.float32
BF16 = jnp.bfloat16


def _cparams(sem):
    return pltpu.CompilerParams(dimension_semantics=sem, vmem_limit_bytes=VMEM_LIMIT)


def _adaln_kernel(c_ref, w_ref, b_ref, o_ref):
    c = c_ref[...]
    cond = c * jax.nn.sigmoid(c)
    o_ref[0] = jnp.dot(cond, w_ref[0], preferred_element_type=F32,
                       precision=lax.Precision.HIGHEST) + b_ref[0]


def _adaln(c, w_ada, b_ada):
    depth, d, n = w_ada.shape
    bsz = c.shape[0]
    tn = 1024
    return pl.pallas_call(
        _adaln_kernel,
        out_shape=jax.ShapeDtypeStruct((depth, bsz, n), F32),
        grid=(depth, n // tn),
        in_specs=[
            pl.BlockSpec((bsz, d), lambda l, j: (0, 0)),
            pl.BlockSpec((1, d, tn), lambda l, j: (l, 0, j)),
            pl.BlockSpec((1, 1, tn), lambda l, j: (l, 0, j)),
        ],
        out_specs=pl.BlockSpec((1, bsz, tn), lambda l, j: (l, 0, j)),
        compiler_params=_cparams(("arbitrary", "arbitrary")),
        name="adaln",
    )(c, w_ada, b_ada.reshape(depth, 1, n))


def _rope_tables(seq):
    pos = jnp.arange(seq, dtype=F32)
    row = jnp.floor(pos / GRID_W)
    col = pos - row * GRID_W
    d = np.arange(LANES) % HEAD_DIM

    def build(segments):
        c = jnp.ones((seq, LANES), F32)
        s1 = jnp.zeros((seq, LANES), F32)
        s2 = jnp.zeros((seq, LANES), F32)
        for lo, half, theta, p in segments:
            first = (d >= lo) & (d < lo + half)
            second = (d >= lo + half) & (d < lo + 2 * half)
            idx = np.where(first, d - lo, np.where(second, d - lo - half, 0))
            freqs = theta ** (-jnp.arange(half, dtype=F32) / half)
            ang = p[:, None] * freqs[idx][None, :]
            cs, sn = jnp.cos(ang), jnp.sin(ang)
            rot = jnp.asarray(first | second)[None, :]
            c = jnp.where(rot, cs, c)
            s1 = jnp.where(jnp.asarray(first)[None, :], -sn, s1)
            s2 = jnp.where(jnp.asarray(second)[None, :], sn, s2)
        return c, s1, s2

    ta = build([(0, A_ROT_DIMS // 2, A_ROPE_THETA, pos)])
    q = HEAD_DIM // 4
    tb = build([(0, q, B_ROPE_THETA, row), (2 * q, q, B_ROPE_THETA, col)])
    return ta + tb


def _in_proj_kernel(prev_moe, *refs):
    if prev_moe:
        xn_ref, y_ref, gtp_ref, *refs = refs
        xo_ref = refs.pop()
        tm = xn_ref.shape[1]
        moe = jnp.concatenate([y_ref[pl.ds(c, tm, stride=ROW_CHUNKS), :]
                               for c in range(ROW_CHUNKS)], axis=-1)
        x = xn_ref[0] + gtp_ref[0] * moe
        xo_ref[0] = x
    else:
        x_ref, *refs = refs
        x = x_ref[0]
    (sh_ref, sc_ref, g_ref, w_ref, bd_ref, gqa_ref, gka_ref, gqb_ref, gkb_ref,
     ca_ref, s1a_ref, s2a_ref, cb_ref, s1b_ref, s2b_ref,
     za_ref, qb_ref, kb_ref, vb_ref, ga_ref, gb_ref) = refs
    ms = jnp.mean(x * x, axis=-1, keepdims=True)
    h = x * lax.rsqrt(ms + NORM_EPS) * g_ref[...]
    h = h * (1.0 + sc_ref[0]) + sh_ref[0]
    hb = h.astype(BF16)

    def seg(lo, width):
        return jnp.dot(hb, w_ref[:, lo:lo + width], preferred_element_type=F32)

    def qk_norm(z, gain_ref):
        width = z.shape[-1]
        bdw = min(width, bd_ref.shape[0])
        sq = (z * z).astype(BF16)
        parts = [jnp.dot(sq[:, lo:lo + bdw], bd_ref[:bdw, :bdw], preferred_element_type=F32)
                 for lo in range(0, width, bdw)]
        ss = parts[0] if len(parts) == 1 else jnp.concatenate(parts, axis=-1)
        return z * lax.rsqrt(ss * (1.0 / HEAD_DIM) + NORM_EPS) * gain_ref[...]

    def tile(t, width):
        reps = width // LANES
        return t if reps == 1 else jnp.concatenate([t] * reps, axis=-1)

    def rope(z, c_ref, s1_ref, s2_ref, half):
        width = z.shape[-1]
        up = pltpu.roll(z, width - half, 1)
        dn = pltpu.roll(z, half, 1)
        return (z * tile(c_ref[...], width) + up * tile(s1_ref[...], width)
                + dn * tile(s2_ref[...], width))

    ha = A_ROT_DIMS // 2
    hq = HEAD_DIM // 4
    o = 0
    qa = rope(qk_norm(seg(o, A_WIDTH), gqa_ref), ca_ref, s1a_ref, s2a_ref, ha)
    za_ref[0, :, 0:A_WIDTH] = qa
    o += A_WIDTH
    ka = rope(qk_norm(seg(o, A_WIDTH), gka_ref), ca_ref, s1a_ref, s2a_ref, ha)
    za_ref[0, :, A_WIDTH:2 * A_WIDTH] = ka
    o += A_WIDTH
    za_ref[0, :, 2 * A_WIDTH:3 * A_WIDTH] = seg(o, A_WIDTH)
    o += A_WIDTH
    qb = rope(qk_norm(seg(o, B_Q_WIDTH), gqb_ref), cb_ref, s1b_ref, s2b_ref, hq)
    qb_ref[0] = qb.astype(BF16)
    o += B_Q_WIDTH
    kb = rope(qk_norm(seg(o, B_KV_WIDTH), gkb_ref), cb_ref, s1b_ref, s2b_ref, hq)
    kb_ref[0] = kb.astype(BF16)
    o += B_KV_WIDTH
    vb_ref[0] = seg(o, B_KV_WIDTH).astype(BF16)
    o += B_KV_WIDTH
    ga_ref[0] = jax.nn.sigmoid(seg(o, D_MODEL)).astype(BF16)
    o += D_MODEL
    gb_ref[0] = jax.nn.sigmoid(seg(o, D_MODEL)).astype(BF16)


def _in_proj(x, prev_moe, sh, sc, g, w_in, bd, gains, tables, tm):
    bsz, seq, d = x.shape
    n_in = w_in.shape[1]
    nst = seq // tm
    row = lambda st, b: (b, st, 0)
    per_b = lambda st, b: (b, 0, 0)
    const2 = lambda st, b: (0, 0)
    tab = pl.BlockSpec((tm, LANES), lambda st, b: (st, 0))
    x_specs = [pl.BlockSpec((1, tm, d), row)]
    x_args = [x]
    extra_shape, extra_spec = (), ()
    if prev_moe is not None:
        x_specs += [pl.BlockSpec((tm * ROW_CHUNKS, LANES), lambda st, b: (b * nst + st, 0)),
                    pl.BlockSpec((1, 1, d), per_b)]
        x_args += list(prev_moe)
        extra_shape = (jax.ShapeDtypeStruct((bsz, seq, d), F32),)
        extra_spec = (pl.BlockSpec((1, tm, d), row),)
    return pl.pallas_call(
        functools.partial(_in_proj_kernel, prev_moe is not None),
        out_shape=(
            jax.ShapeDtypeStruct((bsz, seq, 3 * A_WIDTH), F32),
            jax.ShapeDtypeStruct((bsz, seq, B_Q_WIDTH), BF16),
            jax.ShapeDtypeStruct((bsz, seq, B_KV_WIDTH), BF16),
            jax.ShapeDtypeStruct((bsz, seq, B_KV_WIDTH), BF16),
            jax.ShapeDtypeStruct((bsz, seq, d), BF16),
            jax.ShapeDtypeStruct((bsz, seq, d), BF16),
        ) + extra_shape,
        grid=(nst, bsz),
        in_specs=x_specs + [
            pl.BlockSpec((1, 1, d), per_b),
            pl.BlockSpec((1, 1, d), per_b),
            pl.BlockSpec((1, d), const2),
            pl.BlockSpec((d, n_in), const2),
            pl.BlockSpec(bd.shape, const2),
            pl.BlockSpec((1, A_WIDTH), const2),
            pl.BlockSpec((1, A_WIDTH), const2),
            pl.BlockSpec((1, B_Q_WIDTH), const2),
            pl.BlockSpec((1, B_KV_WIDTH), const2),
            tab, tab, tab, tab, tab, tab,
        ],
        out_specs=(
            pl.BlockSpec((1, tm, 3 * A_WIDTH), row),
            pl.BlockSpec((1, tm, B_Q_WIDTH), row),
            pl.BlockSpec((1, tm, B_KV_WIDTH), row),
            pl.BlockSpec((1, tm, B_KV_WIDTH), row),
            pl.BlockSpec((1, tm, d), row),
            pl.BlockSpec((1, tm, d), row),
        ) + extra_spec,
        compiler_params=_cparams(("arbitrary", "arbitrary")),
        name="in_proj",
    )(*x_args, sh, sc, g, w_in, bd, *gains, *tables)


_QB = 128
_UNROLL = 16


def _dilated_kernel(q_ref, k_ref, v_ref, o_ref, m_scr, l_scr, acc_scr, bias_scr):
    seq = q_ref.shape[1]
    lane = lax.broadcasted_iota(jnp.int32, (1, LANES), 1)
    head0 = lane < HEAD_DIM
    radius = A_PATTERNS[0][0] // (2 * A_PATTERNS[0][1])
    assert all(w // (2 * d) == radius for w, d in A_PATTERNS) and 2 * radius == _QB

    @pl.when((pl.program_id(0) == 0) & (pl.program_id(1) == 0))
    def _():
        qrow = lax.broadcasted_iota(jnp.int32, (2 * _QB, 2 * _QB), 0) & (_QB - 1)
        kcol = lax.broadcasted_iota(jnp.int32, (2 * _QB, 2 * _QB), 1)
        for i in range(3):
            bias_scr[i] = jnp.where(jnp.abs(kcol - qrow - radius * i) <= radius, 0.0, -jnp.inf)

    def merge(t):
        return jnp.where(head0, t[:_QB], t[_QB:])

    patterns = sorted(A_PATTERNS, key=lambda wd: -wd[1])
    assert patterns[-1][1] == 1 and seq % (patterns[0][1] * _QB) == 0
    for pi, (window, dil) in enumerate(patterns):
        sub_len = seq // dil
        kw = min(2 * _QB, sub_len)
        nqb = sub_len // _QB
        nblk = dil * nqb

        def wide(t, kw=kw):
            return t if kw == LANES else jnp.concatenate([t] * (kw // LANES), axis=-1)

        def load(blk, pi=pi, dil=dil, sub_len=sub_len, kw=kw, nqb=nqb):
            r = blk // nqb
            i0q = (blk % nqb) * _QB
            i0k = jnp.clip(i0q - (kw - _QB) // 2, 0, sub_len - kw)
            if dil == 1:
                qrows = pl.ds(pl.multiple_of(i0q, _QB), _QB)
                krows = pl.ds(pl.multiple_of(i0k, SUBLANES), kw)
            else:
                qrows = pl.ds(r + dil * i0q, _QB, stride=dil)
                krows = pl.ds(r + dil * i0k, kw, stride=dil)
            q = q_ref[0, qrows, :]
            kk = k_ref[0, krows, :].astype(BF16)
            vv = v_ref[0, krows, :].astype(BF16)
            old = None
            if pi > 0:
                old = (jnp.concatenate([m_scr[0, qrows, :], m_scr[1, qrows, :]], axis=0),
                       l_scr[qrows, :], acc_scr[qrows, :])
            return qrows, (i0q - i0k) // radius, q, kk, vv, old

        def compute(mask_id, q, kk, vv, old, kw=kw, wide=wide):
            bias = bias_scr[0, :, 0:LANES] if kw == LANES else bias_scr[mask_id]
            q2 = jnp.concatenate([jnp.where(head0, q, 0.0), jnp.where(head0, 0.0, q)],
                                 axis=0).astype(BF16)
            s = lax.dot_general(q2, kk, (((1,), (1,)), ((), ())),
                                preferred_element_type=F32) + bias
            mb = jnp.broadcast_to(jnp.max(s, axis=-1, keepdims=True), (2 * _QB, LANES))
            m_new = mb if old is None else jnp.maximum(old[0], mb)
            p = jnp.exp2((s - wide(m_new)).astype(BF16))
            v_aug = jnp.concatenate([vv, jnp.ones_like(vv)], axis=-1)
            oa = jnp.dot(p, v_aug, preferred_element_type=F32)
            pv = merge(oa[:, :LANES])
            psum = merge(oa[:, LANES:])
            if old is None:
                return m_new, psum, pv
            alpha = jnp.exp2(merge(old[0]) - merge(m_new))
            return m_new, alpha * old[1] + psum, alpha * old[2] + pv

        def group(it, carry, load=load, compute=compute, last=pi == len(patterns) - 1):
            loaded = [load(it * _UNROLL + u) for u in range(_UNROLL)]
            results = [compute(*ld[1:]) for ld in loaded]
            for ld, (m_new, l_new, acc_new) in zip(loaded, results):
                qrows = ld[0]
                if last:
                    o_ref[0, qrows, :] = (acc_new / l_new).astype(o_ref.dtype)
                else:
                    m_scr[0, qrows, :] = m_new[:_QB]
                    m_scr[1, qrows, :] = m_new[_QB:]
                    l_scr[qrows, :] = l_new
                    acc_scr[qrows, :] = acc_new
            return carry

        lax.fori_loop(0, nblk // _UNROLL, group, 0)


def _dilated(za):
    bsz, seq, _ = za.shape
    nhp = A_WIDTH // LANES
    blk = (1, seq, LANES)
    return pl.pallas_call(
        _dilated_kernel,
        out_shape=jax.ShapeDtypeStruct((bsz, seq, A_WIDTH), BF16),
        grid=(bsz, nhp),
        in_specs=[
            pl.BlockSpec(blk, lambda b, j: (b, 0, j)),
            pl.BlockSpec(blk, lambda b, j: (b, 0, nhp + j)),
            pl.BlockSpec(blk, lambda b, j: (b, 0, 2 * nhp + j)),
        ],
        out_specs=pl.BlockSpec(blk, lambda b, j: (b, 0, j)),
        scratch_shapes=[
            pltpu.VMEM((2, seq, LANES), F32),
            pltpu.VMEM((seq, LANES), F32),
            pltpu.VMEM((seq, LANES), F32),
            pltpu.VMEM((3, 2 * _QB, 2 * _QB), F32),
        ],
        compiler_params=_cparams(("arbitrary", "arbitrary")),
        name="dilated_attn",
    )(za, za, za)


def _gqa_kernel(q_ref, k_ref, v_ref, o_ref):
    tq = q_ref.shape[1]
    lane = lax.broadcasted_iota(jnp.int32, (1, LANES), 1)
    head0 = lane < HEAD_DIM
    kk = k_ref[0]
    zero = jnp.zeros((), BF16)
    v_aug = jnp.concatenate([v_ref[0], jnp.ones_like(v_ref[0])], axis=-1)
    scores = []
    for j in range(B_Q_WIDTH // LANES):
        qj = q_ref[0, :, j * LANES:(j + 1) * LANES]
        q2 = jnp.concatenate([jnp.where(head0, qj, zero), jnp.where(head0, zero, qj)], axis=0)
        scores.append(lax.dot_general(q2, kk, (((1,), (1,)), ((), ())),
                                      preferred_element_type=F32))
    for j, s in enumerate(scores):
        m = jnp.max(s, axis=-1, keepdims=True)
        pb = jnp.exp2((s - m).astype(BF16))
        oa = jnp.dot(pb, v_aug, preferred_element_type=F32)
        pv = oa[:, :LANES] / oa[:, LANES:]
        o_ref[0, :, j * LANES:(j + 1) * LANES] = jnp.where(
            head0, pv[:tq], pv[tq:]).astype(o_ref.dtype)


def _gqa(qb, kb, vb, tq):
    bsz, seq, _ = qb.shape
    kv_spec = pl.BlockSpec((1, seq, B_KV_WIDTH), lambda b, i: (b, 0, 0))
    return pl.pallas_call(
        _gqa_kernel,
        out_shape=jax.ShapeDtypeStruct((bsz, seq, B_Q_WIDTH), BF16),
        grid=(bsz, seq // tq),
        in_specs=[pl.BlockSpec((1, tq, B_Q_WIDTH), lambda b, i: (b, i, 0)), kv_spec, kv_spec],
        out_specs=pl.BlockSpec((1, tq, B_Q_WIDTH), lambda b, i: (b, i, 0)),
        compiler_params=_cparams(("arbitrary", "arbitrary")),
        name="gqa_attn",
    )(qb, kb, vb)


def _merge_kernel(x_ref, ya_ref, yb_ref, ga_ref, gb_ref, wpa_ref, wpb_ref, wo_ref,
                  gt_ref, g_ref, sh_ref, sc_ref, wr_ref,
                  xn_ref, hrow_ref, e_ref, w_ref):
    tm = x_ref.shape[0]
    pa = jnp.dot(ya_ref[...], wpa_ref[...], preferred_element_type=F32)
    pb = jnp.dot(yb_ref[...], wpb_ref[...], preferred_element_type=F32)
    merged = ga_ref[...].astype(F32) * pa + gb_ref[...].astype(F32) * pb
    out = jnp.dot(merged.astype(BF16), wo_ref[...], preferred_element_type=F32)
    xn = x_ref[...] + gt_ref[0] * out
    xn_ref[...] = xn

    ms = jnp.mean(xn * xn, axis=-1, keepdims=True)
    h = xn * lax.rsqrt(ms + NORM_EPS) * g_ref[...]
    h = h * (1.0 + sc_ref[0]) + sh_ref[0]
    for c in range(ROW_CHUNKS):
        hrow_ref[pl.ds(c, tm, stride=ROW_CHUNKS), :] = h[:, c * LANES:(c + 1) * LANES]

    logits = lax.dot_general(wr_ref[...], h, (((1,), (1,)), ((), ())),
                             preferred_element_type=F32,
                             precision=lax.Precision.HIGHEST)
    mx = jnp.max(logits, axis=0, keepdims=True)
    ex = jnp.exp(logits - mx)
    probs = ex / jnp.sum(ex, axis=0, keepdims=True)
    pg = probs.reshape(N_GROUPS, EXPERTS_PER_GROUP, tm)
    sub = lax.broadcasted_iota(jnp.int32, pg.shape, 1).astype(F32)
    m1 = jnp.max(pg, axis=1, keepdims=True)
    i1 = jnp.min(jnp.where(pg == m1, sub, float(EXPERTS_PER_GROUP)), axis=1, keepdims=True)
    pg2 = jnp.where(sub == i1, -1.0, pg)
    m2 = jnp.max(pg2, axis=1, keepdims=True)
    i2 = jnp.min(jnp.where(pg2 == m2, sub, float(EXPERTS_PER_GROUP)), axis=1, keepdims=True)
    score = m1 + m2
    gid = lax.broadcasted_iota(jnp.int32, score.shape, 0).astype(F32)
    best = jnp.max(score, axis=0, keepdims=True)
    gsel = jnp.min(jnp.where(score == best, gid, float(N_GROUPS)), axis=0, keepdims=True)
    pick = gid == gsel
    w0 = jnp.sum(jnp.where(pick, m1, 0.0), axis=0)
    w1 = jnp.sum(jnp.where(pick, m2, 0.0), axis=0)
    j0 = jnp.sum(jnp.where(pick, i1, 0.0), axis=0)
    j1 = jnp.sum(jnp.where(pick, i2, 0.0), axis=0)
    base = gsel[0] * float(EXPERTS_PER_GROUP)
    tot = w0 + w1
    e_ref[0, 0:1, :] = (base + j0).astype(jnp.int32)
    e_ref[0, 1:2, :] = (base + j1).astype(jnp.int32)
    w_ref[0, 0:1, :] = w0 / tot
    w_ref[0, 1:2, :] = w1 / tot


def _merge(x, ya, yb, ga, gb, wpa, wpb, wo, gt, g, sh, sc, wr_t, tm, tiles_per_seq):
    t, d = x.shape
    nt = t // tm
    row = lambda i: (i, 0)
    per_b = lambda i: (i // tiles_per_seq, 0, 0)
    const2 = lambda i: (0, 0)
    return pl.pallas_call(
        _merge_kernel,
        out_shape=(
            jax.ShapeDtypeStruct((t, d), F32),
            jax.ShapeDtypeStruct((t * ROW_CHUNKS, LANES), F32),
            jax.ShapeDtypeStruct((nt, TOP_K, tm), jnp.int32),
            jax.ShapeDtypeStruct((nt, TOP_K, tm), F32),
        ),
        grid=(nt,),
        in_specs=[
            pl.BlockSpec((tm, d), row),
            pl.BlockSpec((tm, A_WIDTH), row),
            pl.BlockSpec((tm, B_Q_WIDTH), row),
            pl.BlockSpec((tm, d), row),
            pl.BlockSpec((tm, d), row),
            pl.BlockSpec((A_WIDTH, d), const2),
            pl.BlockSpec((B_Q_WIDTH, d), const2),
            pl.BlockSpec((d, d), const2),
            pl.BlockSpec((1, 1, d), per_b),
            pl.BlockSpec((1, d), const2),
            pl.BlockSpec((1, 1, d), per_b),
            pl.BlockSpec((1, 1, d), per_b),
            pl.BlockSpec((N_EXPERTS, d), const2),
        ],
        out_specs=(
            pl.BlockSpec((tm, d), row),
            pl.BlockSpec((tm * ROW_CHUNKS, LANES), row),
            pl.BlockSpec((1, TOP_K, tm), lambda i: (i, 0, 0)),
            pl.BlockSpec((1, TOP_K, tm), lambda i: (i, 0, 0)),
        ),
        compiler_params=_cparams(("arbitrary",)),
        name="merge_router",
    )(x, ya, yb, ga, gb, wpa, wpb, wo, gt, g, sh, sc, wr_t)


_ROW_UNROLL = 16
_STEP_ROWS = 2 * MOE_BLOCK
_FFN_ROWS = (64, MOE_BLOCK, MOE_BLOCK + 32, MOE_BLOCK + 64, _STEP_ROWS)


def _expert_kernel(step_e_ref, step_cnt_ref, step_base_ref, tok_ref, wrow_ref, h_hbm,
                   wg_ref, wu_ref, wd_ref, acc_hbm, xs, acc, gbuf, ybuf, sem):
    g = pl.program_id(0)
    j = pl.program_id(1)
    n_groups = pl.num_programs(0)
    n_steps = pl.num_programs(1)
    group_len = xs.shape[0]

    def group_rows(gi):
        return pl.ds(pl.multiple_of(gi * group_len, ROW_CHUNKS), group_len)

    def store_copy(gi):
        return pltpu.make_async_copy(acc.at[pl.ds(0, group_len), :],
                                     acc_hbm.at[group_rows(gi), :], sem.at[1])

    @pl.when(j == 0)
    def _():
        load = pltpu.make_async_copy(h_hbm.at[group_rows(g), :], xs, sem.at[0])
        load.start()

        @pl.when(g > 0)
        def _():
            store_copy(g - 1).wait()
        acc[...] = jnp.zeros_like(acc)

        @pl.when(g == 0)
        def _():
            gbuf[...] = jnp.zeros_like(gbuf)
        load.wait()

    step = g * n_steps + j
    pl.when(step_cnt_ref[step] > 0)(functools.partial(
        _expert_block, step_cnt_ref[step], step_base_ref[step], tok_ref, wrow_ref,
        wg_ref, wu_ref, wd_ref, xs, acc, gbuf, ybuf))

    @pl.when(j == n_steps - 1)
    def _():
        store_copy(g).start()

        @pl.when(g == n_groups - 1)
        def _():
            store_copy(g).wait()


def _expert_block(cnt, first, tok_ref, wrow_ref, wg_ref, wu_ref, wd_ref, xs, acc, gbuf, ybuf):
    group_len = xs.shape[0]

    def tile_at(row):
        return pl.ds(pl.multiple_of(row, ROW_CHUNKS), ROW_CHUNKS)

    nfull = cnt // _ROW_UNROLL
    tail = cnt - nfull * _ROW_UNROLL

    def gather(c, carry):
        base = c * _ROW_UNROLL
        for u in range(_ROW_UNROLL):
            gbuf[tile_at((base + u) * ROW_CHUNKS), :] = xs[
                tile_at(tok_ref[0, 0, first + base + u]), :]
        return carry
    lax.fori_loop(0, (cnt + _ROW_UNROLL - 1) // _ROW_UNROLL, gather, 0)

    def ffn(rows):
        xb = jnp.concatenate(
            [gbuf[pl.ds(c, rows, stride=ROW_CHUNKS), :] for c in range(ROW_CHUNKS)],
            axis=-1).astype(BF16)
        gate = jnp.dot(xb, wg_ref[0], preferred_element_type=F32)
        up = jnp.dot(xb, wu_ref[0], preferred_element_type=F32)
        hid = (gate * jax.nn.sigmoid(gate) * up).astype(BF16)
        y = jnp.dot(hid, wd_ref[0], preferred_element_type=F32)
        for c in range(ROW_CHUNKS):
            ybuf[pl.ds(c, rows, stride=ROW_CHUNKS), :] = y[:, c * LANES:(c + 1) * LANES]

    lo = 0
    for rows in _FFN_ROWS:
        pl.when((cnt > lo) & (cnt <= rows))(functools.partial(ffn, rows))
        lo = rows

    def scatter_chunk(base, n_real):
        new = []
        for u in range(_ROW_UNROLL):
            dst_row = tok_ref[0, 0, first + base + u]
            wgt = wrow_ref[0, 0, first + base + u]
            if n_real is not None:
                dst_row = jnp.where(u < n_real, dst_row, group_len)
                wgt = jnp.where(u < n_real, wgt, 0.0)
            dst = tile_at(dst_row)
            new.append((dst, acc[dst, :] + wgt * ybuf[tile_at((base + u) * ROW_CHUNKS), :]))
        for dst, val in new:
            acc[dst, :] = val

    def scatter_add(c, carry):
        scatter_chunk(c * _ROW_UNROLL, None)
        return carry
    lax.fori_loop(0, nfull, scatter_add, 0)

    @pl.when(tail > 0)
    def _():
        scatter_chunk(nfull * _ROW_UNROLL, tail)


def _experts(hrows, plan, wg, wu, wd, ts):
    step_e, step_cnt, step_first, tok, wrow = plan
    n_groups, list_len = tok.shape[0], tok.shape[2]
    n_steps = step_e.shape[0] // n_groups
    d = D_MODEL
    idx_blk = (1, 1, list_len)
    idx_map = lambda g, j, se, sc, sf: (g, 0, 0)
    w_map = lambda g, j, se, sc, sf: (se[g * n_steps + j], 0, 0)
    grid_spec = pltpu.PrefetchScalarGridSpec(
        num_scalar_prefetch=3,
        grid=(n_groups, n_steps),
        in_specs=[
            pl.BlockSpec(idx_blk, idx_map, memory_space=pltpu.SMEM),
            pl.BlockSpec(idx_blk, idx_map, memory_space=pltpu.SMEM),
            pl.BlockSpec(memory_space=pl.ANY),
            pl.BlockSpec((1, d, D_FF_EXPERT), w_map),
            pl.BlockSpec((1, d, D_FF_EXPERT), w_map),
            pl.BlockSpec((1, D_FF_EXPERT, d), w_map),
        ],
        out_specs=pl.BlockSpec(memory_space=pl.ANY),
        scratch_shapes=[
            pltpu.VMEM((ts * ROW_CHUNKS, LANES), F32),
            pltpu.VMEM(((ts + 1) * ROW_CHUNKS, LANES), F32),
            pltpu.VMEM((_STEP_ROWS * ROW_CHUNKS, LANES), F32),
            pltpu.VMEM((_STEP_ROWS * ROW_CHUNKS, LANES), F32),
            pltpu.SemaphoreType.DMA((2,)),
        ],
    )
    return pl.pallas_call(
        _expert_kernel,
        out_shape=jax.ShapeDtypeStruct(hrows.shape, F32),
        grid_spec=grid_spec,
        compiler_params=_cparams(("arbitrary", "arbitrary")),
        name="experts",
    )(step_e, step_cnt, step_first, tok, wrow, hrows, wg, wu, wd)


def _combine_kernel(x_ref, y_ref, gt_ref, o_ref):
    tm = x_ref.shape[0]
    gt = gt_ref[0]
    for c in range(ROW_CHUNKS):
        cols = slice(c * LANES, (c + 1) * LANES)
        o_ref[:, cols] = x_ref[:, cols] + gt[:, cols] * y_ref[pl.ds(c, tm, stride=ROW_CHUNKS), :]


def _combine(xn, yrows, gt, tm, tiles_per_seq):
    t, d = xn.shape
    nt = t // tm
    return pl.pallas_call(
        _combine_kernel,
        out_shape=jax.ShapeDtypeStruct((t, d), F32),
        grid=(nt,),
        in_specs=[
            pl.BlockSpec((tm, d), lambda i: (i, 0)),
            pl.BlockSpec((tm * ROW_CHUNKS, LANES), lambda i: (i, 0)),
            pl.BlockSpec((1, 1, d), lambda i: (i // tiles_per_seq, 0, 0)),
        ],
        out_specs=pl.BlockSpec((tm, d), lambda i: (i, 0)),
        compiler_params=_cparams(("arbitrary",)),
        name="moe_combine",
    )(xn, yrows, gt)


def _dispatch_plan(e_sel, w_sel, ts):
    t = e_sel.shape[0]
    n_groups = t // ts
    na = ts * TOP_K
    n_steps = na // _STEP_ROWS + N_EXPERTS
    e_flat = e_sel.reshape(n_groups, ts, TOP_K).transpose(0, 2, 1).reshape(n_groups, na)
    w_flat = w_sel.reshape(n_groups, ts, TOP_K).transpose(0, 2, 1).reshape(n_groups, na)
    a_ids = jnp.broadcast_to(jnp.arange(na, dtype=jnp.int32)[None, :], (n_groups, na))
    _, a_sorted, w_sorted = lax.sort((e_flat, a_ids, w_flat), dimension=1, num_keys=1)
    experts = jnp.arange(N_EXPERTS, dtype=jnp.int32)
    counts = jnp.sum((e_flat[:, :, None] == experts[None, None, :]).astype(jnp.int32), axis=1)
    padded = (counts + _STEP_ROWS - 1) // _STEP_ROWS * _STEP_ROWS
    pad_end = jnp.cumsum(padded, axis=1)
    pad_start = pad_end - padded
    start = jnp.cumsum(counts, axis=1) - counts
    row0 = jnp.arange(n_steps, dtype=jnp.int32) * _STEP_ROWS
    step_e = jnp.minimum(
        jnp.sum((row0[None, :, None] >= pad_end[:, None, :]).astype(jnp.int32), axis=2),
        N_EXPERTS - 1)
    is_e = step_e[:, :, None] == experts[None, None, :]
    take = lambda table: jnp.sum(jnp.where(is_e, table[:, None, :], 0), axis=2)
    off = row0[None, :] - take(pad_start)
    step_cnt = jnp.clip(take(counts) - off, 0, _STEP_ROWS)
    step_first = jnp.clip(take(start) + off, 0, na - 1)
    flat = lambda v: v.astype(jnp.int32).reshape(n_groups * n_steps)
    filler = jnp.zeros((n_groups, LANES), jnp.int32)
    tok_rows = jnp.concatenate([(a_sorted % ts) * ROW_CHUNKS, filler], axis=1).astype(jnp.int32)
    w_list = jnp.concatenate([w_sorted.astype(F32), filler.astype(F32)], axis=1)
    return (flat(step_e), flat(step_cnt), flat(step_first),
            tok_rows.reshape(n_groups, 1, na + LANES), w_list.reshape(n_groups, 1, na + LANES))


def _tile_sizes(bsz, seq):
    tm = min(512, seq)
    tm_merge = min(1024, seq)
    tq = min(256, seq)
    ts = min(4096, bsz * seq)
    assert seq % tm == 0 and seq % tm_merge == 0 and seq % tq == 0
    assert (bsz * seq) % ts == 0 and ts % tm_merge == 0
    return tm, tm_merge, tq, ts


def kernel(x, c, w_ada, b_ada, g_mix, w_in, qn_a, kn_a, qn_b, kn_b, w_pa, w_pb, w_out,
           g_ffn, w_router, w_gate, w_up, w_down):
    bsz, seq, d = x.shape
    depth = w_ada.shape[0]
    t = bsz * seq
    tm, tm_merge, tq, ts = _tile_sizes(bsz, seq)
    tiles_per_seq = seq // tm

    mod = _adaln(c, w_ada, b_ada)
    tables = _rope_tables(seq)
    bd_heads = 2 * LANES // HEAD_DIM
    bd = jnp.asarray(np.kron(np.eye(bd_heads), np.ones((HEAD_DIM, HEAD_DIM))), BF16)

    g4 = B_Q_HEADS // B_KV_HEADS
    head_perm = np.concatenate(
        [np.r_[np.arange(j * HEAD_DIM, (j + 1) * HEAD_DIM),
               np.arange((j + g4) * HEAD_DIM, (j + g4 + 1) * HEAD_DIM)] for j in range(g4)])
    qb_lo = 3 * A_WIDTH
    col_perm = np.arange(w_in.shape[2])
    col_perm[qb_lo:qb_lo + B_Q_WIDTH] = qb_lo + head_perm
    scale = HEAD_DIM ** -0.5 * float(np.log2(np.e))
    wr_t = w_router.T

    xf = x
    prev_moe = None
    for l in range(depth):
        sh1, sc1, gt1, sh2, sc2, gt2 = [
            mod[l, :, i * d:(i + 1) * d].reshape(bsz, 1, d) for i in range(6)]
        w_in_l = w_in[l][:, col_perm].astype(BF16)
        gains = (
            jnp.tile(qn_a[l] * scale, A_HEADS).reshape(1, A_WIDTH),
            jnp.tile(kn_a[l], A_HEADS).reshape(1, A_WIDTH),
            jnp.tile(qn_b[l] * scale, B_Q_HEADS).reshape(1, B_Q_WIDTH),
            jnp.tile(kn_b[l], B_KV_HEADS).reshape(1, B_KV_WIDTH),
        )
        outs = _in_proj(xf, prev_moe, sh1, sc1, g_mix[l].reshape(1, d), w_in_l, bd, gains,
                        tables, tm)
        za, qb, kb, vb, ga, gb = outs[:6]
        if prev_moe is not None:
            xf = outs[6]
        ya = _dilated(za)
        yb = _gqa(qb, kb, vb, tq)
        xn, hrows, e_sel, w_sel = _merge(
            xf.reshape(t, d), ya.reshape(t, A_WIDTH), yb.reshape(t, B_Q_WIDTH),
            ga.reshape(t, d), gb.reshape(t, d),
            w_pa[l].astype(BF16), w_pb[l][head_perm].astype(BF16), w_out[l].astype(BF16),
            gt1, g_ffn[l].reshape(1, d), sh2, sc2, wr_t, tm_merge, seq // tm_merge)
        e_tok = e_sel.transpose(0, 2, 1).reshape(t, TOP_K)
        w_tok = w_sel.transpose(0, 2, 1).reshape(t, TOP_K)
        plan = _dispatch_plan(e_tok, w_tok, ts)
        yrows = _experts(hrows, plan, w_gate[l].astype(BF16), w_up[l].astype(BF16),
                         w_down[l].astype(BF16), ts)
        xf = xn.reshape(bsz, seq, d)
        prev_moe = (yrows, gt2)
    out = _combine(xf.reshape(t, d), prev_moe[0], prev_moe[1], tm, tiles_per_seq)
    return out.reshape(bsz, seq, d)
```

```python
import functools

import jax
import jax.numpy as jnp
import numpy as np
from jax import lax
from jax.experimental import pallas as pl
from jax.experimental.pallas import tpu as pltpu

D_MODEL = 1024
HEAD_DIM = 64
NORM_EPS = 1e-6
A_HEADS = 8
A_WIDTH = A_HEADS * HEAD_DIM
A_PATTERNS = ((128, 1), (512, 4), (2048, 16))
A_ROT_DIMS = HEAD_DIM // 4
A_ROPE_THETA = 500000.0
B_Q_HEADS = 8
B_KV_HEADS = 2
B_Q_WIDTH = B_Q_HEADS * HEAD_DIM
B_KV_WIDTH = B_KV_HEADS * HEAD_DIM
B_ROPE_THETA = 10000.0
GRID_W = 64
N_EXPERTS = 32
N_GROUPS = 4
EXPERTS_PER_GROUP = N_EXPERTS // N_GROUPS
TOP_K = 2
D_FF_EXPERT = D_MODEL // 2
MOE_BLOCK = 256

LANES = 128
SUBLANES = 8
ROW_CHUNKS = D_MODEL // LANES
assert ROW_CHUNKS == SUBLANES
VMEM_LIMIT = 56 * 1024 * 1024

F32 = jnp.float32
BF16 = jnp.bfloat16


def _cparams(sem):
    return pltpu.CompilerParams(dimension_semantics=sem, vmem_limit_bytes=VMEM_LIMIT)


def _adaln_kernel(c_ref, w_ref, b_ref, o_ref):
    c = c_ref[...]
    cond = c * jax.nn.sigmoid(c)
    o_ref[0] = jnp.dot(cond, w_ref[0], preferred_element_type=F32,
                       precision=lax.Precision.HIGHEST) + b_ref[0]


def _adaln(c, w_ada, b_ada):
    depth, d, n = w_ada.shape
    bsz = c.shape[0]
    tn = 1024
    return pl.pallas_call(
        _adaln_kernel,
        out_shape=jax.ShapeDtypeStruct((depth, bsz, n), F32),
        grid=(depth, n // tn),
        in_specs=[
            pl.BlockSpec((bsz, d), lambda l, j: (0, 0)),
            pl.BlockSpec((1, d, tn), lambda l, j: (l, 0, j)),
            pl.BlockSpec((1, 1, tn), lambda l, j: (l, 0, j)),
        ],
        out_specs=pl.BlockSpec((1, bsz, tn), lambda l, j: (l, 0, j)),
        compiler_params=_cparams(("arbitrary", "arbitrary")),
        name="adaln",
    )(c, w_ada, b_ada.reshape(depth, 1, n))


def _rope_tables(seq):
    pos = jnp.arange(seq, dtype=F32)
    row = jnp.floor(pos / GRID_W)
    col = pos - row * GRID_W
    d = np.arange(LANES) % HEAD_DIM

    def build(segments):
        c = jnp.ones((seq, LANES), F32)
        s1 = jnp.zeros((seq, LANES), F32)
        s2 = jnp.zeros((seq, LANES), F32)
        for lo, half, theta, p in segments:
            first = (d >= lo) & (d < lo + half)
            second = (d >= lo + half) & (d < lo + 2 * half)
            idx = np.where(first, d - lo, np.where(second, d - lo - half, 0))
            freqs = theta ** (-jnp.arange(half, dtype=F32) / half)
            ang = p[:, None] * freqs[idx][None, :]
            cs, sn = jnp.cos(ang), jnp.sin(ang)
            rot = jnp.asarray(first | second)[None, :]
            c = jnp.where(rot, cs, c)
            s1 = jnp.where(jnp.asarray(first)[None, :], -sn, s1)
            s2 = jnp.where(jnp.asarray(second)[None, :], sn, s2)
        return c, s1, s2

    ta = build([(0, A_ROT_DIMS // 2, A_ROPE_THETA, pos)])
    q = HEAD_DIM // 4
    tb = build([(0, q, B_ROPE_THETA, row), (2 * q, q, B_ROPE_THETA, col)])
    return ta + tb


def _in_proj_kernel(prev_moe, *refs):
    if prev_moe:
        xn_ref, y_ref, gtp_ref, *refs = refs
        xo_ref = refs.pop()
        tm = xn_ref.shape[1]
        moe = jnp.concatenate([y_ref[pl.ds(c, tm, stride=ROW_CHUNKS), :]
                               for c in range(ROW_CHUNKS)], axis=-1)
        x = xn_ref[0] + gtp_ref[0] * moe
        xo_ref[0] = x
    else:
        x_ref, *refs = refs
        x = x_ref[0]
    (sh_ref, sc_ref, g_ref, w_ref, bd_ref, gqa_ref, gka_ref, gqb_ref, gkb_ref,
     ca_ref, s1a_ref, s2a_ref, cb_ref, s1b_ref, s2b_ref,
     za_ref, qb_ref, kb_ref, vb_ref, ga_ref, gb_ref) = refs
    ms = jnp.mean(x * x, axis=-1, keepdims=True)
    h = x * lax.rsqrt(ms + NORM_EPS) * g_ref[...]
    h = h * (1.0 + sc_ref[0]) + sh_ref[0]
    hb = h.astype(BF16)

    def seg(lo, width):
        return jnp.dot(hb, w_ref[:, lo:lo + width], preferred_element_type=F32)

    def qk_norm(z, gain_ref):
        width = z.shape[-1]
        bdw = min(width, bd_ref.shape[0])
        sq = (z * z).astype(BF16)
        parts = [jnp.dot(sq[:, lo:lo + bdw], bd_ref[:bdw, :bdw], preferred_element_type=F32)
                 for lo in range(0, width, bdw)]
        ss = parts[0] if len(parts) == 1 else jnp.concatenate(parts, axis=-1)
        return z * lax.rsqrt(ss * (1.0 / HEAD_DIM) + NORM_EPS) * gain_ref[...]

    def tile(t, width):
        reps = width // LANES
        return t if reps == 1 else jnp.concatenate([t] * reps, axis=-1)

    def rope(z, c_ref, s1_ref, s2_ref, half):
        width = z.shape[-1]
        up = pltpu.roll(z, width - half, 1)
        dn = pltpu.roll(z, half, 1)
        return (z * tile(c_ref[...], width) + up * tile(s1_ref[...], width)
                + dn * tile(s2_ref[...], width))

    ha = A_ROT_DIMS // 2
    hq = HEAD_DIM // 4
    o = 0
    qa = rope(qk_norm(seg(o, A_WIDTH), gqa_ref), ca_ref, s1a_ref, s2a_ref, ha)
    za_ref[0, :, 0:A_WIDTH] = qa
    o += A_WIDTH
    ka = rope(qk_norm(seg(o, A_WIDTH), gka_ref), ca_ref, s1a_ref, s2a_ref, ha)
    za_ref[0, :, A_WIDTH:2 * A_WIDTH] = ka
    o += A_WIDTH
    za_ref[0, :, 2 * A_WIDTH:3 * A_WIDTH] = seg(o, A_WIDTH)
    o += A_WIDTH
    qb = rope(qk_norm(seg(o, B_Q_WIDTH), gqb_ref), cb_ref, s1b_ref, s2b_ref, hq)
    qb_ref[0] = qb.astype(BF16)
    o += B_Q_WIDTH
    kb = rope(qk_norm(seg(o, B_KV_WIDTH), gkb_ref), cb_ref, s1b_ref, s2b_ref, hq)
    kb_ref[0] = kb.astype(BF16)
    o += B_KV_WIDTH
    vb_ref[0] = seg(o, B_KV_WIDTH).astype(BF16)
    o += B_KV_WIDTH
    ga_ref[0] = jax.nn.sigmoid(seg(o, D_MODEL)).astype(BF16)
    o += D_MODEL
    gb_ref[0] = jax.nn.sigmoid(seg(o, D_MODEL)).astype(BF16)


def _in_proj(x, prev_moe, sh, sc, g, w_in, bd, gains, tables, tm):
    bsz, seq, d = x.shape
    n_in = w_in.shape[1]
    nst = seq // tm
    row = lambda st, b: (b, st, 0)
    per_b = lambda st, b: (b, 0, 0)
    const2 = lambda st, b: (0, 0)
    tab = pl.BlockSpec((tm, LANES), lambda st, b: (st, 0))
    x_specs = [pl.BlockSpec((1, tm, d), row)]
    x_args = [x]
    extra_shape, extra_spec = (), ()
    if prev_moe is not None:
        x_specs += [pl.BlockSpec((tm * ROW_CHUNKS, LANES), lambda st, b: (b * nst + st, 0)),
                    pl.BlockSpec((1, 1, d), per_b)]
        x_args += list(prev_moe)
        extra_shape = (jax.ShapeDtypeStruct((bsz, seq, d), F32),)
        extra_spec = (pl.BlockSpec((1, tm, d), row),)
    return pl.pallas_call(
        functools.partial(_in_proj_kernel, prev_moe is not None),
        out_shape=(
            jax.ShapeDtypeStruct((bsz, seq, 3 * A_WIDTH), F32),
            jax.ShapeDtypeStruct((bsz, seq, B_Q_WIDTH), BF16),
            jax.ShapeDtypeStruct((bsz, seq, B_KV_WIDTH), BF16),
            jax.ShapeDtypeStruct((bsz, seq, B_KV_WIDTH), BF16),
            jax.ShapeDtypeStruct((bsz, seq, d), BF16),
            jax.ShapeDtypeStruct((bsz, seq, d), BF16),
        ) + extra_shape,
        grid=(nst, bsz),
        in_specs=x_specs + [
            pl.BlockSpec((1, 1, d), per_b),
            pl.BlockSpec((1, 1, d), per_b),
            pl.BlockSpec((1, d), const2),
            pl.BlockSpec((d, n_in), const2),
            pl.BlockSpec(bd.shape, const2),
            pl.BlockSpec((1, A_WIDTH), const2),
            pl.BlockSpec((1, A_WIDTH), const2),
            pl.BlockSpec((1, B_Q_WIDTH), const2),
            pl.BlockSpec((1, B_KV_WIDTH), const2),
            tab, tab, tab, tab, tab, tab,
        ],
        out_specs=(
            pl.BlockSpec((1, tm, 3 * A_WIDTH), row),
            pl.BlockSpec((1, tm, B_Q_WIDTH), row),
            pl.BlockSpec((1, tm, B_KV_WIDTH), row),
            pl.BlockSpec((1, tm, B_KV_WIDTH), row),
            pl.BlockSpec((1, tm, d), row),
            pl.BlockSpec((1, tm, d), row),
        ) + extra_spec,
        compiler_params=_cparams(("arbitrary", "arbitrary")),
        name="in_proj",
    )(*x_args, sh, sc, g, w_in, bd, *gains, *tables)


_QB = 128
_UNROLL = 16


def _dilated_kernel(q_ref, k_ref, v_ref, o_ref, m_scr, l_scr, acc_scr, bias_scr):
    seq = q_ref.shape[1]
    lane = lax.broadcasted_iota(jnp.int32, (1, LANES), 1)
    head0 = lane < HEAD_DIM
    radius = A_PATTERNS[0][0] // (2 * A_PATTERNS[0][1])
    assert all(w // (2 * d) == radius for w, d in A_PATTERNS) and 2 * radius == _QB

    @pl.when((pl.program_id(0) == 0) & (pl.program_id(1) == 0))
    def _():
        qrow = lax.broadcasted_iota(jnp.int32, (2 * _QB, 2 * _QB), 0) & (_QB - 1)
        kcol = lax.broadcasted_iota(jnp.int32, (2 * _QB, 2 * _QB), 1)
        for i in range(3):
            bias_scr[i] = jnp.where(jnp.abs(kcol - qrow - radius * i) <= radius, 0.0, -jnp.inf)

    def merge(t):
        return jnp.where(head0, t[:_QB], t[_QB:])

    patterns = sorted(A_PATTERNS, key=lambda wd: -wd[1])
    assert patterns[-1][1] == 1 and seq % (patterns[0][1] * _QB) == 0
    for pi, (window, dil) in enumerate(patterns):
        sub_len = seq // dil
        kw = min(2 * _QB, sub_len)
        nqb = sub_len // _QB
        nblk = dil * nqb

        def wide(t, kw=kw):
            return t if kw == LANES else jnp.concatenate([t] * (kw // LANES), axis=-1)

        def load(blk, pi=pi, dil=dil, sub_len=sub_len, kw=kw, nqb=nqb):
            r = blk // nqb
            i0q = (blk % nqb) * _QB
            i0k = jnp.clip(i0q - (kw - _QB) // 2, 0, sub_len - kw)
            if dil == 1:
                qrows = pl.ds(pl.multiple_of(i0q, _QB), _QB)
                krows = pl.ds(pl.multiple_of(i0k, SUBLANES), kw)
            else:
                qrows = pl.ds(r + dil * i0q, _QB, stride=dil)
                krows = pl.ds(r + dil * i0k, kw, stride=dil)
            q = q_ref[0, qrows, :]
            kk = k_ref[0, krows, :].astype(BF16)
            vv = v_ref[0, krows, :].astype(BF16)
            old = None
            if pi > 0:
                old = (jnp.concatenate([m_scr[0, qrows, :], m_scr[1, qrows, :]], axis=0),
                       l_scr[qrows, :], acc_scr[qrows, :])
            return qrows, (i0q - i0k) // radius, q, kk, vv, old

        def compute(mask_id, q, kk, vv, old, kw=kw, wide=wide):
            bias = bias_scr[0, :, 0:LANES] if kw == LANES else bias_scr[mask_id]
            q2 = jnp.concatenate([jnp.where(head0, q, 0.0), jnp.where(head0, 0.0, q)],
                                 axis=0).astype(BF16)
            s = lax.dot_general(q2, kk, (((1,), (1,)), ((), ())),
                                preferred_element_type=F32) + bias
            mb = jnp.broadcast_to(jnp.max(s, axis=-1, keepdims=True), (2 * _QB, LANES))
            m_new = mb if old is None else jnp.maximum(old[0], mb)
            p = jnp.exp2((s - wide(m_new)).astype(BF16))
            v_aug = jnp.concatenate([vv, jnp.ones_like(vv)], axis=-1)
            oa = jnp.dot(p, v_aug, preferred_element_type=F32)
            pv = merge(oa[:, :LANES])
            psum = merge(oa[:, LANES:])
            if old is None:
                return m_new, psum, pv
            alpha = jnp.exp2(merge(old[0]) - merge(m_new))
            return m_new, alpha * old[1] + psum, alpha * old[2] + pv

        def group(it, carry, load=load, compute=compute, last=pi == len(patterns) - 1):
            loaded = [load(it * _UNROLL + u) for u in range(_UNROLL)]
            results = [compute(*ld[1:]) for ld in loaded]
            for ld, (m_new, l_new, acc_new) in zip(loaded, results):
                qrows = ld[0]
                if last:
                    o_ref[0, qrows, :] = (acc_new / l_new).astype(o_ref.dtype)
                else:
                    m_scr[0, qrows, :] = m_new[:_QB]
                    m_scr[1, qrows, :] = m_new[_QB:]
                    l_scr[qrows, :] = l_new
                    acc_scr[qrows, :] = acc_new
            return carry

        lax.fori_loop(0, nblk // _UNROLL, group, 0)


def _dilated(za):
    bsz, seq, _ = za.shape
    nhp = A_WIDTH // LANES
    blk = (1, seq, LANES)
    return pl.pallas_call(
        _dilated_kernel,
        out_shape=jax.ShapeDtypeStruct((bsz, seq, A_WIDTH), BF16),
        grid=(bsz, nhp),
        in_specs=[
            pl.BlockSpec(blk, lambda b, j: (b, 0, j)),
            pl.BlockSpec(blk, lambda b, j: (b, 0, nhp + j)),
            pl.BlockSpec(blk, lambda b, j: (b, 0, 2 * nhp + j)),
        ],
        out_specs=pl.BlockSpec(blk, lambda b, j: (b, 0, j)),
        scratch_shapes=[
            pltpu.VMEM((2, seq, LANES), F32),
            pltpu.VMEM((seq, LANES), F32),
            pltpu.VMEM((seq, LANES), F32),
            pltpu.VMEM((3, 2 * _QB, 2 * _QB), F32),
        ],
        compiler_params=_cparams(("arbitrary", "arbitrary")),
        name="dilated_attn",
    )(za, za, za)


def _gqa_kernel(q_ref, k_ref, v_ref, o_ref):
    tq = q_ref.shape[1]
    lane = lax.broadcasted_iota(jnp.int32, (1, LANES), 1)
    head0 = lane < HEAD_DIM
    kk = k_ref[0]
    zero = jnp.zeros((), BF16)
    v_aug = jnp.concatenate([v_ref[0], jnp.ones_like(v_ref[0])], axis=-1)
    scores = []
    for j in range(B_Q_WIDTH // LANES):
        qj = q_ref[0, :, j * LANES:(j + 1) * LANES]
        q2 = jnp.concatenate([jnp.where(head0, qj, zero), jnp.where(head0, zero, qj)], axis=0)
        scores.append(lax.dot_general(q2, kk, (((1,), (1,)), ((), ())),
                                      preferred_element_type=F32))
    for j, s in enumerate(scores):
        m = jnp.max(s, axis=-1, keepdims=True)
        pb = jnp.exp2((s - m).astype(BF16))
        oa = jnp.dot(pb, v_aug, preferred_element_type=F32)
        pv = oa[:, :LANES] / oa[:, LANES:]
        o_ref[0, :, j * LANES:(j + 1) * LANES] = jnp.where(
            head0, pv[:tq], pv[tq:]).astype(o_ref.dtype)


def _gqa(qb, kb, vb, tq):
    bsz, seq, _ = qb.shape
    kv_spec = pl.BlockSpec((1, seq, B_KV_WIDTH), lambda b, i: (b, 0, 0))
    return pl.pallas_call(
        _gqa_kernel,
        out_shape=jax.ShapeDtypeStruct((bsz, seq, B_Q_WIDTH), BF16),
        grid=(bsz, seq // tq),
        in_specs=[pl.BlockSpec((1, tq, B_Q_WIDTH), lambda b, i: (b, i, 0)), kv_spec, kv_spec],
        out_specs=pl.BlockSpec((1, tq, B_Q_WIDTH), lambda b, i: (b, i, 0)),
        compiler_params=_cparams(("arbitrary", "arbitrary")),
        name="gqa_attn",
    )(qb, kb, vb)


def _merge_kernel(x_ref, ya_ref, yb_ref, ga_ref, gb_ref, wpa_ref, wpb_ref, wo_ref,
                  gt_ref, g_ref, sh_ref, sc_ref, wr_ref,
                  xn_ref, hrow_ref, e_ref, w_ref):
    tm = x_ref.shape[0]
    pa = jnp.dot(ya_ref[...], wpa_ref[...], preferred_element_type=F32)
    pb = jnp.dot(yb_ref[...], wpb_ref[...], preferred_element_type=F32)
    merged = ga_ref[...].astype(F32) * pa + gb_ref[...].astype(F32) * pb
    out = jnp.dot(merged.astype(BF16), wo_ref[...], preferred_element_type=F32)
    xn = x_ref[...] + gt_ref[0] * out
    xn_ref[...] = xn

    ms = jnp.mean(xn * xn, axis=-1, keepdims=True)
    h = xn * lax.rsqrt(ms + NORM_EPS) * g_ref[...]
    h = h * (1.0 + sc_ref[0]) + sh_ref[0]
    for c in range(ROW_CHUNKS):
        hrow_ref[pl.ds(c, tm, stride=ROW_CHUNKS), :] = h[:, c * LANES:(c + 1) * LANES]

    w = wr_ref[...]
    w_hi = w.astype(BF16)
    w_lo = (w - w_hi.astype(F32)).astype(BF16)
    h_hi = h.astype(BF16)
    h_lo = (h - h_hi.astype(F32)).astype(BF16)
    nt = (((1,), (1,)), ((), ()))
    both = lax.dot_general(jnp.concatenate([w_hi, w_lo], axis=0), h_hi, nt,
                           preferred_element_type=F32)
    logits = (both[:N_EXPERTS] + both[N_EXPERTS:]
              + lax.dot_general(w_hi, h_lo, nt, preferred_element_type=F32))
    mx = jnp.max(logits, axis=0, keepdims=True)
    ex = jnp.exp(logits - mx)
    probs = ex / jnp.sum(ex, axis=0, keepdims=True)
    pg = probs.reshape(N_GROUPS, EXPERTS_PER_GROUP, tm)
    sub = lax.broadcasted_iota(jnp.int32, pg.shape, 1).astype(F32)
    m1 = jnp.max(pg, axis=1, keepdims=True)
    i1 = jnp.min(jnp.where(pg == m1, sub, float(EXPERTS_PER_GROUP)), axis=1, keepdims=True)
    pg2 = jnp.where(sub == i1, -1.0, pg)
    m2 = jnp.max(pg2, axis=1, keepdims=True)
    i2 = jnp.min(jnp.where(pg2 == m2, sub, float(EXPERTS_PER_GROUP)), axis=1, keepdims=True)
    score = m1 + m2
    gid = lax.broadcasted_iota(jnp.int32, score.shape, 0).astype(F32)
    best = jnp.max(score, axis=0, keepdims=True)
    gsel = jnp.min(jnp.where(score == best, gid, float(N_GROUPS)), axis=0, keepdims=True)
    pick = gid == gsel
    w0 = jnp.sum(jnp.where(pick, m1, 0.0), axis=0)
    w1 = jnp.sum(jnp.where(pick, m2, 0.0), axis=0)
    j0 = jnp.sum(jnp.where(pick, i1, 0.0), axis=0)
    j1 = jnp.sum(jnp.where(pick, i2, 0.0), axis=0)
    base = gsel[0] * float(EXPERTS_PER_GROUP)
    tot = w0 + w1
    e_ref[0, 0:1, :] = (base + j0).astype(jnp.int32)
    e_ref[0, 1:2, :] = (base + j1).astype(jnp.int32)
    w_ref[0, 0:1, :] = w0 / tot
    w_ref[0, 1:2, :] = w1 / tot


def _merge(x, ya, yb, ga, gb, wpa, wpb, wo, gt, g, sh, sc, wr_t, tm, tiles_per_seq):
    t, d = x.shape
    nt = t // tm
    row = lambda i: (i, 0)
    per_b = lambda i: (i // tiles_per_seq, 0, 0)
    const2 = lambda i: (0, 0)
    return pl.pallas_call(
        _merge_kernel,
        out_shape=(
            jax.ShapeDtypeStruct((t, d), F32),
            jax.ShapeDtypeStruct((t * ROW_CHUNKS, LANES), F32),
            jax.ShapeDtypeStruct((nt, TOP_K, tm), jnp.int32),
            jax.ShapeDtypeStruct((nt, TOP_K, tm), F32),
        ),
        grid=(nt,),
        in_specs=[
            pl.BlockSpec((tm, d), row),
            pl.BlockSpec((tm, A_WIDTH), row),
            pl.BlockSpec((tm, B_Q_WIDTH), row),
            pl.BlockSpec((tm, d), row),
            pl.BlockSpec((tm, d), row),
            pl.BlockSpec((A_WIDTH, d), const2),
            pl.BlockSpec((B_Q_WIDTH, d), const2),
            pl.BlockSpec((d, d), const2),
            pl.BlockSpec((1, 1, d), per_b),
            pl.BlockSpec((1, d), const2),
            pl.BlockSpec((1, 1, d), per_b),
            pl.BlockSpec((1, 1, d), per_b),
            pl.BlockSpec((N_EXPERTS, d), const2),
        ],
        out_specs=(
            pl.BlockSpec((tm, d), row),
            pl.BlockSpec((tm * ROW_CHUNKS, LANES), row),
            pl.BlockSpec((1, TOP_K, tm), lambda i: (i, 0, 0)),
            pl.BlockSpec((1, TOP_K, tm), lambda i: (i, 0, 0)),
        ),
        compiler_params=_cparams(("arbitrary",)),
        name="merge_router",
    )(x, ya, yb, ga, gb, wpa, wpb, wo, gt, g, sh, sc, wr_t)


_ROW_UNROLL = 8
_STEP_ROWS = 2 * MOE_BLOCK
_FFN_ROWS = (64, MOE_BLOCK, MOE_BLOCK + 64, _STEP_ROWS)


def _expert_kernel(step_e_ref, step_cnt_ref, step_base_ref, tok_ref, wrow_ref, h_hbm,
                   wg_ref, wu_ref, wd_ref, acc_hbm, xs, acc, gbuf, ybuf, sem):
    g = pl.program_id(0)
    j = pl.program_id(1)
    n_groups = pl.num_programs(0)
    n_steps = pl.num_programs(1)
    group_len = xs.shape[0]

    def group_rows(gi):
        return pl.ds(pl.multiple_of(gi * group_len, ROW_CHUNKS), group_len)

    def store_copy(gi):
        return pltpu.make_async_copy(acc.at[pl.ds(0, group_len), :],
                                     acc_hbm.at[group_rows(gi), :], sem.at[1])

    @pl.when(j == 0)
    def _():
        load = pltpu.make_async_copy(h_hbm.at[group_rows(g), :], xs, sem.at[0])
        load.start()

        @pl.when(g > 0)
        def _():
            store_copy(g - 1).wait()
        acc[...] = jnp.zeros_like(acc)

        @pl.when(g == 0)
        def _():
            gbuf[...] = jnp.zeros_like(gbuf)
        load.wait()

    step = g * n_steps + j
    pl.when(step_cnt_ref[step] > 0)(functools.partial(
        _expert_block, step_cnt_ref[step], step_base_ref[step], tok_ref, wrow_ref,
        wg_ref, wu_ref, wd_ref, xs, acc, gbuf, ybuf))

    @pl.when(j == n_steps - 1)
    def _():
        store_copy(g).start()

        @pl.when(g == n_groups - 1)
        def _():
            store_copy(g).wait()


def _expert_block(cnt, first, tok_ref, wrow_ref, wg_ref, wu_ref, wd_ref, xs, acc, gbuf, ybuf):
    group_len = xs.shape[0]

    def tile_at(row):
        return pl.ds(pl.multiple_of(row, ROW_CHUNKS), ROW_CHUNKS)

    nfull = cnt // _ROW_UNROLL
    tail = cnt - nfull * _ROW_UNROLL

    def gather(c, carry):
        base = c * _ROW_UNROLL
        for u in range(_ROW_UNROLL):
            gbuf[tile_at((base + u) * ROW_CHUNKS), :] = xs[
                tile_at(tok_ref[0, 0, first + base + u]), :]
        return carry
    lax.fori_loop(0, (cnt + _ROW_UNROLL - 1) // _ROW_UNROLL, gather, 0)

    def ffn(rows):
        xb = jnp.concatenate(
            [gbuf[pl.ds(c, rows, stride=ROW_CHUNKS), :] for c in range(ROW_CHUNKS)],
            axis=-1).astype(BF16)
        gate = jnp.dot(xb, wg_ref[0], preferred_element_type=F32)
        up = jnp.dot(xb, wu_ref[0], preferred_element_type=F32)
        hid = (gate * jax.nn.sigmoid(gate) * up).astype(BF16)
        y = jnp.dot(hid, wd_ref[0], preferred_element_type=F32)
        for c in range(ROW_CHUNKS):
            ybuf[pl.ds(c, rows, stride=ROW_CHUNKS), :] = y[:, c * LANES:(c + 1) * LANES]

    lo = 0
    for rows in _FFN_ROWS:
        pl.when((cnt > lo) & (cnt <= rows))(functools.partial(ffn, rows))
        lo = rows

    def scatter_chunk(base, n_real):
        new = []
        for u in range(_ROW_UNROLL):
            dst_row = tok_ref[0, 0, first + base + u]
            wgt = wrow_ref[0, 0, first + base + u]
            if n_real is not None:
                dst_row = jnp.where(u < n_real, dst_row, group_len)
                wgt = jnp.where(u < n_real, wgt, 0.0)
            dst = tile_at(dst_row)
            new.append((dst, acc[dst, :] + wgt * ybuf[tile_at((base + u) * ROW_CHUNKS), :]))
        for dst, val in new:
            acc[dst, :] = val

    def scatter_add(c, carry):
        scatter_chunk(c * _ROW_UNROLL, None)
        return carry
    lax.fori_loop(0, nfull, scatter_add, 0)

    @pl.when(tail > 0)
    def _():
        scatter_chunk(nfull * _ROW_UNROLL, tail)


def _experts(hrows, plan, wg, wu, wd, ts):
    step_e, step_cnt, step_first, tok, wrow = plan
    n_groups, list_len = tok.shape[0], tok.shape[2]
    n_steps = step_e.shape[0] // n_groups
    d = D_MODEL
    idx_blk = (1, 1, list_len)
    idx_map = lambda g, j, se, sc, sf: (g, 0, 0)
    w_map = lambda g, j, se, sc, sf: (se[g * n_steps + j], 0, 0)
    grid_spec = pltpu.PrefetchScalarGridSpec(
        num_scalar_prefetch=3,
        grid=(n_groups, n_steps),
        in_specs=[
            pl.BlockSpec(idx_blk, idx_map, memory_space=pltpu.SMEM),
            pl.BlockSpec(idx_blk, idx_map, memory_space=pltpu.SMEM),
            pl.BlockSpec(memory_space=pl.ANY),
            pl.BlockSpec((1, d, D_FF_EXPERT), w_map),
            pl.BlockSpec((1, d, D_FF_EXPERT), w_map),
            pl.BlockSpec((1, D_FF_EXPERT, d), w_map),
        ],
        out_specs=pl.BlockSpec(memory_space=pl.ANY),
        scratch_shapes=[
            pltpu.VMEM((ts * ROW_CHUNKS, LANES), F32),
            pltpu.VMEM(((ts + 1) * ROW_CHUNKS, LANES), F32),
            pltpu.VMEM((_STEP_ROWS * ROW_CHUNKS, LANES), F32),
            pltpu.VMEM((_STEP_ROWS * ROW_CHUNKS, LANES), F32),
            pltpu.SemaphoreType.DMA((2,)),
        ],
    )
    return pl.pallas_call(
        _expert_kernel,
        out_shape=jax.ShapeDtypeStruct(hrows.shape, F32),
        grid_spec=grid_spec,
        compiler_params=_cparams(("arbitrary", "arbitrary")),
        name="experts",
    )(step_e, step_cnt, step_first, tok, wrow, hrows, wg, wu, wd)


def _combine_kernel(x_ref, y_ref, gt_ref, o_ref):
    tm = x_ref.shape[0]
    gt = gt_ref[0]
    for c in range(ROW_CHUNKS):
        cols = slice(c * LANES, (c + 1) * LANES)
        o_ref[:, cols] = x_ref[:, cols] + gt[:, cols] * y_ref[pl.ds(c, tm, stride=ROW_CHUNKS), :]


def _combine(xn, yrows, gt, tm, tiles_per_seq):
    t, d = xn.shape
    nt = t // tm
    return pl.pallas_call(
        _combine_kernel,
        out_shape=jax.ShapeDtypeStruct((t, d), F32),
        grid=(nt,),
        in_specs=[
            pl.BlockSpec((tm, d), lambda i: (i, 0)),
            pl.BlockSpec((tm * ROW_CHUNKS, LANES), lambda i: (i, 0)),
            pl.BlockSpec((1, 1, d), lambda i: (i // tiles_per_seq, 0, 0)),
        ],
        out_specs=pl.BlockSpec((tm, d), lambda i: (i, 0)),
        compiler_params=_cparams(("arbitrary",)),
        name="moe_combine",
    )(xn, yrows, gt)


def _dispatch_plan(e_sel, w_sel, ts):
    t = e_sel.shape[0]
    n_groups = t // ts
    na = ts * TOP_K
    n_steps = na // _STEP_ROWS + N_EXPERTS
    e_flat = e_sel.reshape(n_groups, ts, TOP_K).transpose(0, 2, 1).reshape(n_groups, na)
    w_flat = w_sel.reshape(n_groups, ts, TOP_K).transpose(0, 2, 1).reshape(n_groups, na)
    a_ids = jnp.broadcast_to(jnp.arange(na, dtype=jnp.int32)[None, :], (n_groups, na))
    _, a_sorted, w_sorted = lax.sort((e_flat, a_ids, w_flat), dimension=1, num_keys=1)
    experts = jnp.arange(N_EXPERTS, dtype=jnp.int32)
    counts = jnp.sum((e_flat[:, :, None] == experts[None, None, :]).astype(jnp.int32), axis=1)
    padded = (counts + _STEP_ROWS - 1) // _STEP_ROWS * _STEP_ROWS
    pad_end = jnp.cumsum(padded, axis=1)
    pad_start = pad_end - padded
    start = jnp.cumsum(counts, axis=1) - counts
    row0 = jnp.arange(n_steps, dtype=jnp.int32) * _STEP_ROWS
    step_e = jnp.minimum(
        jnp.sum((row0[None, :, None] >= pad_end[:, None, :]).astype(jnp.int32), axis=2),
        N_EXPERTS - 1)
    is_e = step_e[:, :, None] == experts[None, None, :]
    take = lambda table: jnp.sum(jnp.where(is_e, table[:, None, :], 0), axis=2)
    off = row0[None, :] - take(pad_start)
    step_cnt = jnp.clip(take(counts) - off, 0, _STEP_ROWS)
    step_first = jnp.clip(take(start) + off, 0, na - 1)
    flat = lambda v: v.astype(jnp.int32).reshape(n_groups * n_steps)
    filler = jnp.zeros((n_groups, LANES), jnp.int32)
    tok_rows = jnp.concatenate([(a_sorted % ts) * ROW_CHUNKS, filler], axis=1).astype(jnp.int32)
    w_list = jnp.concatenate([w_sorted.astype(F32), filler.astype(F32)], axis=1)
    return (flat(step_e), flat(step_cnt), flat(step_first),
            tok_rows.reshape(n_groups, 1, na + LANES), w_list.reshape(n_groups, 1, na + LANES))


def _tile_sizes(bsz, seq):
    tm = min(512, seq)
    tm_merge = min(1024, seq)
    tq = min(256, seq)
    ts = min(4096, bsz * seq)
    assert seq % tm == 0 and seq % tm_merge == 0 and seq % tq == 0
    assert (bsz * seq) % ts == 0 and ts % tm_merge == 0
    return tm, tm_merge, tq, ts


def kernel(x, c, w_ada, b_ada, g_mix, w_in, qn_a, kn_a, qn_b, kn_b, w_pa, w_pb, w_out,
           g_ffn, w_router, w_gate, w_up, w_down):
    bsz, seq, d = x.shape
    depth = w_ada.shape[0]
    t = bsz * seq
    tm, tm_merge, tq, ts = _tile_sizes(bsz, seq)
    tiles_per_seq = seq // tm

    mod = _adaln(c, w_ada, b_ada)
    tables = _rope_tables(seq)
    bd_heads = 2 * LANES // HEAD_DIM
    bd = jnp.asarray(np.kron(np.eye(bd_heads), np.ones((HEAD_DIM, HEAD_DIM))), BF16)

    g4 = B_Q_HEADS // B_KV_HEADS
    head_perm = np.concatenate(
        [np.r_[np.arange(j * HEAD_DIM, (j + 1) * HEAD_DIM),
               np.arange((j + g4) * HEAD_DIM, (j + g4 + 1) * HEAD_DIM)] for j in range(g4)])
    qb_lo = 3 * A_WIDTH
    col_perm = np.arange(w_in.shape[2])
    col_perm[qb_lo:qb_lo + B_Q_WIDTH] = qb_lo + head_perm
    scale = HEAD_DIM ** -0.5 * float(np.log2(np.e))
    wr_t = w_router.T

    xf = x
    prev_moe = None
    for l in range(depth):
        sh1, sc1, gt1, sh2, sc2, gt2 = [
            mod[l, :, i * d:(i + 1) * d].reshape(bsz, 1, d) for i in range(6)]
        w_in_l = w_in[l][:, col_perm].astype(BF16)
        gains = (
            jnp.tile(qn_a[l] * scale, A_HEADS).reshape(1, A_WIDTH),
            jnp.tile(kn_a[l], A_HEADS).reshape(1, A_WIDTH),
            jnp.tile(qn_b[l] * scale, B_Q_HEADS).reshape(1, B_Q_WIDTH),
            jnp.tile(kn_b[l], B_KV_HEADS).reshape(1, B_KV_WIDTH),
        )
        outs = _in_proj(xf, prev_moe, sh1, sc1, g_mix[l].reshape(1, d), w_in_l, bd, gains,
                        tables, tm)
        za, qb, kb, vb, ga, gb = outs[:6]
        if prev_moe is not None:
            xf = outs[6]
        ya = _dilated(za)
        yb = _gqa(qb, kb, vb, tq)
        xn, hrows, e_sel, w_sel = _merge(
            xf.reshape(t, d), ya.reshape(t, A_WIDTH), yb.reshape(t, B_Q_WIDTH),
            ga.reshape(t, d), gb.reshape(t, d),
            w_pa[l].astype(BF16), w_pb[l][head_perm].astype(BF16), w_out[l].astype(BF16),
            gt1, g_ffn[l].reshape(1, d), sh2, sc2, wr_t, tm_merge, seq // tm_merge)
        e_tok = e_sel.transpose(0, 2, 1).reshape(t, TOP_K)
        w_tok = w_sel.transpose(0, 2, 1).reshape(t, TOP_K)
        plan = _dispatch_plan(e_tok, w_tok, ts)
        yrows = _experts(hrows, plan, w_gate[l].astype(BF16), w_up[l].astype(BF16),
                         w_down[l].astype(BF16), ts)
        xf = xn.reshape(bsz, seq, d)
        prev_moe = (yrows, gt2)
    out = _combine(xf.reshape(t, d), prev_moe[0], prev_moe[1], tm, tiles_per_seq)
    return out.reshape(bsz, seq, d)
```

```python
import functools

import jax
import jax.numpy as jnp
import numpy as np
from jax import lax
from jax.experimental import pallas as pl
from jax.experimental.pallas import tpu as pltpu

D_MODEL = 1024
HEAD_DIM = 64
NORM_EPS = 1e-6
A_HEADS = 8
A_WIDTH = A_HEADS * HEAD_DIM
A_PATTERNS = ((128, 1), (512, 4), (2048, 16))
A_ROT_DIMS = HEAD_DIM // 4
A_ROPE_THETA = 500000.0
B_Q_HEADS = 8
B_KV_HEADS = 2
B_Q_WIDTH = B_Q_HEADS * HEAD_DIM
B_KV_WIDTH = B_KV_HEADS * HEAD_DIM
B_ROPE_THETA = 10000.0
GRID_W = 64
N_EXPERTS = 32
N_GROUPS = 4
EXPERTS_PER_GROUP = N_EXPERTS // N_GROUPS
TOP_K = 2
D_FF_EXPERT = D_MODEL // 2
MOE_BLOCK = 256

LANES = 128
SUBLANES = 8
ROW_CHUNKS = D_MODEL // LANES
assert ROW_CHUNKS == SUBLANES
VMEM_LIMIT = 56 * 1024 * 1024

F32 = jnp.float32
BF16 = jnp.bfloat16


def _cparams(sem):
    return pltpu.CompilerParams(dimension_semantics=sem, vmem_limit_bytes=VMEM_LIMIT)


def _adaln_kernel(c_ref, w_ref, b_ref, o_ref):
    c = c_ref[...]
    cond = c * jax.nn.sigmoid(c)
    o_ref[0] = jnp.dot(cond, w_ref[0], preferred_element_type=F32,
                       precision=lax.Precision.HIGHEST) + b_ref[0]


def _adaln(c, w_ada, b_ada):
    depth, d, n = w_ada.shape
    bsz = c.shape[0]
    tn = 1024
    return pl.pallas_call(
        _adaln_kernel,
        out_shape=jax.ShapeDtypeStruct((depth, bsz, n), F32),
        grid=(depth, n // tn),
        in_specs=[
            pl.BlockSpec((bsz, d), lambda l, j: (0, 0)),
            pl.BlockSpec((1, d, tn), lambda l, j: (l, 0, j)),
            pl.BlockSpec((1, 1, tn), lambda l, j: (l, 0, j)),
        ],
        out_specs=pl.BlockSpec((1, bsz, tn), lambda l, j: (l, 0, j)),
        compiler_params=_cparams(("arbitrary", "arbitrary")),
        name="adaln",
    )(c, w_ada, b_ada.reshape(depth, 1, n))


def _rope_tables(seq):
    pos = jnp.arange(seq, dtype=F32)
    row = jnp.floor(pos / GRID_W)
    col = pos - row * GRID_W
    d = np.arange(LANES) % HEAD_DIM

    def build(segments):
        c = jnp.ones((seq, LANES), F32)
        s1 = jnp.zeros((seq, LANES), F32)
        s2 = jnp.zeros((seq, LANES), F32)
        for lo, half, theta, p in segments:
            first = (d >= lo) & (d < lo + half)
            second = (d >= lo + half) & (d < lo + 2 * half)
            idx = np.where(first, d - lo, np.where(second, d - lo - half, 0))
            freqs = theta ** (-jnp.arange(half, dtype=F32) / half)
            ang = p[:, None] * freqs[idx][None, :]
            cs, sn = jnp.cos(ang), jnp.sin(ang)
            rot = jnp.asarray(first | second)[None, :]
            c = jnp.where(rot, cs, c)
            s1 = jnp.where(jnp.asarray(first)[None, :], -sn, s1)
            s2 = jnp.where(jnp.asarray(second)[None, :], sn, s2)
        return c, s1, s2

    ta = build([(0, A_ROT_DIMS // 2, A_ROPE_THETA, pos)])
    q = HEAD_DIM // 4
    tb = build([(0, q, B_ROPE_THETA, row), (2 * q, q, B_ROPE_THETA, col)])
    return ta + tb


def _in_proj_kernel(prev_moe, *refs):
    if prev_moe:
        xn_ref, y_ref, gtp_ref, *refs = refs
        xo_ref = refs.pop()
        tm = xn_ref.shape[1]
        moe = jnp.concatenate([y_ref[pl.ds(c, tm, stride=ROW_CHUNKS), :]
                               for c in range(ROW_CHUNKS)], axis=-1)
        x = xn_ref[0] + gtp_ref[0] * moe
        xo_ref[0] = x
    else:
        x_ref, *refs = refs
        x = x_ref[0]
    (sh_ref, sc_ref, g_ref, w_ref, bd_ref, gqa_ref, gka_ref, gqb_ref, gkb_ref,
     ca_ref, s1a_ref, s2a_ref, cb_ref, s1b_ref, s2b_ref,
     za_ref, qb_ref, kb_ref, vb_ref, ga_ref, gb_ref) = refs
    ms = jnp.mean(x * x, axis=-1, keepdims=True)
    h = x * lax.rsqrt(ms + NORM_EPS) * g_ref[...]
    h = h * (1.0 + sc_ref[0]) + sh_ref[0]
    hb = h.astype(BF16)

    def seg(lo, width):
        return jnp.dot(hb, w_ref[:, lo:lo + width], preferred_element_type=F32)

    def qk_norm(z, gain_ref):
        width = z.shape[-1]
        bdw = min(width, bd_ref.shape[0])
        sq = (z * z).astype(BF16)
        parts = [jnp.dot(sq[:, lo:lo + bdw], bd_ref[:bdw, :bdw], preferred_element_type=F32)
                 for lo in range(0, width, bdw)]
        ss = parts[0] if len(parts) == 1 else jnp.concatenate(parts, axis=-1)
        return z * lax.rsqrt(ss * (1.0 / HEAD_DIM) + NORM_EPS) * gain_ref[...]

    def tile(t, width):
        reps = width // LANES
        return t if reps == 1 else jnp.concatenate([t] * reps, axis=-1)

    def rope(z, c_ref, s1_ref, s2_ref, half):
        width = z.shape[-1]
        up = pltpu.roll(z, width - half, 1)
        dn = pltpu.roll(z, half, 1)
        return (z * tile(c_ref[...], width) + up * tile(s1_ref[...], width)
                + dn * tile(s2_ref[...], width))

    ha = A_ROT_DIMS // 2
    hq = HEAD_DIM // 4
    widths = (A_WIDTH, A_WIDTH, A_WIDTH, B_Q_WIDTH, B_KV_WIDTH, B_KV_WIDTH, D_MODEL, D_MODEL)
    starts = np.cumsum((0,) + widths[:-1])
    z_qa, z_ka, z_va, z_qb, z_kb, z_vb, z_ga, z_gb = [
        seg(int(lo), w) for lo, w in zip(starts, widths)]
    za_ref[0, :, 0:A_WIDTH] = rope(qk_norm(z_qa, gqa_ref), ca_ref, s1a_ref, s2a_ref, ha)
    za_ref[0, :, A_WIDTH:2 * A_WIDTH] = rope(qk_norm(z_ka, gka_ref), ca_ref, s1a_ref, s2a_ref, ha)
    za_ref[0, :, 2 * A_WIDTH:3 * A_WIDTH] = z_va
    qb_ref[0] = rope(qk_norm(z_qb, gqb_ref), cb_ref, s1b_ref, s2b_ref, hq).astype(BF16)
    kb_ref[0] = rope(qk_norm(z_kb, gkb_ref), cb_ref, s1b_ref, s2b_ref, hq).astype(BF16)
    vb_ref[0] = z_vb.astype(BF16)
    ga_ref[0] = jax.nn.sigmoid(z_ga).astype(BF16)
    gb_ref[0] = jax.nn.sigmoid(z_gb).astype(BF16)


def _in_proj(x, prev_moe, sh, sc, g, w_in, bd, gains, tables, tm):
    bsz, seq, d = x.shape
    n_in = w_in.shape[1]
    nst = seq // tm
    row = lambda st, b: (b, st, 0)
    per_b = lambda st, b: (b, 0, 0)
    const2 = lambda st, b: (0, 0)
    tab = pl.BlockSpec((tm, LANES), lambda st, b: (st, 0))
    x_specs = [pl.BlockSpec((1, tm, d), row)]
    x_args = [x]
    extra_shape, extra_spec = (), ()
    if prev_moe is not None:
        x_specs += [pl.BlockSpec((tm * ROW_CHUNKS, LANES), lambda st, b: (b * nst + st, 0)),
                    pl.BlockSpec((1, 1, d), per_b)]
        x_args += list(prev_moe)
        extra_shape = (jax.ShapeDtypeStruct((bsz, seq, d), F32),)
        extra_spec = (pl.BlockSpec((1, tm, d), row),)
    return pl.pallas_call(
        functools.partial(_in_proj_kernel, prev_moe is not None),
        out_shape=(
            jax.ShapeDtypeStruct((bsz, seq, 3 * A_WIDTH), F32),
            jax.ShapeDtypeStruct((bsz, seq, B_Q_WIDTH), BF16),
            jax.ShapeDtypeStruct((bsz, seq, B_KV_WIDTH), BF16),
            jax.ShapeDtypeStruct((bsz, seq, B_KV_WIDTH), BF16),
            jax.ShapeDtypeStruct((bsz, seq, d), BF16),
            jax.ShapeDtypeStruct((bsz, seq, d), BF16),
        ) + extra_shape,
        grid=(nst, bsz),
        in_specs=x_specs + [
            pl.BlockSpec((1, 1, d), per_b),
            pl.BlockSpec((1, 1, d), per_b),
            pl.BlockSpec((1, d), const2),
            pl.BlockSpec((d, n_in), const2),
            pl.BlockSpec(bd.shape, const2),
            pl.BlockSpec((1, A_WIDTH), const2),
            pl.BlockSpec((1, A_WIDTH), const2),
            pl.BlockSpec((1, B_Q_WIDTH), const2),
            pl.BlockSpec((1, B_KV_WIDTH), const2),
            tab, tab, tab, tab, tab, tab,
        ],
        out_specs=(
            pl.BlockSpec((1, tm, 3 * A_WIDTH), row),
            pl.BlockSpec((1, tm, B_Q_WIDTH), row),
            pl.BlockSpec((1, tm, B_KV_WIDTH), row),
            pl.BlockSpec((1, tm, B_KV_WIDTH), row),
            pl.BlockSpec((1, tm, d), row),
            pl.BlockSpec((1, tm, d), row),
        ) + extra_spec,
        compiler_params=_cparams(("arbitrary", "arbitrary")),
        name="in_proj",
    )(*x_args, sh, sc, g, w_in, bd, *gains, *tables)


_QB = 128
_UNROLL = 16


def _dilated_kernel(q_ref, k_ref, v_ref, o_ref, m_scr, l_scr, acc_scr, bias_scr):
    seq = q_ref.shape[1]
    lane = lax.broadcasted_iota(jnp.int32, (1, LANES), 1)
    head0 = lane < HEAD_DIM
    radius = A_PATTERNS[0][0] // (2 * A_PATTERNS[0][1])
    assert all(w // (2 * d) == radius for w, d in A_PATTERNS) and 2 * radius == _QB

    @pl.when((pl.program_id(0) == 0) & (pl.program_id(1) == 0))
    def _():
        qrow = lax.broadcasted_iota(jnp.int32, (2 * _QB, 2 * _QB), 0) & (_QB - 1)
        kcol = lax.broadcasted_iota(jnp.int32, (2 * _QB, 2 * _QB), 1)
        for i in range(3):
            bias_scr[i] = jnp.where(jnp.abs(kcol - qrow - radius * i) <= radius, 0.0, -jnp.inf)

    def merge(t):
        return jnp.where(head0, t[:_QB], t[_QB:])

    patterns = sorted(A_PATTERNS, key=lambda wd: -wd[1])
    assert patterns[-1][1] == 1 and seq % (patterns[0][1] * _QB) == 0
    for pi, (window, dil) in enumerate(patterns):
        sub_len = seq // dil
        kw = min(2 * _QB, sub_len)
        nqb = sub_len // _QB
        nblk = dil * nqb

        def wide(t, kw=kw):
            return t if kw == LANES else jnp.concatenate([t] * (kw // LANES), axis=-1)

        def load(blk, pi=pi, dil=dil, sub_len=sub_len, kw=kw, nqb=nqb):
            r = blk // nqb
            i0q = (blk % nqb) * _QB
            i0k = jnp.clip(i0q - (kw - _QB) // 2, 0, sub_len - kw)
            if dil == 1:
                qrows = pl.ds(pl.multiple_of(i0q, _QB), _QB)
                krows = pl.ds(pl.multiple_of(i0k, SUBLANES), kw)
            else:
                qrows = pl.ds(r + dil * i0q, _QB, stride=dil)
                krows = pl.ds(r + dil * i0k, kw, stride=dil)
            q = q_ref[0, qrows, :]
            kk = k_ref[0, krows, :].astype(BF16)
            vv = v_ref[0, krows, :].astype(BF16)
            old = None
            if pi > 0:
                old = (jnp.concatenate([m_scr[0, qrows, :], m_scr[1, qrows, :]], axis=0),
                       l_scr[qrows, :], acc_scr[qrows, :])
            return qrows, (i0q - i0k) // radius, q, kk, vv, old

        def compute(mask_id, q, kk, vv, old, kw=kw, wide=wide):
            bias = bias_scr[0, :, 0:LANES] if kw == LANES else bias_scr[mask_id]
            q2 = jnp.concatenate([jnp.where(head0, q, 0.0), jnp.where(head0, 0.0, q)],
                                 axis=0).astype(BF16)
            s = lax.dot_general(q2, kk, (((1,), (1,)), ((), ())),
                                preferred_element_type=F32) + bias
            mb = jnp.broadcast_to(jnp.max(s, axis=-1, keepdims=True), (2 * _QB, LANES))
            m_new = mb if old is None else jnp.maximum(old[0], mb)
            p = jnp.exp2((s - wide(m_new)).astype(BF16))
            v_aug = jnp.concatenate([vv, jnp.ones_like(vv)], axis=-1)
            oa = jnp.dot(p, v_aug, preferred_element_type=F32)
            pv = merge(oa[:, :LANES])
            psum = merge(oa[:, LANES:])
            if old is None:
                return m_new, psum, pv
            alpha = jnp.exp2(merge(old[0]) - merge(m_new))
            return m_new, alpha * old[1] + psum, alpha * old[2] + pv

        def group(it, carry, load=load, compute=compute, last=pi == len(patterns) - 1):
            loaded = [load(it * _UNROLL + u) for u in range(_UNROLL)]
            results = [compute(*ld[1:]) for ld in loaded]
            for ld, (m_new, l_new, acc_new) in zip(loaded, results):
                qrows = ld[0]
                if last:
                    o_ref[0, qrows, :] = (acc_new / l_new).astype(o_ref.dtype)
                else:
                    m_scr[0, qrows, :] = m_new[:_QB]
                    m_scr[1, qrows, :] = m_new[_QB:]
                    l_scr[qrows, :] = l_new
                    acc_scr[qrows, :] = acc_new
            return carry

        lax.fori_loop(0, nblk // _UNROLL, group, 0)


def _dilated(za):
    bsz, seq, _ = za.shape
    nhp = A_WIDTH // LANES
    blk = (1, seq, LANES)
    return pl.pallas_call(
        _dilated_kernel,
        out_shape=jax.ShapeDtypeStruct((bsz, seq, A_WIDTH), BF16),
        grid=(bsz, nhp),
        in_specs=[
            pl.BlockSpec(blk, lambda b, j: (b, 0, j)),
            pl.BlockSpec(blk, lambda b, j: (b, 0, nhp + j)),
            pl.BlockSpec(blk, lambda b, j: (b, 0, 2 * nhp + j)),
        ],
        out_specs=pl.BlockSpec(blk, lambda b, j: (b, 0, j)),
        scratch_shapes=[
            pltpu.VMEM((2, seq, LANES), F32),
            pltpu.VMEM((seq, LANES), F32),
            pltpu.VMEM((seq, LANES), F32),
            pltpu.VMEM((3, 2 * _QB, 2 * _QB), F32),
        ],
        compiler_params=_cparams(("arbitrary", "arbitrary")),
        name="dilated_attn",
    )(za, za, za)


def _gqa_kernel(q_ref, k_ref, v_ref, o_ref):
    tq = q_ref.shape[1]
    lane = lax.broadcasted_iota(jnp.int32, (1, LANES), 1)
    head0 = lane < HEAD_DIM
    kk = k_ref[0]
    zero = jnp.zeros((), BF16)
    v_aug = jnp.concatenate([v_ref[0], jnp.ones_like(v_ref[0])], axis=-1)
    scores = []
    for j in range(B_Q_WIDTH // LANES):
        qj = q_ref[0, :, j * LANES:(j + 1) * LANES]
        q2 = jnp.concatenate([jnp.where(head0, qj, zero), jnp.where(head0, zero, qj)], axis=0)
        scores.append(lax.dot_general(q2, kk, (((1,), (1,)), ((), ())),
                                      preferred_element_type=F32))
    for j, s in enumerate(scores):
        m = jnp.max(s, axis=-1, keepdims=True)
        pb = jnp.exp2((s - m).astype(BF16))
        oa = jnp.dot(pb, v_aug, preferred_element_type=F32)
        pv = oa[:, :LANES] / oa[:, LANES:]
        o_ref[0, :, j * LANES:(j + 1) * LANES] = jnp.where(
            head0, pv[:tq], pv[tq:]).astype(o_ref.dtype)


def _gqa(qb, kb, vb, tq):
    bsz, seq, _ = qb.shape
    kv_spec = pl.BlockSpec((1, seq, B_KV_WIDTH), lambda b, i: (b, 0, 0))
    return pl.pallas_call(
        _gqa_kernel,
        out_shape=jax.ShapeDtypeStruct((bsz, seq, B_Q_WIDTH), BF16),
        grid=(bsz, seq // tq),
        in_specs=[pl.BlockSpec((1, tq, B_Q_WIDTH), lambda b, i: (b, i, 0)), kv_spec, kv_spec],
        out_specs=pl.BlockSpec((1, tq, B_Q_WIDTH), lambda b, i: (b, i, 0)),
        compiler_params=_cparams(("arbitrary", "arbitrary")),
        name="gqa_attn",
    )(qb, kb, vb)


def _merge_kernel(x_ref, ya_ref, yb_ref, ga_ref, gb_ref, wpa_ref, wpb_ref, wo_ref,
                  gt_ref, g_ref, sh_ref, sc_ref, wr_ref,
                  xn_ref, hrow_ref, e_ref, w_ref):
    tm = x_ref.shape[0]
    pa = jnp.dot(ya_ref[...], wpa_ref[...], preferred_element_type=F32)
    pb = jnp.dot(yb_ref[...], wpb_ref[...], preferred_element_type=F32)
    merged = ga_ref[...].astype(F32) * pa + gb_ref[...].astype(F32) * pb
    out = jnp.dot(merged.astype(BF16), wo_ref[...], preferred_element_type=F32)
    xn = x_ref[...] + gt_ref[0] * out
    xn_ref[...] = xn

    ms = jnp.mean(xn * xn, axis=-1, keepdims=True)
    h = xn * lax.rsqrt(ms + NORM_EPS) * g_ref[...]
    h = h * (1.0 + sc_ref[0]) + sh_ref[0]
    for c in range(ROW_CHUNKS):
        hrow_ref[pl.ds(c, tm, stride=ROW_CHUNKS), :] = h[:, c * LANES:(c + 1) * LANES]

    w = wr_ref[...]
    w_hi = w.astype(BF16)
    w_lo = (w - w_hi.astype(F32)).astype(BF16)
    h_hi = h.astype(BF16)
    h_lo = (h - h_hi.astype(F32)).astype(BF16)
    nt = (((1,), (1,)), ((), ()))
    both = lax.dot_general(jnp.concatenate([w_hi, w_lo], axis=0), h_hi, nt,
                           preferred_element_type=F32)
    logits = (both[:N_EXPERTS] + both[N_EXPERTS:]
              + lax.dot_general(w_hi, h_lo, nt, preferred_element_type=F32))
    mx = jnp.max(logits, axis=0, keepdims=True)
    ex = jnp.exp(logits - mx)
    probs = ex / jnp.sum(ex, axis=0, keepdims=True)
    pg = probs.reshape(N_GROUPS, EXPERTS_PER_GROUP, tm)
    sub = lax.broadcasted_iota(jnp.int32, pg.shape, 1).astype(F32)
    m1 = jnp.max(pg, axis=1, keepdims=True)
    i1 = jnp.min(jnp.where(pg == m1, sub, float(EXPERTS_PER_GROUP)), axis=1, keepdims=True)
    pg2 = jnp.where(sub == i1, -1.0, pg)
    m2 = jnp.max(pg2, axis=1, keepdims=True)
    i2 = jnp.min(jnp.where(pg2 == m2, sub, float(EXPERTS_PER_GROUP)), axis=1, keepdims=True)
    score = m1 + m2
    gid = lax.broadcasted_iota(jnp.int32, score.shape, 0).astype(F32)
    best = jnp.max(score, axis=0, keepdims=True)
    gsel = jnp.min(jnp.where(score == best, gid, float(N_GROUPS)), axis=0, keepdims=True)
    pick = gid == gsel
    w0 = jnp.sum(jnp.where(pick, m1, 0.0), axis=0)
    w1 = jnp.sum(jnp.where(pick, m2, 0.0), axis=0)
    j0 = jnp.sum(jnp.where(pick, i1, 0.0), axis=0)
    j1 = jnp.sum(jnp.where(pick, i2, 0.0), axis=0)
    base = gsel[0] * float(EXPERTS_PER_GROUP)
    tot = w0 + w1
    e_ref[0, 0:1, :] = (base + j0).astype(jnp.int32)
    e_ref[0, 1:2, :] = (base + j1).astype(jnp.int32)
    w_ref[0, 0:1, :] = w0 / tot
    w_ref[0, 1:2, :] = w1 / tot


def _merge(x, ya, yb, ga, gb, wpa, wpb, wo, gt, g, sh, sc, wr_t, tm, tiles_per_seq):
    t, d = x.shape
    nt = t // tm
    row = lambda i: (i, 0)
    per_b = lambda i: (i // tiles_per_seq, 0, 0)
    const2 = lambda i: (0, 0)
    return pl.pallas_call(
        _merge_kernel,
        out_shape=(
            jax.ShapeDtypeStruct((t, d), F32),
            jax.ShapeDtypeStruct((t * ROW_CHUNKS, LANES), F32),
            jax.ShapeDtypeStruct((nt, TOP_K, tm), jnp.int32),
            jax.ShapeDtypeStruct((nt, TOP_K, tm), F32),
        ),
        grid=(nt,),
        in_specs=[
            pl.BlockSpec((tm, d), row),
            pl.BlockSpec((tm, A_WIDTH), row),
            pl.BlockSpec((tm, B_Q_WIDTH), row),
            pl.BlockSpec((tm, d), row),
            pl.BlockSpec((tm, d), row),
            pl.BlockSpec((A_WIDTH, d), const2),
            pl.BlockSpec((B_Q_WIDTH, d), const2),
            pl.BlockSpec((d, d), const2),
            pl.BlockSpec((1, 1, d), per_b),
            pl.BlockSpec((1, d), const2),
            pl.BlockSpec((1, 1, d), per_b),
            pl.BlockSpec((1, 1, d), per_b),
            pl.BlockSpec((N_EXPERTS, d), const2),
        ],
        out_specs=(
            pl.BlockSpec((tm, d), row),
            pl.BlockSpec((tm * ROW_CHUNKS, LANES), row),
            pl.BlockSpec((1, TOP_K, tm), lambda i: (i, 0, 0)),
            pl.BlockSpec((1, TOP_K, tm), lambda i: (i, 0, 0)),
        ),
        compiler_params=_cparams(("arbitrary",)),
        name="merge_router",
    )(x, ya, yb, ga, gb, wpa, wpb, wo, gt, g, sh, sc, wr_t)


_ROW_UNROLL = 8
_STEP_ROWS = 2 * MOE_BLOCK
_FFN_ROWS = (64, MOE_BLOCK, MOE_BLOCK + 64, _STEP_ROWS)


def _expert_kernel(step_e_ref, step_cnt_ref, step_base_ref, tok_ref, wrow_ref, h_hbm,
                   wg_ref, wu_ref, wd_ref, acc_hbm, xs, acc, gbuf, ybuf, sem):
    g = pl.program_id(0)
    j = pl.program_id(1)
    n_groups = pl.num_programs(0)
    n_steps = pl.num_programs(1)
    group_len = xs.shape[0]

    def group_rows(gi):
        return pl.ds(pl.multiple_of(gi * group_len, ROW_CHUNKS), group_len)

    def store_copy(gi):
        return pltpu.make_async_copy(acc.at[pl.ds(0, group_len), :],
                                     acc_hbm.at[group_rows(gi), :], sem.at[1])

    @pl.when(j == 0)
    def _():
        load = pltpu.make_async_copy(h_hbm.at[group_rows(g), :], xs, sem.at[0])
        load.start()

        @pl.when(g > 0)
        def _():
            store_copy(g - 1).wait()
        acc[...] = jnp.zeros_like(acc)

        @pl.when(g == 0)
        def _():
            gbuf[...] = jnp.zeros_like(gbuf)
        load.wait()

    step = g * n_steps + j
    pl.when(step_cnt_ref[step] > 0)(functools.partial(
        _expert_block, step_cnt_ref[step], step_base_ref[step], tok_ref, wrow_ref,
        wg_ref, wu_ref, wd_ref, xs, acc, gbuf, ybuf))

    @pl.when(j == n_steps - 1)
    def _():
        store_copy(g).start()

        @pl.when(g == n_groups - 1)
        def _():
            store_copy(g).wait()


def _expert_block(cnt, first, tok_ref, wrow_ref, wg_ref, wu_ref, wd_ref, xs, acc, gbuf, ybuf):
    group_len = xs.shape[0]

    def tile_at(row):
        return pl.ds(pl.multiple_of(row, ROW_CHUNKS), ROW_CHUNKS)

    nfull = cnt // _ROW_UNROLL
    tail = cnt - nfull * _ROW_UNROLL

    def gather(c, carry):
        base = c * _ROW_UNROLL
        for u in range(_ROW_UNROLL):
            gbuf[tile_at((base + u) * ROW_CHUNKS), :] = xs[
                tile_at(tok_ref[0, 0, first + base + u]), :]
        return carry
    lax.fori_loop(0, (cnt + _ROW_UNROLL - 1) // _ROW_UNROLL, gather, 0)

    def ffn(rows):
        xb = jnp.concatenate(
            [gbuf[pl.ds(c, rows, stride=ROW_CHUNKS), :] for c in range(ROW_CHUNKS)],
            axis=-1).astype(BF16)
        gate = jnp.dot(xb, wg_ref[0], preferred_element_type=F32)
        up = jnp.dot(xb, wu_ref[0], preferred_element_type=F32)
        hid = (gate * jax.nn.sigmoid(gate) * up).astype(BF16)
        y = jnp.dot(hid, wd_ref[0], preferred_element_type=F32)
        for c in range(ROW_CHUNKS):
            ybuf[pl.ds(c, rows, stride=ROW_CHUNKS), :] = y[:, c * LANES:(c + 1) * LANES]

    lo = 0
    for rows in _FFN_ROWS:
        pl.when((cnt > lo) & (cnt <= rows))(functools.partial(ffn, rows))
        lo = rows

    def scatter_chunk(base, n_real):
        new = []
        for u in range(_ROW_UNROLL):
            dst_row = tok_ref[0, 0, first + base + u]
            wgt = wrow_ref[0, 0, first + base + u]
            if n_real is not None:
                dst_row = jnp.where(u < n_real, dst_row, group_len)
                wgt = jnp.where(u < n_real, wgt, 0.0)
            dst = tile_at(dst_row)
            new.append((dst, acc[dst, :] + wgt * ybuf[tile_at((base + u) * ROW_CHUNKS), :]))
        for dst, val in new:
            acc[dst, :] = val

    def scatter_add(c, carry):
        scatter_chunk(c * _ROW_UNROLL, None)
        return carry
    lax.fori_loop(0, nfull, scatter_add, 0)

    @pl.when(tail > 0)
    def _():
        scatter_chunk(nfull * _ROW_UNROLL, tail)


def _experts(hrows, plan, wg, wu, wd, ts):
    step_e, step_cnt, step_first, tok, wrow = plan
    n_groups, list_len = tok.shape[0], tok.shape[2]
    n_steps = step_e.shape[0] // n_groups
    d = D_MODEL
    idx_blk = (1, 1, list_len)
    idx_map = lambda g, j, se, sc, sf: (g, 0, 0)
    w_map = lambda g, j, se, sc, sf: (se[g * n_steps + j], 0, 0)
    grid_spec = pltpu.PrefetchScalarGridSpec(
        num_scalar_prefetch=3,
        grid=(n_groups, n_steps),
        in_specs=[
            pl.BlockSpec(idx_blk, idx_map, memory_space=pltpu.SMEM),
            pl.BlockSpec(idx_blk, idx_map, memory_space=pltpu.SMEM),
            pl.BlockSpec(memory_space=pl.ANY),
            pl.BlockSpec((1, d, D_FF_EXPERT), w_map),
            pl.BlockSpec((1, d, D_FF_EXPERT), w_map),
            pl.BlockSpec((1, D_FF_EXPERT, d), w_map),
        ],
        out_specs=pl.BlockSpec(memory_space=pl.ANY),
        scratch_shapes=[
            pltpu.VMEM((ts * ROW_CHUNKS, LANES), F32),
            pltpu.VMEM(((ts + 1) * ROW_CHUNKS, LANES), F32),
            pltpu.VMEM((_STEP_ROWS * ROW_CHUNKS, LANES), F32),
            pltpu.VMEM((_STEP_ROWS * ROW_CHUNKS, LANES), F32),
            pltpu.SemaphoreType.DMA((2,)),
        ],
    )
    return pl.pallas_call(
        _expert_kernel,
        out_shape=jax.ShapeDtypeStruct(hrows.shape, F32),
        grid_spec=grid_spec,
        compiler_params=_cparams(("arbitrary", "arbitrary")),
        name="experts",
    )(step_e, step_cnt, step_first, tok, wrow, hrows, wg, wu, wd)


def _combine_kernel(x_ref, y_ref, gt_ref, o_ref):
    tm = x_ref.shape[0]
    gt = gt_ref[0]
    for c in range(ROW_CHUNKS):
        cols = slice(c * LANES, (c + 1) * LANES)
        o_ref[:, cols] = x_ref[:, cols] + gt[:, cols] * y_ref[pl.ds(c, tm, stride=ROW_CHUNKS), :]


def _combine(xn, yrows, gt, tm, tiles_per_seq):
    t, d = xn.shape
    nt = t // tm
    return pl.pallas_call(
        _combine_kernel,
        out_shape=jax.ShapeDtypeStruct((t, d), F32),
        grid=(nt,),
        in_specs=[
            pl.BlockSpec((tm, d), lambda i: (i, 0)),
            pl.BlockSpec((tm * ROW_CHUNKS, LANES), lambda i: (i, 0)),
            pl.BlockSpec((1, 1, d), lambda i: (i // tiles_per_seq, 0, 0)),
        ],
        out_specs=pl.BlockSpec((tm, d), lambda i: (i, 0)),
        compiler_params=_cparams(("arbitrary",)),
        name="moe_combine",
    )(xn, yrows, gt)


def _dispatch_plan(e_sel, w_sel, ts):
    t = e_sel.shape[0]
    n_groups = t // ts
    na = ts * TOP_K
    n_steps = na // _STEP_ROWS + N_EXPERTS
    e_flat = e_sel.reshape(n_groups, ts, TOP_K).transpose(0, 2, 1).reshape(n_groups, na)
    w_flat = w_sel.reshape(n_groups, ts, TOP_K).transpose(0, 2, 1).reshape(n_groups, na)
    a_ids = jnp.broadcast_to(jnp.arange(na, dtype=jnp.int32)[None, :], (n_groups, na))
    _, a_sorted, w_sorted = lax.sort((e_flat, a_ids, w_flat), dimension=1, num_keys=1)
    experts = jnp.arange(N_EXPERTS, dtype=jnp.int32)
    counts = jnp.sum((e_flat[:, :, None] == experts[None, None, :]).astype(jnp.int32), axis=1)
    padded = (counts + _STEP_ROWS - 1) // _STEP_ROWS * _STEP_ROWS
    pad_end = jnp.cumsum(padded, axis=1)
    pad_start = pad_end - padded
    start = jnp.cumsum(counts, axis=1) - counts
    row0 = jnp.arange(n_steps, dtype=jnp.int32) * _STEP_ROWS
    step_e = jnp.minimum(
        jnp.sum((row0[None, :, None] >= pad_end[:, None, :]).astype(jnp.int32), axis=2),
        N_EXPERTS - 1)
    is_e = step_e[:, :, None] == experts[None, None, :]
    take = lambda table: jnp.sum(jnp.where(is_e, table[:, None, :], 0), axis=2)
    off = row0[None, :] - take(pad_start)
    step_cnt = jnp.clip(take(counts) - off, 0, _STEP_ROWS)
    step_first = jnp.clip(take(start) + off, 0, na - 1)
    flat = lambda v: v.astype(jnp.int32).reshape(n_groups * n_steps)
    filler = jnp.zeros((n_groups, LANES), jnp.int32)
    tok_rows = jnp.concatenate([(a_sorted % ts) * ROW_CHUNKS, filler], axis=1).astype(jnp.int32)
    w_list = jnp.concatenate([w_sorted.astype(F32), filler.astype(F32)], axis=1)
    return (flat(step_e), flat(step_cnt), flat(step_first),
            tok_rows.reshape(n_groups, 1, na + LANES), w_list.reshape(n_groups, 1, na + LANES))


def _tile_sizes(bsz, seq):
    tm = min(512, seq)
    tm_merge = min(1024, seq)
    tq = min(256, seq)
    ts = min(4096, bsz * seq)
    assert seq % tm == 0 and seq % tm_merge == 0 and seq % tq == 0
    assert (bsz * seq) % ts == 0 and ts % tm_merge == 0
    return tm, tm_merge, tq, ts


def kernel(x, c, w_ada, b_ada, g_mix, w_in, qn_a, kn_a, qn_b, kn_b, w_pa, w_pb, w_out,
           g_ffn, w_router, w_gate, w_up, w_down):
    bsz, seq, d = x.shape
    depth = w_ada.shape[0]
    t = bsz * seq
    tm, tm_merge, tq, ts = _tile_sizes(bsz, seq)
    tiles_per_seq = seq // tm

    mod = _adaln(c, w_ada, b_ada)
    tables = _rope_tables(seq)
    bd_heads = 2 * LANES // HEAD_DIM
    bd = jnp.asarray(np.kron(np.eye(bd_heads), np.ones((HEAD_DIM, HEAD_DIM))), BF16)

    g4 = B_Q_HEADS // B_KV_HEADS
    head_perm = np.concatenate(
        [np.r_[np.arange(j * HEAD_DIM, (j + 1) * HEAD_DIM),
               np.arange((j + g4) * HEAD_DIM, (j + g4 + 1) * HEAD_DIM)] for j in range(g4)])
    qb_lo = 3 * A_WIDTH
    col_perm = np.arange(w_in.shape[2])
    col_perm[qb_lo:qb_lo + B_Q_WIDTH] = qb_lo + head_perm
    scale = HEAD_DIM ** -0.5 * float(np.log2(np.e))
    wr_t = w_router.T

    xf = x
    prev_moe = None
    for l in range(depth):
        sh1, sc1, gt1, sh2, sc2, gt2 = [
            mod[l, :, i * d:(i + 1) * d].reshape(bsz, 1, d) for i in range(6)]
        w_in_l = w_in[l][:, col_perm].astype(BF16)
        gains = (
            jnp.tile(qn_a[l] * scale, A_HEADS).reshape(1, A_WIDTH),
            jnp.tile(kn_a[l], A_HEADS).reshape(1, A_WIDTH),
            jnp.tile(qn_b[l] * scale, B_Q_HEADS).reshape(1, B_Q_WIDTH),
            jnp.tile(kn_b[l], B_KV_HEADS).reshape(1, B_KV_WIDTH),
        )
        outs = _in_proj(xf, prev_moe, sh1, sc1, g_mix[l].reshape(1, d), w_in_l, bd, gains,
                        tables, tm)
        za, qb, kb, vb, ga, gb = outs[:6]
        if prev_moe is not None:
            xf = outs[6]
        ya = _dilated(za)
        yb = _gqa(qb, kb, vb, tq)
        xn, hrows, e_sel, w_sel = _merge(
            xf.reshape(t, d), ya.reshape(t, A_WIDTH), yb.reshape(t, B_Q_WIDTH),
            ga.reshape(t, d), gb.reshape(t, d),
            w_pa[l].astype(BF16), w_pb[l][head_perm].astype(BF16), w_out[l].astype(BF16),
            gt1, g_ffn[l].reshape(1, d), sh2, sc2, wr_t, tm_merge, seq // tm_merge)
        e_tok = e_sel.transpose(0, 2, 1).reshape(t, TOP_K)
        w_tok = w_sel.transpose(0, 2, 1).reshape(t, TOP_K)
        plan = _dispatch_plan(e_tok, w_tok, ts)
        yrows = _experts(hrows, plan, w_gate[l].astype(BF16), w_up[l].astype(BF16),
                         w_down[l].astype(BF16), ts)
        xf = xn.reshape(bsz, seq, d)
        prev_moe = (yrows, gt2)
    out = _combine(xf.reshape(t, d), prev_moe[0], prev_moe[1], tm, tiles_per_seq)
    return out.reshape(bsz, seq, d)
```
